```python
import math
import jax, jax.numpy as jnp
from jax import lax
import numpy as np

D_MODEL = 2048
BATCH = 4
SEQ = 8192
DEPTH = 1

GRID_W = 64
CTX_LEN = 256
D_MIX = D_MODEL
W_LRU = D_MIX // 2
W_HY = D_MIX - W_LRU
IN_COLS = 2 * W_LRU + 3 * W_HY
LRU_BLOCKS = 8
LRU_BW = W_LRU // LRU_BLOCKS
LRU_CONV = 4
LRU_PAD = (2, 1)
LRU_C = 8.0
HY_ORDER = 2
HY_CONV = 3
HY_PAD = (1, 1)
HY_BANDS = 16
HY_EMB = 1 + 2 * HY_BANDS
HY_FH = 64
HY_TARGET = 1e-2
HY_FAST = 0.3
HY_SLOW = 1.5
HY_FILTER_STD = 0.02
N_EXPERTS = 32
TOP_K = 4
D_FF = D_MODEL
SWIGLU_LIMIT = 7.0
SWIGLU_ALPHA = 1.702
MOE_BLOCK = 256
N_MOD = 6
EPS = 1e-6

kernel_name = 'hybrid_rglru_hyena_moe_dit'

F32 = jnp.float32


def rmsnorm(x, g):
    xf = x.astype(F32)
    y = xf * lax.rsqrt(jnp.mean(xf * xf, axis=-1, keepdims=True) + EPS)
    return (y * g.astype(F32)).astype(x.dtype)


def dwconv(u, w, b, pad):
    y = lax.conv_general_dilated(u, w[:, None, :].astype(u.dtype), window_strides=(1,),
                                 padding=[pad], dimension_numbers=('NWC', 'WIO', 'NWC'),
                                 feature_group_count=u.shape[-1])
    return y + b.astype(u.dtype)


def to_col_major(u, rows):
    b, s, ch = u.shape
    return u.reshape(b, rows, GRID_W, ch).transpose(0, 2, 1, 3).reshape(b, s, ch)


def from_col_major(u, rows):
    b, s, ch = u.shape
    return u.reshape(b, GRID_W, rows, ch).transpose(0, 2, 1, 3).reshape(b, s, ch)


def rglru_coeffs(u, wa, ba, wi, bi, lam):
    bsz, L, w = u.shape
    ub = u.reshape(bsz, L, LRU_BLOCKS, LRU_BW)
    r = jax.nn.sigmoid((jnp.einsum('blnd,nde->blne', ub, wa).reshape(bsz, L, w) + ba).astype(F32))
    i = jax.nn.sigmoid((jnp.einsum('blnd,nde->blne', ub, wi).reshape(bsz, L, w) + bi).astype(F32))
    log_a = -LRU_C * r * jax.nn.softplus(-lam.astype(F32))
    a = jnp.exp(log_a)
    b = jnp.sqrt(-jnp.expm1(2.0 * log_a)) * (i * u.astype(F32))
    return a, b


def _lin_combine(e1, e2):
    a1, b1 = e1
    a2, b2 = e2
    return a1 * a2, a2 * b1 + b2


def linear_scan(a, b, h0, reverse):
    a_cum, b_cum = lax.associative_scan(_lin_combine, (a, b), axis=1, reverse=reverse)
    return a_cum * h0[:, None, :] + b_cum


def rglru_group(r_c, r_x, g_c, g_x, need_ctx, conv_w, conv_b, wa, ba, wi, bi, lam):
    u_c = dwconv(r_c, conv_w, conv_b, LRU_PAD)
    u_x = dwconv(r_x, conv_w, conv_b, LRU_PAD)
    hs_c, hs_x = [], []
    for d, rev in enumerate((False, True)):
        a_c, b_c = rglru_coeffs(u_c, wa[d], ba[d], wi[d], bi[d], lam[d])
        h_c = linear_scan(a_c, b_c, jnp.zeros_like(b_c[:, 0]), rev)
        h_last = h_c[:, 0] if rev else h_c[:, -1]
        a_x, b_x = rglru_coeffs(u_x, wa[d], ba[d], wi[d], bi[d], lam[d])
        hs_x.append(linear_scan(a_x, b_x, h_last, rev))
        hs_c.append(h_c)
    y_x = ((hs_x[0] + hs_x[1]) * jax.nn.gelu(g_x.astype(F32))).astype(r_x.dtype)
    if not need_ctx:
        return None, y_x
    y_c = ((hs_c[0] + hs_c[1]) * jax.nn.gelu(g_c.astype(F32))).astype(r_c.dtype)
    return y_c, y_x


def hyena_filters(L, w1, b1, w2, b2, w3, b3, freq, w4):
    t = jnp.linspace(0.0, 1.0, L, dtype=F32)[:, None]
    w = 2.0 * math.pi * jnp.arange(L, dtype=F32)[:, None] / L
    f = jnp.linspace(1e-4, HY_BANDS - 1, HY_BANDS, dtype=F32)[None, :]
    z = jnp.concatenate([t, jnp.cos(f * w), -jnp.sin(f * w)], axis=-1)
    fr = freq.astype(F32)
    hdn = jnp.sin(fr * (z @ w1.astype(F32) + b1.astype(F32)))
    hdn = jnp.sin(fr * (hdn @ w2.astype(F32) + b2.astype(F32)))
    hdn = jnp.sin(fr * (hdn @ w3.astype(F32) + b3.astype(F32)))
    h = (hdn @ w4.astype(F32)).reshape(L, HY_ORDER, 2, W_HY)
    max_decay = math.log(HY_TARGET) / HY_FAST
    min_decay = math.log(HY_TARGET) / HY_SLOW
    deltas = jnp.abs(jnp.linspace(min_decay, max_decay, W_HY, dtype=F32))
    decay = jnp.exp(-t * deltas[None, :])
    return h * decay[:, None, None, :]


def bidir_long_conv(u, h_fwd, h_bwd, d):
    L = u.shape[1]
    k = jnp.concatenate([h_fwd, jnp.zeros_like(h_fwd[:1]), h_bwd[:0:-1]], axis=0)
    kf = jnp.fft.rfft(k, axis=0)
    uf = jnp.fft.rfft(u.astype(F32), n=2 * L, axis=1)
    y = jnp.fft.irfft(uf * kf[None], n=2 * L, axis=1)[:, :L]
    return (y + u.astype(F32) * d.astype(F32)).astype(u.dtype)


def hyena_group(p, conv_w, conv_b, filt, hy_d):
    u = dwconv(p, conv_w, conv_b, HY_PAD)
    v, x1, x2 = jnp.split(u, 3, axis=-1)
    z = x1 * bidir_long_conv(v, filt[:, 0, 0], filt[:, 0, 1], hy_d[0])
    z = x2 * bidir_long_conv(z, filt[:, 1, 0], filt[:, 1, 1], hy_d[1])
    return z


def token_mixer(a_c, a_x, need_ctx, w_in, lru_conv_w, lru_conv_b, lru_wa, lru_ba, lru_wi, lru_bi,
                lru_lambda, hy_conv_w, hy_conv_b, hy_fparams, hy_d, gn_lru, gn_hy, w_out):
    seq = a_x.shape[1]
    rows = seq // GRID_W
    p_c = a_c @ w_in
    p_x = a_x @ w_in
    lru_c, lru_x = rglru_group(p_c[..., :W_LRU], p_x[..., :W_LRU],
                               p_c[..., W_LRU:2 * W_LRU], p_x[..., W_LRU:2 * W_LRU], need_ctx,
                               lru_conv_w, lru_conv_b, lru_wa, lru_ba, lru_wi, lru_bi, lru_lambda)
    filt_x = hyena_filters(seq, *hy_fparams)
    hy_x = from_col_major(hyena_group(to_col_major(p_x[..., 2 * W_LRU:], rows), hy_conv_w, hy_conv_b,
                                      filt_x, hy_d), rows)
    out_x = jnp.concatenate([rmsnorm(lru_x, gn_lru), rmsnorm(hy_x, gn_hy)], axis=-1) @ w_out
    if not need_ctx:
        return None, out_x
    filt_c = hyena_filters(a_c.shape[1], *hy_fparams)
    hy_c = hyena_group(p_c[..., 2 * W_LRU:], hy_conv_w, hy_conv_b, filt_c, hy_d)
    out_c = jnp.concatenate([rmsnorm(lru_c, gn_lru), rmsnorm(hy_c, gn_hy)], axis=-1) @ w_out
    return out_c, out_x


def moe_ffn(h, router_w, router_b, w_gu, b_gu, w_down, b_down):
    n_tok = h.shape[0]
    n_assign = n_tok * TOP_K
    n_blocks = -(-n_assign // MOE_BLOCK) + N_EXPERTS
    cap = n_blocks * MOE_BLOCK
    logits = (h @ router_w + router_b).astype(F32)
    top_val, top_idx = lax.top_k(logits, TOP_K)
    gate = jax.nn.softmax(top_val, axis=-1)
    e_flat = top_idx.reshape(-1)
    tok_flat = jnp.arange(n_assign, dtype=jnp.int32) // TOP_K
    order = jnp.argsort(e_flat)
    e_sorted = e_flat[order]
    counts = jnp.bincount(e_flat, length=N_EXPERTS)
    starts = jnp.cumsum(counts) - counts
    padded = (counts + MOE_BLOCK - 1) // MOE_BLOCK * MOE_BLOCK
    pends = jnp.cumsum(padded)
    pstarts = pends - padded
    dest = pstarts[e_sorted] + jnp.arange(n_assign, dtype=jnp.int32) - starts[e_sorted]
    slot_tok = jnp.zeros((cap,), jnp.int32).at[dest].set(tok_flat[order])
    slot_gate = jnp.zeros((cap,), F32).at[dest].set(gate.reshape(-1)[order])
    block_start = jnp.arange(n_blocks, dtype=jnp.int32) * MOE_BLOCK
    block_expert = jnp.minimum(jnp.searchsorted(pends, block_start, side='right'), N_EXPERTS - 1)

    def run_block(args):
        tok, e = args
        xb = h[tok]
        gu = xb @ w_gu[e] + b_gu[e]
        x_glu = jnp.minimum(gu[:, :D_FF], SWIGLU_LIMIT)
        x_lin = jnp.clip(gu[:, D_FF:], -SWIGLU_LIMIT, SWIGLU_LIMIT)
        act = x_glu * jax.nn.sigmoid(SWIGLU_ALPHA * x_glu) * (x_lin + 1)
        return act @ w_down[e] + b_down[e]

    y = lax.map(run_block, (slot_tok.reshape(n_blocks, MOE_BLOCK), block_expert))
    y = y.reshape(cap, -1) * slot_gate[:, None].astype(y.dtype)
    return jax.ops.segment_sum(y, slot_tok, num_segments=n_tok)


def setup_inputs(seed: int = 0) -> dict:
    key = jax.random.key(seed)
    ks = iter(jax.random.split(key, 40))

    def nrm(shape, std):
        return jax.random.normal(next(ks), shape, F32) * std

    def gain(shape):
        return 1.0 + nrm(shape, 0.05)

    L_ = DEPTH
    a0 = jax.random.uniform(next(ks), (L_, 2, W_LRU), F32, 0.9, 0.999)
    s0 = a0 ** (1.0 / LRU_C)
    lam = jnp.log(s0) - jnp.log1p(-s0)
    return {
        'x': nrm((BATCH, SEQ, D_MODEL), 1.0),
        'c': nrm((BATCH, D_MODEL), 1.0),
        'ctx': nrm((BATCH, CTX_LEN, D_MODEL), 1.0),
        'c_ctx': nrm((D_MODEL,), 1.0),
        'w_mod': nrm((L_, D_MODEL, N_MOD * D_MODEL), 0.5 * D_MODEL ** -0.5),
        'b_mod': nrm((L_, N_MOD * D_MODEL), 0.01),
        'norm1_g': gain((L_, D_MODEL)),
        'norm2_g': gain((L_, D_MODEL)),
        'w_in': nrm((L_, D_MODEL, IN_COLS), D_MODEL ** -0.5),
        'lru_conv_w': nrm((L_, LRU_CONV, W_LRU), LRU_CONV ** -0.5),
        'lru_conv_b': nrm((L_, W_LRU), 0.01),
        'lru_wa': nrm((L_, 2, LRU_BLOCKS, LRU_BW, LRU_BW), LRU_BW ** -0.5),
        'lru_ba': nrm((L_, 2, W_LRU), 0.01),
        'lru_wi': nrm((L_, 2, LRU_BLOCKS, LRU_BW, LRU_BW), LRU_BW ** -0.5),
        'lru_bi': nrm((L_, 2, W_LRU), 0.01),
        'lru_lambda': lam,
        'hy_conv_w': nrm((L_, HY_CONV, 3 * W_HY), HY_CONV ** -0.5),
        'hy_conv_b': nrm((L_, 3 * W_HY), 0.01),
        'hy_w1': nrm((L_, HY_EMB, HY_FH), HY_EMB ** -0.5),
        'hy_b1': nrm((L_, HY_FH), 0.1),
        'hy_w2': nrm((L_, HY_FH, HY_FH), HY_FH ** -0.5),
        'hy_b2': nrm((L_, HY_FH), 0.1),
        'hy_w3': nrm((L_, HY_FH, HY_FH), HY_FH ** -0.5),
        'hy_b3': nrm((L_, HY_FH), 0.1),
        'hy_freq': gain((L_, HY_FH)),
        'hy_w4': nrm((L_, HY_FH, HY_ORDER * 2 * W_HY), HY_FILTER_STD * (2.0 / HY_FH) ** 0.5),
        'hy_d': nrm((L_, HY_ORDER, W_HY), 0.5),
        'gn_lru': gain((L_, W_LRU)),
        'gn_hy': gain((L_, W_HY)),
        'w_out': nrm((L_, D_MIX, D_MODEL), D_MIX ** -0.5),
        'router_w': nrm((L_, D_MODEL, N_EXPERTS), D_MODEL ** -0.5),
        'router_b': nrm((L_, N_EXPERTS), 0.01),
        'exp_w_gu': nrm((L_, N_EXPERTS, D_MODEL, 2 * D_FF), D_MODEL ** -0.5),
        'exp_b_gu': nrm((L_, N_EXPERTS, 2 * D_FF), 0.01),
        'exp_w_down': nrm((L_, N_EXPERTS, D_FF, D_MODEL), D_FF ** -0.5),
        'exp_b_down': nrm((L_, N_EXPERTS, D_MODEL), 0.01),
        'final_g': gain((D_MODEL,)),
    }


def reference(x, c, ctx, c_ctx, w_mod, b_mod, norm1_g, norm2_g, w_in, lru_conv_w, lru_conv_b,
              lru_wa, lru_ba, lru_wi, lru_bi, lru_lambda, hy_conv_w, hy_conv_b, hy_w1, hy_b1,
              hy_w2, hy_b2, hy_w3, hy_b3, hy_freq, hy_w4, hy_d, gn_lru, gn_hy, w_out,
              router_w, router_b, exp_w_gu, exp_b_gu, exp_w_down, exp_b_down, final_g):
    hx = x
    hc = ctx
    silu_c = jax.nn.silu(c)
    silu_cc = jax.nn.silu(c_ctx)
    for l in range(DEPTH):
        need_ctx = l < DEPTH - 1
        mod_x = (silu_c @ w_mod[l] + b_mod[l])[:, None, :]
        mod_c = (silu_cc @ w_mod[l] + b_mod[l])[None, None, :]
        sh1x, sc1x, g1x, sh2x, sc2x, g2x = jnp.split(mod_x, N_MOD, axis=-1)
        sh1c, sc1c, g1c, sh2c, sc2c, g2c = jnp.split(mod_c, N_MOD, axis=-1)
        a_x = rmsnorm(hx, norm1_g[l]) * (1 + sc1x) + sh1x
        a_c = rmsnorm(hc, norm1_g[l]) * (1 + sc1c) + sh1c
        hy_fparams = (hy_w1[l], hy_b1[l], hy_w2[l], hy_b2[l], hy_w3[l], hy_b3[l], hy_freq[l], hy_w4[l])
        out_c, out_x = token_mixer(a_c, a_x, need_ctx, w_in[l], lru_conv_w[l], lru_conv_b[l],
                                   lru_wa[l], lru_ba[l], lru_wi[l], lru_bi[l], lru_lambda[l],
                                   hy_conv_w[l], hy_conv_b[l], hy_fparams, hy_d[l],
                                   gn_lru[l], gn_hy[l], w_out[l])
        hx = hx + g1x * out_x
        m_x = rmsnorm(hx, norm2_g[l]) * (1 + sc2x) + sh2x
        hx = hx + g2x * moe_ffn(m_x.reshape(-1, D_MODEL), router_w[l], router_b[l], exp_w_gu[l],
                                exp_b_gu[l], exp_w_down[l], exp_b_down[l]).reshape(hx.shape)
        if need_ctx:
            hc = hc + g1c * out_c
            m_c = rmsnorm(hc, norm2_g[l]) * (1 + sc2c) + sh2c
            hc = hc + g2c * moe_ffn(m_c.reshape(-1, D_MODEL), router_w[l], router_b[l], exp_w_gu[l],
                                    exp_b_gu[l], exp_w_down[l], exp_b_down[l]).reshape(hc.shape)
    return rmsnorm(hx, final_g)
```

```python
import functools
import math

import jax
import jax.numpy as jnp
from jax import lax
from jax.experimental import pallas as pl
from jax.experimental.pallas import tpu as pltpu

F32 = jnp.float32
BF16 = jnp.bfloat16
I32 = jnp.int32
U32 = jnp.uint32
HI = lax.Precision.HIGHEST

GRID_W = 64
EPS = 1e-6
N_MOD = 6
LRU_CONV = 4
LRU_C = 8.0
HY_CONV = 3
HY_ORDER = 2
HY_BANDS = 16
HY_TARGET = 1e-2
HY_FAST = 0.3
HY_SLOW = 1.5
TOP_K = 4
SWIGLU_LIMIT = 7.0
SWIGLU_ALPHA = 1.702

LANES = 128
SUBLANES = 8
VMEM_LIMIT_BYTES = 56 * 1024 * 1024

DFT_N1 = 2 * GRID_W
K1_USED = DFT_N1 // 2 + 1
K1_PAD = 72
assert K1_PAD % SUBLANES == 0 and K1_PAD >= K1_USED


def _cparams(*sem):
    return pltpu.CompilerParams(dimension_semantics=sem, vmem_limit_bytes=VMEM_LIMIT_BYTES)


def _rms(x):
    return x * lax.rsqrt(jnp.mean(x * x, axis=-1, keepdims=True) + EPS)


def _mod_kernel(c_ref, w_ref, b_ref, o_ref):
    c = c_ref[...]
    s = c * jax.nn.sigmoid(c)
    o_ref[...] = jnp.dot(s, w_ref[...], preferred_element_type=F32, precision=HI) + b_ref[...]


def _mod_call(cstack, w_mod, b_mod):
    rows, d = cstack.shape
    n = w_mod.shape[1]
    tn = 1536 if n % 1536 == 0 else n
    return pl.pallas_call(
        _mod_kernel,
        out_shape=jax.ShapeDtypeStruct((rows, n), F32),
        grid=(n // tn,),
        in_specs=[pl.BlockSpec((rows, d), lambda j: (0, 0)),
                  pl.BlockSpec((d, tn), lambda j: (0, j)),
                  pl.BlockSpec((1, tn), lambda j: (0, j))],
        out_specs=pl.BlockSpec((rows, tn), lambda j: (0, j)),
        compiler_params=_cparams("arbitrary"),
        name="mod",
    )(cstack, w_mod, b_mod.reshape(1, n))


def _inproj_kernel(x_ref, mod_ref, g_ref, w_ref, o_ref, *, d, n_slabs, nchunk):
    x = x_ref[...]
    sh = mod_ref[0, :, 0 * d:1 * d]
    sc = mod_ref[0, :, 1 * d:2 * d]
    a = (_rms(x) * g_ref[...]) * (1.0 + sc) + sh
    ab = a.astype(BF16)
    spc = nchunk // LANES
    for j in range(n_slabs // spc):
        p = jnp.dot(ab, w_ref[:, j * nchunk:(j + 1) * nchunk], preferred_element_type=F32)
        for s in range(spc):
            o_ref[j * spc + s] = p[:, s * LANES:(s + 1) * LANES].astype(BF16)


def _inproj_call(x2d, mod3, mod_row_of_block, g, w_bf16, tm):
    t, d = x2d.shape
    n = w_bf16.shape[1]
    n_slabs = n // LANES
    nchunk = 512 if n % 512 == 0 else LANES
    kern = functools.partial(_inproj_kernel, d=d, n_slabs=n_slabs, nchunk=nchunk)
    return pl.pallas_call(
        kern,
        out_shape=jax.ShapeDtypeStruct((n_slabs, t, LANES), BF16),
        grid=(t // tm,),
        in_specs=[pl.BlockSpec((tm, d), lambda i: (i, 0)),
                  pl.BlockSpec((1, 1, N_MOD * d), lambda i: (mod_row_of_block(i), 0, 0)),
                  pl.BlockSpec((1, d), lambda i: (0, 0)),
                  pl.BlockSpec((d, n), lambda i: (0, 0), pipeline_mode=pl.Buffered(1))],
        out_specs=pl.BlockSpec((n_slabs, tm, LANES), lambda i: (0, i, 0)),
        compiler_params=_cparams("arbitrary"),
        name="inproj",
    )(x2d, mod3, g.reshape(1, d), w_bf16)


def _gelu_tanh(x):
    return 0.5 * x * (1.0 + jnp.tanh(math.sqrt(2.0 / math.pi) * (x + 0.044715 * (x * x * x))))


def _scan_tile(a, b, row, reverse):
    for s in (1, 2, 4):
        if reverse:
            m = row < SUBLANES - s
            sh = SUBLANES - s
        else:
            m = row >= s
            sh = s
        a_s = jnp.where(m, pltpu.roll(a, sh, 0), 1.0)
        b_s = jnp.where(m, pltpu.roll(b, sh, 0), 0.0)
        b = a * b_s + b
        a = a * a_s
    return a, b


def _lru_kernel(r_ref, g_ref, cw_ref, cb_ref, wa_ref, wi_ref, ba_ref, bi_ref, lam_ref, h0_ref,
                y_ref, hl_ref, xp_ref, hf_ref, ab_ref, bb_ref, *, t_len, tc):
    nc = t_len // tc
    ntile = tc // SUBLANES
    zero8 = jnp.zeros((SUBLANES, LANES), F32)
    xp_ref[0:SUBLANES, :] = zero8
    xp_ref[t_len + SUBLANES:t_len + 2 * SUBLANES, :] = zero8

    def copy_body(ci, c):
        t0 = pl.multiple_of(ci * tc, tc)
        xp_ref[pl.ds(t0 + SUBLANES, tc), :] = r_ref[0, pl.ds(t0, tc), :].astype(F32)
        return c
    lax.fori_loop(0, nc, copy_body, 0)

    row = lax.broadcasted_iota(I32, (SUBLANES, LANES), 0)
    cw = cw_ref[...]
    cb = cb_ref[...]
    lam = lam_ref[...]
    sp = jnp.log1p(jnp.exp(-lam))
    wab = [wa_ref[d, 0].astype(BF16) for d in range(2)]
    wib = [wi_ref[d, 0].astype(BF16) for d in range(2)]

    def coeffs(u, ub, d):
        ga = jnp.dot(ub, wab[d], preferred_element_type=F32) + ba_ref[d:d + 1, :]
        gi = jnp.dot(ub, wib[d], preferred_element_type=F32) + bi_ref[d:d + 1, :]
        rg = jax.nn.sigmoid(ga)
        ig = jax.nn.sigmoid(gi)
        a = jnp.exp((-LRU_C) * rg * sp[d:d + 1, :])
        b = jnp.sqrt(1.0 - a * a) * (ig * u)
        return a, b

    def fwd_body(ci, hcar):
        t0 = pl.multiple_of(ci * tc, tc)
        blk = xp_ref[pl.ds(t0, tc + 2 * SUBLANES), :]
        u = (cw[0:1, :] * blk[6:6 + tc] + cw[1:2, :] * blk[7:7 + tc]
             + cw[2:3, :] * blk[8:8 + tc] + cw[3:4, :] * blk[9:9 + tc]) + cb
        ub = u.astype(BF16)
        a0, b0 = coeffs(u, ub, 0)
        a1, b1 = coeffs(u, ub, 1)
        ab_ref[pl.ds(t0, tc), :] = a1
        bb_ref[pl.ds(t0, tc), :] = b1
        for j in range(ntile):
            at, bt = _scan_tile(a0[j * 8:(j + 1) * 8], b0[j * 8:(j + 1) * 8], row, False)
            hf_ref[pl.ds(t0 + j * 8, 8), :] = at * hcar + bt
            atot = jnp.broadcast_to(at[7:8, :], (SUBLANES, LANES))
            btot = jnp.broadcast_to(bt[7:8, :], (SUBLANES, LANES))
            hcar = atot * hcar + btot
        return hcar

    h0f = jnp.broadcast_to(h0_ref[0, 0:1, :], (SUBLANES, LANES))
    hfin = lax.fori_loop(0, nc, fwd_body, h0f)
    hl_ref[0, 0:1, :] = hfin[0:1, :]

    def bwd_body(k, hcar):
        ci = nc - 1 - k
        t0 = pl.multiple_of(ci * tc, tc)
        a1 = ab_ref[pl.ds(t0, tc), :]
        b1 = bb_ref[pl.ds(t0, tc), :]
        gg = g_ref[0, pl.ds(t0, tc), :].astype(F32)
        gl = _gelu_tanh(gg)
        for j in reversed(range(ntile)):
            at, bt = _scan_tile(a1[j * 8:(j + 1) * 8], b1[j * 8:(j + 1) * 8], row, True)
            hb = at * hcar + bt
            hf = hf_ref[pl.ds(t0 + j * 8, 8), :]
            y_ref[0, pl.ds(t0 + j * 8, 8), :] = ((hf + hb) * gl[j * 8:(j + 1) * 8]).astype(y_ref.dtype)
            atot = jnp.broadcast_to(at[0:1, :], (SUBLANES, LANES))
            btot = jnp.broadcast_to(bt[0:1, :], (SUBLANES, LANES))
            hcar = atot * hcar + btot
        return hcar

    h0b = jnp.broadcast_to(h0_ref[0, 1:2, :], (SUBLANES, LANES))
    hfin_b = lax.fori_loop(0, nc, bwd_body, h0b)
    hl_ref[0, 1:2, :] = hfin_b[0:1, :]


def _lru_call(p_slabs, nb, n_batch, t_len, conv_w, conv_b, wa, ba, wi, bi, lam, h0):
    tc = 128 if t_len % 128 == 0 else t_len
    kern = functools.partial(_lru_kernel, t_len=t_len, tc=tc)
    w_lru = nb * LANES
    return pl.pallas_call(
        kern,
        out_shape=[jax.ShapeDtypeStruct((nb, n_batch * t_len, LANES), F32),
                   jax.ShapeDtypeStruct((n_batch, 2, w_lru), F32)],
        grid=(nb, n_batch),
        in_specs=[pl.BlockSpec((1, t_len, LANES), lambda n, b: (n, b, 0)),
                  pl.BlockSpec((1, t_len, LANES), lambda n, b: (nb + n, b, 0)),
                  pl.BlockSpec((LRU_CONV, LANES), lambda n, b: (0, n)),
                  pl.BlockSpec((1, LANES), lambda n, b: (0, n)),
                  pl.BlockSpec((2, 1, LANES, LANES), lambda n, b: (0, n, 0, 0)),
                  pl.BlockSpec((2, 1, LANES, LANES), lambda n, b: (0, n, 0, 0)),
                  pl.BlockSpec((2, LANES), lambda n, b: (0, n)),
                  pl.BlockSpec((2, LANES), lambda n, b: (0, n)),
                  pl.BlockSpec((2, LANES), lambda n, b: (0, n)),
                  pl.BlockSpec((1, 2, LANES), lambda n, b: (b, 0, n))],
        out_specs=[pl.BlockSpec((1, t_len, LANES), lambda n, b: (n, b, 0)),
                   pl.BlockSpec((1, 2, LANES), lambda n, b: (b, 0, n))],
        scratch_shapes=[pltpu.VMEM((t_len + 2 * SUBLANES, LANES), F32),
                        pltpu.VMEM((t_len, LANES), F32),
                        pltpu.VMEM((t_len, LANES), F32),
                        pltpu.VMEM((t_len, LANES), F32)],
        compiler_params=_cparams("arbitrary", "arbitrary"),
        name="lru",
    )(p_slabs, p_slabs, conv_w, conv_b.reshape(1, w_lru), wa, wi, ba, bi, lam, h0)


def _filt_kernel(w1_ref, b1_ref, w2_ref, b2_ref, w3_ref, b3_ref, fr_ref, w4_ref, o_ref,
                 *, l_len, rows, tl, w_hy):
    i = pl.program_id(0)
    rho = i * tl + lax.broadcasted_iota(I32, (tl, 1), 0)
    tidx = ((rho % GRID_W) * rows + rho // GRID_W).astype(F32)
    tt = tidx * (1.0 / (l_len - 1))
    ww = (2.0 * math.pi) * tidx / l_len
    band = lax.broadcasted_iota(I32, (1, HY_BANDS), 1).astype(F32)
    f = 1e-4 + band * ((HY_BANDS - 1 - 1e-4) / (HY_BANDS - 1))
    fw = f * ww
    w1 = w1_ref[...]
    pre = (tt * w1[0:1, :]
           + jnp.dot(jnp.cos(fw), w1[1:1 + HY_BANDS, :], preferred_element_type=F32, precision=HI)
           + jnp.dot(-jnp.sin(fw), w1[1 + HY_BANDS:1 + 2 * HY_BANDS, :], preferred_element_type=F32, precision=HI))
    fr = fr_ref[...]
    h = jnp.sin(fr * (pre + b1_ref[...]))
    h = jnp.sin(fr * (jnp.dot(h, w2_ref[...], preferred_element_type=F32, precision=HI) + b2_ref[...]))
    h = jnp.sin(fr * (jnp.dot(h, w3_ref[...], preferred_element_type=F32, precision=HI) + b3_ref[...]))
    max_decay = math.log(HY_TARGET) / HY_FAST
    min_decay = math.log(HY_TARGET) / HY_SLOW
    nblk = w_hy // LANES
    for cb in range(nblk):
        ch = (cb * LANES + lax.broadcasted_iota(I32, (1, LANES), 1)).astype(F32)
        delta = jnp.abs(min_decay + ch * ((max_decay - min_decay) / (w_hy - 1)))
        decay = jnp.exp(-tt * delta)
        for od in range(HY_ORDER * 2):
            col = od * w_hy + cb * LANES
            v = jnp.dot(h, w4_ref[:, col:col + LANES], preferred_element_type=F32, precision=HI)
            o_ref[od * nblk + cb] = v * decay


def _filt_call(l_len, w_hy, w1, b1, w2, b2, w3, b3, freq, w4):
    rows = l_len // GRID_W
    tl = 512 if l_len % 512 == 0 else l_len
    fh = w2.shape[0]
    nslab = HY_ORDER * 2 * (w_hy // LANES)
    kern = functools.partial(_filt_kernel, l_len=l_len, rows=rows, tl=tl, w_hy=w_hy)
    full = lambda a: pl.BlockSpec(a.shape, lambda i: (0,) * a.ndim)
    args = (w1, b1.reshape(1, fh), w2, b2.reshape(1, fh), w3, b3.reshape(1, fh), freq.reshape(1, fh), w4)
    return pl.pallas_call(
        kern,
        out_shape=jax.ShapeDtypeStruct((nslab, l_len, LANES), F32),
        grid=(l_len // tl,),
        in_specs=[full(a) for a in args],
        out_specs=pl.BlockSpec((nslab, tl, LANES), lambda i: (0, i, 0)),
        compiler_params=_cparams("arbitrary"),
        name="filt",
    )(*args)


def _dft_tables(rows):
    n_len = DFT_N1 * rows
    k1 = jnp.arange(K1_PAD, dtype=I32)
    r = jnp.arange(rows, dtype=I32)
    w = jnp.arange(GRID_W, dtype=I32)
    idx = (k1[None, :, None] * (r[:, None, None] + rows * w[None, None, :])) % n_len
    ang = idx.astype(F32) * (2.0 * math.pi / n_len)
    valid = (k1 < K1_USED)[None, :, None]
    ca = jnp.where(valid, jnp.cos(ang), 0.0)
    sa = jnp.where(valid, jnp.sin(ang), 0.0)
    ta = jnp.stack([ca, -sa], axis=2).reshape(rows, 2 * K1_PAD, GRID_W)
    td = jnp.transpose(ta, (0, 2, 1))
    k2 = jnp.arange(rows, dtype=I32)
    th = ((k2[:, None] * r[None, :]) % rows).astype(F32) * (2.0 * math.pi / rows)
    c, s = jnp.cos(th), jnp.sin(th)
    tb = jnp.concatenate([jnp.stack([c, s], axis=2).reshape(rows, 2 * rows),
                          jnp.stack([-s, c], axis=2).reshape(rows, 2 * rows)], axis=0)
    tcm = jnp.stack([jnp.concatenate([c, -s], axis=1), jnp.concatenate([s, c], axis=1)], axis=1).reshape(2 * rows, 2 * rows)
    return ta.astype(BF16), tb.astype(BF16), tcm.astype(BF16), td.astype(BF16)


def _pitch_r(rows):
    return rows + SUBLANES


def _stage_a(tile_fn, ta_ref, s1_ref, rows):
    pitch = _pitch_r(rows)

    def body(r, c):
        out = jnp.dot(ta_ref[r], tile_fn(r), preferred_element_type=F32)
        words = pltpu.bitcast(out.astype(BF16), U32)
        s1_ref[pl.ds(r, K1_PAD, stride=pitch), :] = words
        return c
    lax.fori_loop(0, rows, body, 0)


def _stage_b(k1, tb_ref, s1_ref, rows):
    pitch = _pitch_r(rows)
    t = s1_ref[pl.ds(pl.multiple_of(k1 * pitch, SUBLANES), rows), :]
    y = jnp.dot(tb_ref[...], pltpu.bitcast(t, BF16), preferred_element_type=F32)
    return y[:rows], y[rows:]


def _spec_kernel(hf_ref, hb_ref, ta_ref, tb_ref, kf_ref, s1_ref, *, rows):
    n_len = DFT_N1 * rows

    def scale(k1):
        return jnp.where((k1 == 0) | (k1 == K1_USED - 1), 1.0, 2.0) * (1.0 / n_len)

    _stage_a(lambda r: hf_ref[0, pl.ds(pl.multiple_of(r * GRID_W, GRID_W), GRID_W), :].astype(BF16),
             ta_ref, s1_ref, rows)

    def b_fwd(k1, c):
        yre, yim = _stage_b(k1, tb_ref, s1_ref, rows)
        sc = scale(k1)
        kf_ref[0, 0, k1, 0] = yre * sc
        kf_ref[0, 0, k1, 1] = yim * sc
        return c
    lax.fori_loop(0, K1_USED, b_fwd, 0)

    rid = lax.broadcasted_iota(I32, (GRID_W, LANES), 0)

    def tile_b(r):
        x = hb_ref[0, pl.ds(pl.multiple_of(r * GRID_W, GRID_W), GRID_W), :]
        x = jnp.where((rid == 0) & (r == 0), 0.0, x)
        return x.astype(BF16)
    _stage_a(tile_b, ta_ref, s1_ref, rows)

    def b_bwd(k1, c):
        yre, yim = _stage_b(k1, tb_ref, s1_ref, rows)
        sc = scale(k1)
        kf_ref[0, 0, k1, 0] += yre * sc
        kf_ref[0, 0, k1, 1] -= yim * sc
        return c
    lax.fori_loop(0, K1_USED, b_bwd, 0)


def _spec_call(hfilt, l_len, nblk, ta, tb):
    rows = l_len // GRID_W
    kern = functools.partial(_spec_kernel, rows=rows)
    return pl.pallas_call(
        kern,
        out_shape=jax.ShapeDtypeStruct((HY_ORDER, nblk, K1_USED, 2, rows, LANES), F32),
        grid=(HY_ORDER, nblk),
        in_specs=[pl.BlockSpec((1, l_len, LANES), lambda o, c: (o * 2 * nblk + c, 0, 0)),
                  pl.BlockSpec((1, l_len, LANES), lambda o, c: (o * 2 * nblk + nblk + c, 0, 0)),
                  pl.BlockSpec(ta.shape, lambda o, c: (0, 0, 0), pipeline_mode=pl.Buffered(1)),
                  pl.BlockSpec(tb.shape, lambda o, c: (0, 0), pipeline_mode=pl.Buffered(1))],
        out_specs=pl.BlockSpec((1, 1, K1_USED, 2, rows, LANES), lambda o, c: (o, c, 0, 0, 0, 0)),
        scratch_shapes=[pltpu.VMEM((K1_PAD * _pitch_r(rows), LANES), U32)],
        compiler_params=_cparams("arbitrary", "arbitrary"),
        name="spec",
    )(hfilt, hfilt, ta, tb)


def _conv_grid_order(src_ref, dst_ref, w_ref, b_ref, rows):
    w0, w1, w2, bb = w_ref[0:1, :], w_ref[1:2, :], w_ref[2:3, :], b_ref[...]
    g = GRID_W
    rid = lax.broadcasted_iota(I32, (g, LANES), 0)

    def grp(r0):
        return src_ref[0, pl.ds(r0, g), :].astype(F32)

    last = grp((rows - 1) * g)
    prev0 = jnp.where(rid >= 1, pltpu.roll(last, 1, 0), 0.0)
    dst_ref[pl.ds(0, g), :] = (w0 * prev0 + w1 * grp(0) + w2 * grp(g) + bb).astype(dst_ref.dtype)
    first = grp(0)
    nxt = jnp.where(rid < g - 1, pltpu.roll(first, g - 1, 0), 0.0)
    dst_ref[pl.ds((rows - 1) * g, g), :] = (w0 * grp((rows - 2) * g) + w1 * last + w2 * nxt + bb).astype(dst_ref.dtype)

    def body(r, c):
        rm = pl.multiple_of((r - 1) * g, g)
        r0 = pl.multiple_of(r * g, g)
        rp = pl.multiple_of((r + 1) * g, g)
        v = w0 * grp(rm) + w1 * grp(r0) + w2 * grp(rp) + bb
        dst_ref[pl.ds(r0, g), :] = v.astype(dst_ref.dtype)
        return c
    lax.fori_loop(1, rows - 1, body, 0)


def _hyena_kernel(sig_ref, gate_ref, cws_ref, cbs_ref, cwg_ref, cbg_ref, d_ref, kf_ref,
                  ta_ref, tb_ref, tc_ref, td_ref, o_ref, u_ref, gt_ref, s1_ref, s2_ref,
                  *, rows, conv_sig):
    g = GRID_W

    @pl.when((pl.program_id(0) == 0) & (pl.program_id(1) == 0))
    def _():
        s2_ref[...] = jnp.zeros(s2_ref.shape, U32)

    if conv_sig:
        _conv_grid_order(sig_ref, u_ref, cws_ref, cbs_ref, rows)
    else:
        def cp(r, c):
            r0 = pl.multiple_of(r * g, g)
            u_ref[pl.ds(r0, g), :] = sig_ref[0, pl.ds(r0, g), :]
            return c
        lax.fori_loop(0, rows, cp, 0)
    _conv_grid_order(gate_ref, gt_ref, cwg_ref, cbg_ref, rows)

    _stage_a(lambda r: u_ref[pl.ds(pl.multiple_of(r * g, g), g), :], ta_ref, s1_ref, rows)

    def bc_body(k1, c):
        yre, yim = _stage_b(k1, tb_ref, s1_ref, rows)
        kre = kf_ref[0, 0, k1, 0]
        kim = kf_ref[0, 0, k1, 1]
        zre = yre * kre - yim * kim
        zim = yre * kim + yim * kre
        z = jnp.concatenate([zre, zim], axis=0).astype(BF16)
        v = jnp.dot(tc_ref[...], z, preferred_element_type=F32)
        s2_ref[pl.ds(k1, rows, stride=K1_PAD), :] = pltpu.bitcast(v.astype(BF16), U32)
        return c
    lax.fori_loop(0, K1_USED, bc_body, 0)

    dd = d_ref[...]

    def d_body(r, c):
        r0 = pl.multiple_of(r * g, g)
        t = s2_ref[pl.ds(pl.multiple_of(r * K1_PAD, SUBLANES), K1_PAD), :]
        y = jnp.dot(td_ref[r], pltpu.bitcast(t, BF16), preferred_element_type=F32)
        uu = u_ref[pl.ds(r0, g), :].astype(F32)
        o_ref[0, pl.ds(r0, g), :] = (gt_ref[pl.ds(r0, g), :] * (y + dd * uu)).astype(o_ref.dtype)
        return c
    lax.fori_loop(0, rows, d_body, 0)


def _hyena_call(sig_arr, sig_slab0, gate_arr, gate_slab0, conv_w_sig, conv_b_sig, conv_w_gate, conv_b_gate,
                d_vec, kf, order, tables, n_batch, l_len, nblk, conv_sig):
    rows = l_len // GRID_W
    ta, tb, tcm, td = tables
    kern = functools.partial(_hyena_kernel, rows=rows, conv_sig=conv_sig)
    w_hy = nblk * LANES
    const = lambda a: pl.BlockSpec(a.shape, lambda c, b: (0,) * a.ndim, pipeline_mode=pl.Buffered(1))
    return pl.pallas_call(
        kern,
        out_shape=jax.ShapeDtypeStruct((nblk, n_batch * l_len, LANES), BF16),
        grid=(nblk, n_batch),
        in_specs=[pl.BlockSpec((1, l_len, LANES), lambda c, b: (sig_slab0 + c, b, 0)),
                  pl.BlockSpec((1, l_len, LANES), lambda c, b: (gate_slab0 + c, b, 0)),
                  pl.BlockSpec((HY_CONV, LANES), lambda c, b: (0, c)),
                  pl.BlockSpec((1, LANES), lambda c, b: (0, c)),
                  pl.BlockSpec((HY_CONV, LANES), lambda c, b: (0, c)),
                  pl.BlockSpec((1, LANES), lambda c, b: (0, c)),
                  pl.BlockSpec((1, LANES), lambda c, b: (0, c)),
                  pl.BlockSpec((1, 1, K1_USED, 2, rows, LANES), lambda c, b: (order, c, 0, 0, 0, 0),
                               pipeline_mode=pl.Buffered(1)),
                  const(ta), const(tb), const(tcm), const(td)],
        out_specs=pl.BlockSpec((1, l_len, LANES), lambda c, b: (c, b, 0)),
        scratch_shapes=[pltpu.VMEM((l_len, LANES), BF16),
                        pltpu.VMEM((l_len, LANES), F32),
                        pltpu.VMEM((K1_PAD * _pitch_r(rows), LANES), U32),
                        pltpu.VMEM((rows * K1_PAD, LANES), U32)],
        compiler_params=_cparams("arbitrary", "arbitrary"),
        name="hyena%d" % order,
    )(sig_arr, gate_arr, conv_w_sig, conv_b_sig.reshape(1, w_hy), conv_w_gate, conv_b_gate.reshape(1, w_hy),
      d_vec.reshape(1, w_hy), kf, ta, tb, tcm, td)


def _outproj_kernel(yl_ref, hy_ref, x_ref, mod_ref, gl_ref, gh_ref, n2_ref, w_ref, rw_ref, rb_ref,
                    hx_ref, m_ref, lg_ref, *, d, nb_l, nb_h):
    yl = jnp.concatenate([yl_ref[j].astype(F32) for j in range(nb_l)], axis=-1)
    hy = jnp.concatenate([hy_ref[j].astype(F32) for j in range(nb_h)], axis=-1)
    cat = jnp.concatenate([_rms(yl) * gl_ref[...], _rms(hy) * gh_ref[...]], axis=-1).astype(BF16)
    out = jnp.dot(cat, w_ref[...], preferred_element_type=F32)
    g1 = mod_ref[0, :, 2 * d:3 * d]
    sh2 = mod_ref[0, :, 3 * d:4 * d]
    sc2 = mod_ref[0, :, 4 * d:5 * d]
    hx = x_ref[...] + g1 * out
    hx_ref[...] = hx
    m = (_rms(hx) * n2_ref[...]) * (1.0 + sc2) + sh2
    m_ref[...] = m
    lg_ref[...] = jnp.dot(m, rw_ref[...], preferred_element_type=F32, precision=HI) + rb_ref[...]


def _outproj_call(y_lru, hy, x2d, mod3, rows_per_batch, gn_lru, gn_hy, norm2_g, w_out_bf16, router_w, router_b, tm):
    t, d = x2d.shape
    nb_l, nb_h = y_lru.shape[0], hy.shape[0]
    ne = router_w.shape[1]
    bpb = rows_per_batch // tm
    kern = functools.partial(_outproj_kernel, d=d, nb_l=nb_l, nb_h=nb_h)
    return pl.pallas_call(
        kern,
        out_shape=[jax.ShapeDtypeStruct((t, d), F32), jax.ShapeDtypeStruct((t, d), F32),
                   jax.ShapeDtypeStruct((t, ne), F32)],
        grid=(t // tm,),
        in_specs=[pl.BlockSpec((nb_l, tm, LANES), lambda i: (0, i, 0)),
                  pl.BlockSpec((nb_h, tm, LANES), lambda i: (0, i, 0)),
                  pl.BlockSpec((tm, d), lambda i: (i, 0)),
                  pl.BlockSpec((1, 1, N_MOD * d), lambda i: (i // bpb, 0, 0)),
                  pl.BlockSpec((1, nb_l * LANES), lambda i: (0, 0)),
                  pl.BlockSpec((1, nb_h * LANES), lambda i: (0, 0)),
                  pl.BlockSpec((1, d), lambda i: (0, 0)),
                  pl.BlockSpec(w_out_bf16.shape, lambda i: (0, 0), pipeline_mode=pl.Buffered(1)),
                  pl.BlockSpec((d, ne), lambda i: (0, 0)),
                  pl.BlockSpec((1, ne), lambda i: (0, 0))],
        out_specs=[pl.BlockSpec((tm, d), lambda i: (i, 0)),
                   pl.BlockSpec((tm, d), lambda i: (i, 0)),
                   pl.BlockSpec((tm, ne), lambda i: (i, 0))],
        compiler_params=_cparams("arbitrary"),
        name="outproj",
    )(y_lru, hy, x2d, mod3, gn_lru.reshape(1, -1), gn_hy.reshape(1, -1), norm2_g.reshape(1, d),
      w_out_bf16, router_w, router_b.reshape(1, ne))


def _route_kernel(lg_ref, idx_ref, gate_ref, rank_ref, cnt_ref, run_ref, *, tb, ne):
    @pl.when(pl.program_id(0) == 0)
    def _():
        run_ref[...] = jnp.zeros(run_ref.shape, F32)

    l = lg_ref[...]
    lane = lax.broadcasted_iota(I32, (tb, ne), 1)
    vals, idxs, ohs = [], [], []
    for _ in range(TOP_K):
        m = jnp.max(l, axis=-1, keepdims=True)
        ix = jnp.min(jnp.where(l == m, lane, ne), axis=-1, keepdims=True)
        sel = lane == ix
        vals.append(m)
        idxs.append(ix)
        ohs.append(sel.astype(F32))
        l = jnp.where(sel, -jnp.inf, l)
    es = [jnp.exp(v - vals[0]) for v in vals]
    den = es[0] + es[1] + es[2] + es[3]
    oh_all = ohs[0] + ohs[1] + ohs[2] + ohs[3]
    ri = lax.broadcasted_iota(I32, (tb, tb), 0)
    ci = lax.broadcasted_iota(I32, (tb, tb), 1)
    ltri = (ci < ri).astype(BF16)
    before = jnp.dot(ltri, oh_all.astype(BF16), preferred_element_type=F32) + run_ref[...]
    for k in range(TOP_K):
        idx_ref[:, k:k + 1] = idxs[k]
        gate_ref[:, k:k + 1] = es[k] / den
        rank_ref[:, k:k + 1] = jnp.sum(ohs[k] * before, axis=-1, keepdims=True).astype(I32)
    run_ref[...] += jnp.sum(oh_all, axis=0, keepdims=True)
    cnt_ref[...] = run_ref[...].astype(I32)


def _route_call(logits, tb):
    t, ne = logits.shape
    kern = functools.partial(_route_kernel, tb=tb, ne=ne)
    blk = lambda: pl.BlockSpec((tb, TOP_K), lambda i: (i, 0))
    return pl.pallas_call(
        kern,
        out_shape=[jax.ShapeDtypeStruct((t, TOP_K), I32), jax.ShapeDtypeStruct((t, TOP_K), F32),
                   jax.ShapeDtypeStruct((t, TOP_K), I32), jax.ShapeDtypeStruct((1, ne), I32)],
        grid=(t // tb,),
        in_specs=[pl.BlockSpec((tb, ne), lambda i: (i, 0))],
        out_specs=[blk(), blk(), blk(), pl.BlockSpec((1, ne), lambda i: (0, 0))],
        scratch_shapes=[pltpu.VMEM((1, ne), F32)],
        compiler_params=_cparams("arbitrary"),
        name="route",
    )(logits)


def _dispatch_kernel(dest_ref, pend_ref, pcnt_ref, nu_ref, m_hbm, xs_hbm, zbuf, sem, zsem, *, tbd, tm, ne, nblk):
    i = pl.program_id(0)

    @pl.when(i == 0)
    def _():
        zbuf[...] = jnp.zeros(zbuf.shape, F32)

        def zero_block(row0):
            cp = pltpu.make_async_copy(zbuf, xs_hbm.at[pl.ds(pl.multiple_of(row0, tm), tm), :], zsem)
            cp.start()
            cp.wait()

        def zfill(e, c):
            @pl.when(pcnt_ref[e] > 0)
            def _():
                zero_block(pend_ref[e] - tm)
            return c
        lax.fori_loop(0, ne, zfill, 0)

        def ztail(bi, c):
            zero_block(bi * tm)
            return c
        lax.fori_loop(nu_ref[0], nblk, ztail, 0)

    def body(t, c):
        src = i * tbd + t
        for k in range(TOP_K):
            d = dest_ref[t * TOP_K + k]
            pltpu.make_async_copy(m_hbm.at[pl.ds(src, 1), :], xs_hbm.at[pl.ds(d, 1), :], sem).start()
        return c
    lax.fori_loop(0, tbd, body, 0)
    pltpu.make_async_copy(xs_hbm.at[pl.ds(0, tbd * TOP_K), :], xs_hbm.at[pl.ds(0, tbd * TOP_K), :], sem).wait()


def _dispatch_call(dest_flat, pend, pcnt, n_used, m, cap, tm, tbd):
    t, d = m.shape
    ne = pend.shape[0]
    kern = functools.partial(_dispatch_kernel, tbd=tbd, tm=tm, ne=ne, nblk=cap // tm)
    return pl.pallas_call(
        kern,
        out_shape=jax.ShapeDtypeStruct((cap, d), F32),
        grid=(t // tbd,),
        in_specs=[pl.BlockSpec((tbd * TOP_K,), lambda i: (i,), memory_space=pltpu.SMEM),
                  pl.BlockSpec(memory_space=pltpu.SMEM),
                  pl.BlockSpec(memory_space=pltpu.SMEM),
                  pl.BlockSpec(memory_space=pltpu.SMEM),
                  pl.BlockSpec(memory_space=pl.ANY)],
        out_specs=pl.BlockSpec(memory_space=pl.ANY),
        scratch_shapes=[pltpu.VMEM((tm, d), F32), pltpu.SemaphoreType.DMA(()), pltpu.SemaphoreType.DMA(())],
        compiler_params=_cparams("arbitrary"),
        name="dispatch",
    )(dest_flat, pend, pcnt, n_used, m)


def _moe_kernel(be_ref, nu_ref, x_ref, wg_ref, wl_ref, bg_ref, bl_ref, wd_ref, bd_ref, y_ref, xb_ref):
    i = pl.program_id(0)
    j = pl.program_id(1)
    nj = pl.num_programs(1)

    @pl.when(i < nu_ref[0])
    def _():
        @pl.when(j == 0)
        def _():
            xb_ref[...] = x_ref[...].astype(BF16)
            y_ref[...] = jnp.broadcast_to(bd_ref[0], y_ref.shape)

        xb = xb_ref[...]
        glu = jnp.dot(xb, wg_ref[0].astype(BF16), preferred_element_type=F32) + bg_ref[0]
        lin = jnp.dot(xb, wl_ref[0].astype(BF16), preferred_element_type=F32) + bl_ref[0]
        glu = jnp.minimum(glu, SWIGLU_LIMIT)
        lin = jnp.clip(lin, -SWIGLU_LIMIT, SWIGLU_LIMIT)
        act = glu * jax.nn.sigmoid(SWIGLU_ALPHA * glu) * (lin + 1.0)
        y_ref[...] += jnp.dot(act.astype(BF16), wd_ref[0].astype(BF16), preferred_element_type=F32)

    @pl.when((i >= nu_ref[0]) & (j == 0))
    def _():
        y_ref[...] = jnp.zeros(y_ref.shape, F32)


def _moe_call(block_expert, n_used, xs, w_gu, b_gu, w_down, b_down, tm, tf):
    cap, d = xs.shape
    ne, _, two_ff = w_gu.shape
    d_ff = two_ff // 2
    nf = d_ff // tf
    nblk = cap // tm

    def jj(i, j, nu):
        return jnp.where(i < nu[0], j, nf - 1)

    grid_spec = pltpu.PrefetchScalarGridSpec(
        num_scalar_prefetch=2,
        grid=(nblk, nf),
        in_specs=[pl.BlockSpec((tm, d), lambda i, j, be, nu: (i, 0)),
                  pl.BlockSpec((1, d, tf), lambda i, j, be, nu: (be[i], 0, jj(i, j, nu))),
                  pl.BlockSpec((1, d, tf), lambda i, j, be, nu: (be[i], 0, nf + jj(i, j, nu))),
                  pl.BlockSpec((1, 1, tf), lambda i, j, be, nu: (be[i], 0, jj(i, j, nu))),
                  pl.BlockSpec((1, 1, tf), lambda i, j, be, nu: (be[i], 0, nf + jj(i, j, nu))),
                  pl.BlockSpec((1, tf, d), lambda i, j, be, nu: (be[i], jj(i, j, nu), 0)),
                  pl.BlockSpec((1, 1, d), lambda i, j, be, nu: (be[i], 0, 0))],
        out_specs=pl.BlockSpec((tm, d), lambda i, j, be, nu: (i, 0)),
        scratch_shapes=[pltpu.VMEM((tm, d), BF16)],
    )
    return pl.pallas_call(
        _moe_kernel,
        out_shape=jax.ShapeDtypeStruct((cap, d), F32),
        grid_spec=grid_spec,
        compiler_params=_cparams("arbitrary", "arbitrary"),
        name="moe",
    )(block_expert, n_used, xs, w_gu, w_gu, b_gu.reshape(ne, 1, two_ff), b_gu.reshape(ne, 1, two_ff),
      w_down, b_down.reshape(ne, 1, d))


def _combine_kernel(dest_ref, gate_ref, hx_ref, mod_ref, fg_ref, y_hbm, o_ref, buf, sem, *, tbc, d):
    def body(t, c):
        for k in range(TOP_K):
            dd = dest_ref[t * TOP_K + k]
            pltpu.make_async_copy(y_hbm.at[pl.ds(dd, 1), :], buf.at[k, pl.ds(t, 1), :], sem).start()
        return c
    lax.fori_loop(0, tbc, body, 0)
    for k in range(TOP_K):
        pltpu.make_async_copy(y_hbm.at[pl.ds(0, tbc), :], buf.at[k], sem).wait()
    gate = gate_ref[...]
    moe = gate[:, 0:1] * buf[0]
    for k in range(1, TOP_K):
        moe = moe + gate[:, k:k + 1] * buf[k]
    g2 = mod_ref[0, :, 5 * d:6 * d]
    hx = hx_ref[...] + g2 * moe
    o_ref[...] = _rms(hx) * fg_ref[...]


def _combine_call(dest_flat, gate, hx1, mod3, rows_per_batch, final_g, y, tbc):
    t, d = hx1.shape
    bpb = rows_per_batch // tbc
    kern = functools.partial(_combine_kernel, tbc=tbc, d=d)
    return pl.pallas_call(
        kern,
        out_shape=jax.ShapeDtypeStruct((t, d), F32),
        grid=(t // tbc,),
        in_specs=[pl.BlockSpec((tbc * TOP_K,), lambda i: (i,), memory_space=pltpu.SMEM),
                  pl.BlockSpec((tbc, TOP_K), lambda i: (i, 0)),
                  pl.BlockSpec((tbc, d), lambda i: (i, 0)),
                  pl.BlockSpec((1, 1, N_MOD * d), lambda i: (i // bpb, 0, 0)),
                  pl.BlockSpec((1, d), lambda i: (0, 0)),
                  pl.BlockSpec(memory_space=pl.ANY)],
        out_specs=pl.BlockSpec((tbc, d), lambda i: (i, 0)),
        scratch_shapes=[pltpu.VMEM((TOP_K, tbc, d), F32), pltpu.SemaphoreType.DMA(())],
        compiler_params=_cparams("arbitrary"),
        name="combine",
    )(dest_flat, gate, hx1, mod3, final_g.reshape(1, d), y)


def _tiles(t_tokens, seq):
    tok = 512 if seq % 512 == 0 else seq
    moe_tm = 1024 if t_tokens * TOP_K >= 32 * 1024 else 256
    return dict(tok=tok, route=tok, moe_tm=moe_tm,
                disp=min(2048, seq), comb=256 if seq % 256 == 0 else seq)


def kernel(x, c, ctx, c_ctx, w_mod, b_mod, norm1_g, norm2_g, w_in, lru_conv_w, lru_conv_b, lru_wa, lru_ba,
           lru_wi, lru_bi, lru_lambda, hy_conv_w, hy_conv_b, hy_w1, hy_b1, hy_w2, hy_b2, hy_w3, hy_b3,
           hy_freq, hy_w4, hy_d, gn_lru, gn_hy, w_out, router_w, router_b, exp_w_gu, exp_b_gu,
           exp_w_down, exp_b_down, final_g):
    depth = w_mod.shape[0]
    assert depth == 1, "single-layer stack: the context stream only feeds the latent scan states"
    n_batch, seq, d = x.shape
    ctx_len = ctx.shape[1]
    w_lru = lru_conv_w.shape[-1]
    w_hy = hy_d.shape[-1]
    nb_l, nb_h = w_lru // LANES, w_hy // LANES
    assert lru_wa.shape[2] == nb_l and lru_wa.shape[3] == LANES, "gate blocks must be 128 wide"
    assert seq % (GRID_W * SUBLANES) == 0
    ne = router_w.shape[-1]
    t_tokens = n_batch * seq
    tl = _tiles(t_tokens, seq)
    l = 0

    n_rows = -(-(n_batch + 1) // SUBLANES) * SUBLANES
    cstack = jnp.concatenate([c, c_ctx[None, :], jnp.zeros((n_rows - n_batch - 1, d), F32)], axis=0)
    mod = _mod_call(cstack, w_mod[l], b_mod[l])
    mod3 = mod.reshape(n_rows, 1, N_MOD * d)

    w_in_b = w_in[l].astype(BF16)
    x2d = x.reshape(t_tokens, d)
    bpb = seq // tl["tok"]
    p_x = _inproj_call(x2d, mod3, lambda i: i // bpb, norm1_g[l], w_in_b, tl["tok"])
    ctx_tm = ctx_len if ctx_len <= 512 else 256
    p_c = _inproj_call(ctx.reshape(n_batch * ctx_len, d), mod3, lambda i: n_batch, norm1_g[l],
                       w_in_b[:, :2 * w_lru], ctx_tm)

    lru_args = (lru_conv_w[l], lru_conv_b[l], lru_wa[l], lru_ba[l], lru_wi[l], lru_bi[l], lru_lambda[l])
    _, h_ctx = _lru_call(p_c, nb_l, n_batch, ctx_len, *lru_args, jnp.zeros((n_batch, 2, w_lru), F32))
    y_lru, _ = _lru_call(p_x, nb_l, n_batch, seq, *lru_args, h_ctx)

    rows = seq // GRID_W
    tables = _dft_tables(rows)
    hfilt = _filt_call(seq, w_hy, hy_w1[l], hy_b1[l], hy_w2[l], hy_b2[l], hy_w3[l], hy_b3[l], hy_freq[l], hy_w4[l])
    kf = _spec_call(hfilt, seq, nb_h, tables[0], tables[1])
    cw, cb = hy_conv_w[l], hy_conv_b[l]
    s0 = 2 * nb_l
    z = _hyena_call(p_x, s0, p_x, s0 + nb_h, cw[:, :w_hy], cb[:w_hy], cw[:, w_hy:2 * w_hy], cb[w_hy:2 * w_hy],
                    hy_d[l, 0], kf, 0, tables, n_batch, seq, nb_h, True)
    hy = _hyena_call(z, 0, p_x, s0 + 2 * nb_h, cw[:, :w_hy], cb[:w_hy], cw[:, 2 * w_hy:], cb[2 * w_hy:],
                     hy_d[l, 1], kf, 1, tables, n_batch, seq, nb_h, False)

    hx1, m, logits = _outproj_call(y_lru, hy, x2d, mod3, seq, gn_lru[l], gn_hy[l], norm2_g[l],
                                   w_out[l].astype(BF16), router_w[l], router_b[l], tl["tok"])

    idx, gate, rank, counts = _route_call(logits, tl["route"])
    tm = tl["moe_tm"]
    counts = counts[0]
    pcnt = (counts + tm - 1) // tm * tm
    pend = jnp.cumsum(pcnt)
    pstart = pend - pcnt
    dest = (pstart[idx] + rank).reshape(-1).astype(I32)
    nblk = t_tokens * TOP_K // tm + ne
    cap = nblk * tm
    n_used = (pend[-1] // tm).astype(I32).reshape(1)
    blk_ids = jnp.arange(nblk, dtype=I32)
    block_expert = jnp.minimum(jnp.searchsorted(pend, blk_ids * tm, side="right"), ne - 1).astype(I32)
    block_expert = jnp.where(blk_ids < n_used[0], block_expert, block_expert[n_used[0] - 1])

    xs = _dispatch_call(dest, pend.astype(I32), pcnt.astype(I32), n_used, m, cap, tm, tl["disp"])
    d_ff = exp_w_down.shape[2]
    tf = 256 if d_ff % 256 == 0 else d_ff
    y = _moe_call(block_expert, n_used, xs, exp_w_gu[l], exp_b_gu[l], exp_w_down[l], exp_b_down[l], tm, tf)
    out = _combine_call(dest, gate, hx1, mod3, seq, final_g, y, tl["comb"])
    return out.reshape(n_batch, seq, d)
```

```python
import functools
import math

import jax
import jax.numpy as jnp
from jax import lax
from jax.experimental import pallas as pl
from jax.experimental.pallas import tpu as pltpu

F32 = jnp.float32
BF16 = jnp.bfloat16
I32 = jnp.int32
U32 = jnp.uint32
HI = lax.Precision.HIGHEST

GRID_W = 64
EPS = 1e-6
N_MOD = 6
LRU_CONV = 4
LRU_C = 8.0
HY_CONV = 3
HY_ORDER = 2
HY_BANDS = 16
HY_TARGET = 1e-2
HY_FAST = 0.3
HY_SLOW = 1.5
TOP_K = 4
SWIGLU_LIMIT = 7.0
SWIGLU_ALPHA = 1.702

LANES = 128
SUBLANES = 8
VMEM_LIMIT_BYTES = 56 * 1024 * 1024

DFT_N1 = 2 * GRID_W
K1_USED = DFT_N1 // 2 + 1
K1_PAD = 72
assert K1_PAD % SUBLANES == 0 and K1_PAD >= K1_USED
LOOP_UNROLL = 4


def _cparams(*sem):
    return pltpu.CompilerParams(dimension_semantics=sem, vmem_limit_bytes=VMEM_LIMIT_BYTES)


def _rms(x):
    return x * lax.rsqrt(jnp.mean(x * x, axis=-1, keepdims=True) + EPS)


def _mod_kernel(c_ref, w_ref, b_ref, o_ref):
    c = c_ref[...]
    s = c * jax.nn.sigmoid(c)
    o_ref[...] = jnp.dot(s, w_ref[...], preferred_element_type=F32, precision=HI) + b_ref[...]


def _mod_call(cstack, w_mod, b_mod):
    rows, d = cstack.shape
    n = w_mod.shape[1]
    tn = 1536 if n % 1536 == 0 else n
    return pl.pallas_call(
        _mod_kernel,
        out_shape=jax.ShapeDtypeStruct((rows, n), F32),
        grid=(n // tn,),
        in_specs=[pl.BlockSpec((rows, d), lambda j: (0, 0)),
                  pl.BlockSpec((d, tn), lambda j: (0, j)),
                  pl.BlockSpec((1, tn), lambda j: (0, j))],
        out_specs=pl.BlockSpec((rows, tn), lambda j: (0, j)),
        compiler_params=_cparams("arbitrary"),
        name="mod",
    )(cstack, w_mod, b_mod.reshape(1, n))


def _inproj_kernel(x_ref, mod_ref, g_ref, w_ref, o_ref, *, d, n_slabs, nchunk):
    x = x_ref[...]
    sh = mod_ref[0, :, 0 * d:1 * d]
    sc = mod_ref[0, :, 1 * d:2 * d]
    a = (_rms(x) * g_ref[...]) * (1.0 + sc) + sh
    ab = a.astype(BF16)
    spc = nchunk // LANES
    for j in range(n_slabs // spc):
        p = jnp.dot(ab, w_ref[:, j * nchunk:(j + 1) * nchunk], preferred_element_type=F32)
        for s in range(spc):
            o_ref[j * spc + s] = p[:, s * LANES:(s + 1) * LANES].astype(BF16)


def _inproj_call(x2d, mod3, mod_row_of_block, g, w_bf16, tm):
    t, d = x2d.shape
    n = w_bf16.shape[1]
    n_slabs = n // LANES
    nchunk = 512 if n % 512 == 0 else LANES
    kern = functools.partial(_inproj_kernel, d=d, n_slabs=n_slabs, nchunk=nchunk)
    return pl.pallas_call(
        kern,
        out_shape=jax.ShapeDtypeStruct((n_slabs, t, LANES), BF16),
        grid=(t // tm,),
        in_specs=[pl.BlockSpec((tm, d), lambda i: (i, 0)),
                  pl.BlockSpec((1, 1, N_MOD * d), lambda i: (mod_row_of_block(i), 0, 0)),
                  pl.BlockSpec((1, d), lambda i: (0, 0)),
                  pl.BlockSpec((d, n), lambda i: (0, 0), pipeline_mode=pl.Buffered(1))],
        out_specs=pl.BlockSpec((n_slabs, tm, LANES), lambda i: (0, i, 0)),
        compiler_params=_cparams("arbitrary"),
        name="inproj",
    )(x2d, mod3, g.reshape(1, d), w_bf16)


def _gelu_tanh(x):
    return 0.5 * x * (1.0 + jnp.tanh(math.sqrt(2.0 / math.pi) * (x + 0.044715 * (x * x * x))))


def _scan_tile(a, b, row, reverse):
    for s in (1, 2, 4):
        if reverse:
            m = row < SUBLANES - s
            sh = SUBLANES - s
        else:
            m = row >= s
            sh = s
        a_s = jnp.where(m, pltpu.roll(a, sh, 0), 1.0)
        b_s = jnp.where(m, pltpu.roll(b, sh, 0), 0.0)
        b = a * b_s + b
        a = a * a_s
    return a, b


def _lru_kernel(r_ref, g_ref, cw_ref, cb_ref, wa_ref, wi_ref, ba_ref, bi_ref, lam_ref, h0_ref,
                y_ref, hl_ref, xp_ref, hf_ref, ab_ref, bb_ref, *, t_len, tc):
    nc = t_len // tc
    ntile = tc // SUBLANES
    zero8 = jnp.zeros((SUBLANES, LANES), F32)
    xp_ref[0:SUBLANES, :] = zero8
    xp_ref[t_len + SUBLANES:t_len + 2 * SUBLANES, :] = zero8

    def copy_body(ci, c):
        t0 = pl.multiple_of(ci * tc, tc)
        xp_ref[pl.ds(t0 + SUBLANES, tc), :] = r_ref[0, pl.ds(t0, tc), :].astype(F32)
        return c
    lax.fori_loop(0, nc, copy_body, 0)

    row = lax.broadcasted_iota(I32, (SUBLANES, LANES), 0)
    cw = cw_ref[...]
    cb = cb_ref[...]
    lam = lam_ref[...]
    sp = jnp.log1p(jnp.exp(-lam))
    wab = [wa_ref[d, 0].astype(BF16) for d in range(2)]
    wib = [wi_ref[d, 0].astype(BF16) for d in range(2)]

    def coeffs(u, ub, d):
        ga = jnp.dot(ub, wab[d], preferred_element_type=F32) + ba_ref[d:d + 1, :]
        gi = jnp.dot(ub, wib[d], preferred_element_type=F32) + bi_ref[d:d + 1, :]
        rg = jax.nn.sigmoid(ga)
        ig = jax.nn.sigmoid(gi)
        a = jnp.exp((-LRU_C) * rg * sp[d:d + 1, :])
        b = jnp.sqrt(1.0 - a * a) * (ig * u)
        return a, b

    def fwd_body(ci, hcar):
        t0 = pl.multiple_of(ci * tc, tc)
        blk = xp_ref[pl.ds(t0, tc + 2 * SUBLANES), :]
        u = (cw[0:1, :] * blk[6:6 + tc] + cw[1:2, :] * blk[7:7 + tc]
             + cw[2:3, :] * blk[8:8 + tc] + cw[3:4, :] * blk[9:9 + tc]) + cb
        ub = u.astype(BF16)
        a0, b0 = coeffs(u, ub, 0)
        a1, b1 = coeffs(u, ub, 1)
        ab_ref[pl.ds(t0, tc), :] = a1
        bb_ref[pl.ds(t0, tc), :] = b1
        for j in range(ntile):
            at, bt = _scan_tile(a0[j * 8:(j + 1) * 8], b0[j * 8:(j + 1) * 8], row, False)
            hf_ref[pl.ds(t0 + j * 8, 8), :] = at * hcar + bt
            atot = jnp.broadcast_to(at[7:8, :], (SUBLANES, LANES))
            btot = jnp.broadcast_to(bt[7:8, :], (SUBLANES, LANES))
            hcar = atot * hcar + btot
        return hcar

    h0f = jnp.broadcast_to(h0_ref[0, 0:1, :], (SUBLANES, LANES))
    hfin = lax.fori_loop(0, nc, fwd_body, h0f)
    hl_ref[0, 0:1, :] = hfin[0:1, :]

    def bwd_body(k, hcar):
        ci = nc - 1 - k
        t0 = pl.multiple_of(ci * tc, tc)
        a1 = ab_ref[pl.ds(t0, tc), :]
        b1 = bb_ref[pl.ds(t0, tc), :]
        gg = g_ref[0, pl.ds(t0, tc), :].astype(F32)
        gl = _gelu_tanh(gg)
        for j in reversed(range(ntile)):
            at, bt = _scan_tile(a1[j * 8:(j + 1) * 8], b1[j * 8:(j + 1) * 8], row, True)
            hb = at * hcar + bt
            hf = hf_ref[pl.ds(t0 + j * 8, 8), :]
            y_ref[0, pl.ds(t0 + j * 8, 8), :] = ((hf + hb) * gl[j * 8:(j + 1) * 8]).astype(y_ref.dtype)
            atot = jnp.broadcast_to(at[0:1, :], (SUBLANES, LANES))
            btot = jnp.broadcast_to(bt[0:1, :], (SUBLANES, LANES))
            hcar = atot * hcar + btot
        return hcar

    h0b = jnp.broadcast_to(h0_ref[0, 1:2, :], (SUBLANES, LANES))
    hfin_b = lax.fori_loop(0, nc, bwd_body, h0b)
    hl_ref[0, 1:2, :] = hfin_b[0:1, :]


def _lru_call(p_slabs, nb, n_batch, t_len, conv_w, conv_b, wa, ba, wi, bi, lam, h0):
    tc = 128 if t_len % 128 == 0 else t_len
    kern = functools.partial(_lru_kernel, t_len=t_len, tc=tc)
    w_lru = nb * LANES
    return pl.pallas_call(
        kern,
        out_shape=[jax.ShapeDtypeStruct((nb, n_batch * t_len, LANES), F32),
                   jax.ShapeDtypeStruct((n_batch, 2, w_lru), F32)],
        grid=(nb, n_batch),
        in_specs=[pl.BlockSpec((1, t_len, LANES), lambda n, b: (n, b, 0)),
                  pl.BlockSpec((1, t_len, LANES), lambda n, b: (nb + n, b, 0)),
                  pl.BlockSpec((LRU_CONV, LANES), lambda n, b: (0, n)),
                  pl.BlockSpec((1, LANES), lambda n, b: (0, n)),
                  pl.BlockSpec((2, 1, LANES, LANES), lambda n, b: (0, n, 0, 0)),
                  pl.BlockSpec((2, 1, LANES, LANES), lambda n, b: (0, n, 0, 0)),
                  pl.BlockSpec((2, LANES), lambda n, b: (0, n)),
                  pl.BlockSpec((2, LANES), lambda n, b: (0, n)),
                  pl.BlockSpec((2, LANES), lambda n, b: (0, n)),
                  pl.BlockSpec((1, 2, LANES), lambda n, b: (b, 0, n))],
        out_specs=[pl.BlockSpec((1, t_len, LANES), lambda n, b: (n, b, 0)),
                   pl.BlockSpec((1, 2, LANES), lambda n, b: (b, 0, n))],
        scratch_shapes=[pltpu.VMEM((t_len + 2 * SUBLANES, LANES), F32),
                        pltpu.VMEM((t_len, LANES), F32),
                        pltpu.VMEM((t_len, LANES), F32),
                        pltpu.VMEM((t_len, LANES), F32)],
        compiler_params=_cparams("arbitrary", "arbitrary"),
        name="lru",
    )(p_slabs, p_slabs, conv_w, conv_b.reshape(1, w_lru), wa, wi, ba, bi, lam, h0)


def _filt_kernel(w1_ref, b1_ref, w2_ref, b2_ref, w3_ref, b3_ref, fr_ref, w4_ref, o_ref,
                 *, l_len, rows, tl, w_hy):
    i = pl.program_id(0)
    rho = i * tl + lax.broadcasted_iota(I32, (tl, 1), 0)
    tidx = ((rho % GRID_W) * rows + rho // GRID_W).astype(F32)
    tt = tidx * (1.0 / (l_len - 1))
    ww = (2.0 * math.pi) * tidx / l_len
    band = lax.broadcasted_iota(I32, (1, HY_BANDS), 1).astype(F32)
    f = 1e-4 + band * ((HY_BANDS - 1 - 1e-4) / (HY_BANDS - 1))
    fw = f * ww
    w1 = w1_ref[...]
    pre = (tt * w1[0:1, :]
           + jnp.dot(jnp.cos(fw), w1[1:1 + HY_BANDS, :], preferred_element_type=F32, precision=HI)
           + jnp.dot(-jnp.sin(fw), w1[1 + HY_BANDS:1 + 2 * HY_BANDS, :], preferred_element_type=F32, precision=HI))
    fr = fr_ref[...]
    h = jnp.sin(fr * (pre + b1_ref[...]))
    h = jnp.sin(fr * (jnp.dot(h, w2_ref[...], preferred_element_type=F32, precision=HI) + b2_ref[...]))
    h = jnp.sin(fr * (jnp.dot(h, w3_ref[...], preferred_element_type=F32, precision=HI) + b3_ref[...]))
    max_decay = math.log(HY_TARGET) / HY_FAST
    min_decay = math.log(HY_TARGET) / HY_SLOW
    nblk = w_hy // LANES
    for cb in range(nblk):
        ch = (cb * LANES + lax.broadcasted_iota(I32, (1, LANES), 1)).astype(F32)
        delta = jnp.abs(min_decay + ch * ((max_decay - min_decay) / (w_hy - 1)))
        decay = jnp.exp(-tt * delta)
        for od in range(HY_ORDER * 2):
            col = od * w_hy + cb * LANES
            v = jnp.dot(h, w4_ref[:, col:col + LANES], preferred_element_type=F32, precision=HI)
            o_ref[od * nblk + cb] = v * decay


def _filt_call(l_len, w_hy, w1, b1, w2, b2, w3, b3, freq, w4):
    rows = l_len // GRID_W
    tl = 512 if l_len % 512 == 0 else l_len
    fh = w2.shape[0]
    nslab = HY_ORDER * 2 * (w_hy // LANES)
    kern = functools.partial(_filt_kernel, l_len=l_len, rows=rows, tl=tl, w_hy=w_hy)
    full = lambda a: pl.BlockSpec(a.shape, lambda i: (0,) * a.ndim)
    args = (w1, b1.reshape(1, fh), w2, b2.reshape(1, fh), w3, b3.reshape(1, fh), freq.reshape(1, fh), w4)
    return pl.pallas_call(
        kern,
        out_shape=jax.ShapeDtypeStruct((nslab, l_len, LANES), F32),
        grid=(l_len // tl,),
        in_specs=[full(a) for a in args],
        out_specs=pl.BlockSpec((nslab, tl, LANES), lambda i: (0, i, 0)),
        compiler_params=_cparams("arbitrary"),
        name="filt",
    )(*args)


def _dft_tables(rows):
    n_len = DFT_N1 * rows
    k1 = jnp.arange(K1_PAD, dtype=I32)
    r = jnp.arange(rows, dtype=I32)
    w = jnp.arange(GRID_W, dtype=I32)
    idx = (k1[None, :, None] * (r[:, None, None] + rows * w[None, None, :])) % n_len
    ang = idx.astype(F32) * (2.0 * math.pi / n_len)
    valid = (k1 < K1_USED)[None, :, None]
    ca = jnp.where(valid, jnp.cos(ang), 0.0)
    sa = jnp.where(valid, jnp.sin(ang), 0.0)
    ta = jnp.stack([ca, -sa], axis=2).reshape(rows, 2 * K1_PAD, GRID_W)
    td = jnp.transpose(ta, (0, 2, 1))
    k2 = jnp.arange(rows, dtype=I32)
    th = ((k2[:, None] * r[None, :]) % rows).astype(F32) * (2.0 * math.pi / rows)
    c, s = jnp.cos(th), jnp.sin(th)
    tb = jnp.concatenate([jnp.stack([c, s], axis=2).reshape(rows, 2 * rows),
                          jnp.stack([-s, c], axis=2).reshape(rows, 2 * rows)], axis=0)
    tcm = jnp.stack([jnp.concatenate([c, -s], axis=1), jnp.concatenate([s, c], axis=1)], axis=1).reshape(2 * rows, 2 * rows)
    return ta.astype(BF16), tb.astype(BF16), tcm.astype(BF16), td.astype(BF16)


def _pitch_r(rows):
    return rows + SUBLANES


def _stage_a(tile_fn, ta_ref, s1_ref, rows):
    pitch = _pitch_r(rows)

    def body(r, c):
        out = jnp.dot(ta_ref[r], tile_fn(r), preferred_element_type=F32)
        words = pltpu.bitcast(out.astype(BF16), U32)
        s1_ref[pl.ds(r, K1_PAD, stride=pitch), :] = words
        return c
    lax.fori_loop(0, rows, body, 0, unroll=LOOP_UNROLL)


def _loop_k1(body):
    lax.fori_loop(0, K1_USED - 1, body, 0, unroll=2)
    body(K1_USED - 1, 0)


def _stage_b(k1, tb_ref, s1_ref, rows):
    pitch = _pitch_r(rows)
    t = s1_ref[pl.ds(pl.multiple_of(k1 * pitch, SUBLANES), rows), :]
    y = jnp.dot(tb_ref[...], pltpu.bitcast(t, BF16), preferred_element_type=F32)
    return y[:rows], y[rows:]


def _spec_kernel(hf_ref, hb_ref, ta_ref, tb_ref, kf_ref, s1_ref, *, rows):
    n_len = DFT_N1 * rows

    def scale(k1):
        return jnp.where((k1 == 0) | (k1 == K1_USED - 1), 1.0, 2.0) * (1.0 / n_len)

    _stage_a(lambda r: hf_ref[0, pl.ds(pl.multiple_of(r * GRID_W, GRID_W), GRID_W), :].astype(BF16),
             ta_ref, s1_ref, rows)

    def b_fwd(k1, c):
        yre, yim = _stage_b(k1, tb_ref, s1_ref, rows)
        sc = scale(k1)
        kf_ref[0, 0, k1, 0] = yre * sc
        kf_ref[0, 0, k1, 1] = yim * sc
        return c
    _loop_k1(b_fwd)

    rid = lax.broadcasted_iota(I32, (GRID_W, LANES), 0)

    def tile_b(r):
        x = hb_ref[0, pl.ds(pl.multiple_of(r * GRID_W, GRID_W), GRID_W), :]
        x = jnp.where((rid == 0) & (r == 0), 0.0, x)
        return x.astype(BF16)
    _stage_a(tile_b, ta_ref, s1_ref, rows)

    def b_bwd(k1, c):
        yre, yim = _stage_b(k1, tb_ref, s1_ref, rows)
        sc = scale(k1)
        kf_ref[0, 0, k1, 0] += yre * sc
        kf_ref[0, 0, k1, 1] -= yim * sc
        return c
    _loop_k1(b_bwd)


def _spec_call(hfilt, l_len, nblk, ta, tb):
    rows = l_len // GRID_W
    kern = functools.partial(_spec_kernel, rows=rows)
    return pl.pallas_call(
        kern,
        out_shape=jax.ShapeDtypeStruct((HY_ORDER, nblk, K1_USED, 2, rows, LANES), F32),
        grid=(HY_ORDER, nblk),
        in_specs=[pl.BlockSpec((1, l_len, LANES), lambda o, c: (o * 2 * nblk + c, 0, 0)),
                  pl.BlockSpec((1, l_len, LANES), lambda o, c: (o * 2 * nblk + nblk + c, 0, 0)),
                  pl.BlockSpec(ta.shape, lambda o, c: (0, 0, 0), pipeline_mode=pl.Buffered(1)),
                  pl.BlockSpec(tb.shape, lambda o, c: (0, 0), pipeline_mode=pl.Buffered(1))],
        out_specs=pl.BlockSpec((1, 1, K1_USED, 2, rows, LANES), lambda o, c: (o, c, 0, 0, 0, 0)),
        scratch_shapes=[pltpu.VMEM((K1_PAD * _pitch_r(rows), LANES), U32)],
        compiler_params=_cparams("arbitrary", "arbitrary"),
        name="spec",
    )(hfilt, hfilt, ta, tb)


def _conv_grid_order(src_ref, dst_ref, w_ref, b_ref, rows):
    w0, w1, w2, bb = w_ref[0:1, :], w_ref[1:2, :], w_ref[2:3, :], b_ref[...]
    g = GRID_W
    rid = lax.broadcasted_iota(I32, (g, LANES), 0)

    def grp(r0):
        return src_ref[0, pl.ds(r0, g), :].astype(F32)

    last = grp((rows - 1) * g)
    prev0 = jnp.where(rid >= 1, pltpu.roll(last, 1, 0), 0.0)
    dst_ref[pl.ds(0, g), :] = (w0 * prev0 + w1 * grp(0) + w2 * grp(g) + bb).astype(dst_ref.dtype)
    first = grp(0)
    nxt = jnp.where(rid < g - 1, pltpu.roll(first, g - 1, 0), 0.0)
    dst_ref[pl.ds((rows - 1) * g, g), :] = (w0 * grp((rows - 2) * g) + w1 * last + w2 * nxt + bb).astype(dst_ref.dtype)

    def body(r, c):
        rm = pl.multiple_of((r - 1) * g, g)
        r0 = pl.multiple_of(r * g, g)
        rp = pl.multiple_of((r + 1) * g, g)
        v = w0 * grp(rm) + w1 * grp(r0) + w2 * grp(rp) + bb
        dst_ref[pl.ds(r0, g), :] = v.astype(dst_ref.dtype)
        return c
    lax.fori_loop(1, rows - 1, body, 0, unroll=2)


def _hyena_kernel(sig_ref, gate_ref, cws_ref, cbs_ref, cwg_ref, cbg_ref, d_ref, kf_ref,
                  ta_ref, tb_ref, tc_ref, td_ref, o_ref, u_ref, gt_ref, s1_ref, s2_ref,
                  *, rows, conv_sig):
    g = GRID_W

    @pl.when((pl.program_id(0) == 0) & (pl.program_id(1) == 0))
    def _():
        s2_ref[...] = jnp.zeros(s2_ref.shape, U32)

    if conv_sig:
        _conv_grid_order(sig_ref, u_ref, cws_ref, cbs_ref, rows)
    else:
        def cp(r, c):
            r0 = pl.multiple_of(r * g, g)
            u_ref[pl.ds(r0, g), :] = sig_ref[0, pl.ds(r0, g), :]
            return c
        lax.fori_loop(0, rows, cp, 0, unroll=LOOP_UNROLL)
    _conv_grid_order(gate_ref, gt_ref, cwg_ref, cbg_ref, rows)

    _stage_a(lambda r: u_ref[pl.ds(pl.multiple_of(r * g, g), g), :], ta_ref, s1_ref, rows)

    def bc_body(k1, c):
        yre, yim = _stage_b(k1, tb_ref, s1_ref, rows)
        kre = kf_ref[0, 0, k1, 0]
        kim = kf_ref[0, 0, k1, 1]
        zre = yre * kre - yim * kim
        zim = yre * kim + yim * kre
        z = jnp.concatenate([zre, zim], axis=0).astype(BF16)
        v = jnp.dot(tc_ref[...], z, preferred_element_type=F32)
        s2_ref[pl.ds(k1, rows, stride=K1_PAD), :] = pltpu.bitcast(v.astype(BF16), U32)
        return c
    _loop_k1(bc_body)

    dd = d_ref[...]

    def d_body(r, c):
        r0 = pl.multiple_of(r * g, g)
        t = s2_ref[pl.ds(pl.multiple_of(r * K1_PAD, SUBLANES), K1_PAD), :]
        y = jnp.dot(td_ref[r], pltpu.bitcast(t, BF16), preferred_element_type=F32)
        uu = u_ref[pl.ds(r0, g), :].astype(F32)
        o_ref[0, pl.ds(r0, g), :] = (gt_ref[pl.ds(r0, g), :] * (y + dd * uu)).astype(o_ref.dtype)
        return c
    lax.fori_loop(0, rows, d_body, 0, unroll=LOOP_UNROLL)


def _hyena_call(sig_arr, sig_slab0, gate_arr, gate_slab0, conv_w_sig, conv_b_sig, conv_w_gate, conv_b_gate,
                d_vec, kf, order, tables, n_batch, l_len, nblk, conv_sig):
    rows = l_len // GRID_W
    ta, tb, tcm, td = tables
    kern = functools.partial(_hyena_kernel, rows=rows, conv_sig=conv_sig)
    w_hy = nblk * LANES
    const = lambda a: pl.BlockSpec(a.shape, lambda c, b: (0,) * a.ndim, pipeline_mode=pl.Buffered(1))
    return pl.pallas_call(
        kern,
        out_shape=jax.ShapeDtypeStruct((nblk, n_batch * l_len, LANES), BF16),
        grid=(nblk, n_batch),
        in_specs=[pl.BlockSpec((1, l_len, LANES), lambda c, b: (sig_slab0 + c, b, 0)),
                  pl.BlockSpec((1, l_len, LANES), lambda c, b: (gate_slab0 + c, b, 0)),
                  pl.BlockSpec((HY_CONV, LANES), lambda c, b: (0, c)),
                  pl.BlockSpec((1, LANES), lambda c, b: (0, c)),
                  pl.BlockSpec((HY_CONV, LANES), lambda c, b: (0, c)),
                  pl.BlockSpec((1, LANES), lambda c, b: (0, c)),
                  pl.BlockSpec((1, LANES), lambda c, b: (0, c)),
                  pl.BlockSpec((1, 1, K1_USED, 2, rows, LANES), lambda c, b: (order, c, 0, 0, 0, 0),
                               pipeline_mode=pl.Buffered(1)),
                  const(ta), const(tb), const(tcm), const(td)],
        out_specs=pl.BlockSpec((1, l_len, LANES), lambda c, b: (c, b, 0)),
        scratch_shapes=[pltpu.VMEM((l_len, LANES), BF16),
                        pltpu.VMEM((l_len, LANES), F32),
                        pltpu.VMEM((K1_PAD * _pitch_r(rows), LANES), U32),
                        pltpu.VMEM((rows * K1_PAD, LANES), U32)],
        compiler_params=_cparams("arbitrary", "arbitrary"),
        name="hyena%d" % order,
    )(sig_arr, gate_arr, conv_w_sig, conv_b_sig.reshape(1, w_hy), conv_w_gate, conv_b_gate.reshape(1, w_hy),
      d_vec.reshape(1, w_hy), kf, ta, tb, tcm, td)


def _outproj_kernel(yl_ref, hy_ref, x_ref, mod_ref, gl_ref, gh_ref, n2_ref, w_ref, rw2_ref, rb_ref,
                    hx_ref, m_ref, lg_ref, *, d, nb_l, nb_h, ne):
    yl = jnp.concatenate([yl_ref[j].astype(F32) for j in range(nb_l)], axis=-1)
    hy = jnp.concatenate([hy_ref[j].astype(F32) for j in range(nb_h)], axis=-1)
    cat = jnp.concatenate([_rms(yl) * gl_ref[...], _rms(hy) * gh_ref[...]], axis=-1).astype(BF16)
    out = jnp.dot(cat, w_ref[...], preferred_element_type=F32)
    g1 = mod_ref[0, :, 2 * d:3 * d]
    sh2 = mod_ref[0, :, 3 * d:4 * d]
    sc2 = mod_ref[0, :, 4 * d:5 * d]
    hx = x_ref[...] + g1 * out
    hx_ref[...] = hx
    m = (_rms(hx) * n2_ref[...]) * (1.0 + sc2) + sh2
    m_ref[...] = m
    m_hi = m.astype(BF16)
    m_lo = (m - m_hi.astype(F32)).astype(BF16)
    a = jnp.dot(m_hi, rw2_ref[...], preferred_element_type=F32)
    b = jnp.dot(m_lo, rw2_ref[:, :ne], preferred_element_type=F32)
    lg_ref[...] = (a[:, :ne] + b) + a[:, ne:] + rb_ref[...]


def _outproj_call(y_lru, hy, x2d, mod3, rows_per_batch, gn_lru, gn_hy, norm2_g, w_out_bf16, router_w, router_b, tm):
    t, d = x2d.shape
    nb_l, nb_h = y_lru.shape[0], hy.shape[0]
    ne = router_w.shape[1]
    bpb = rows_per_batch // tm
    rw_hi = router_w.astype(BF16)
    rw_lo = (router_w - rw_hi.astype(F32)).astype(BF16)
    rw2 = jnp.concatenate([rw_hi, rw_lo], axis=1)
    kern = functools.partial(_outproj_kernel, d=d, nb_l=nb_l, nb_h=nb_h, ne=ne)
    return pl.pallas_call(
        kern,
        out_shape=[jax.ShapeDtypeStruct((t, d), F32), jax.ShapeDtypeStruct((t, d), F32),
                   jax.ShapeDtypeStruct((t, ne), F32)],
        grid=(t // tm,),
        in_specs=[pl.BlockSpec((nb_l, tm, LANES), lambda i: (0, i, 0)),
                  pl.BlockSpec((nb_h, tm, LANES), lambda i: (0, i, 0)),
                  pl.BlockSpec((tm, d), lambda i: (i, 0)),
                  pl.BlockSpec((1, 1, N_MOD * d), lambda i: (i // bpb, 0, 0)),
                  pl.BlockSpec((1, nb_l * LANES), lambda i: (0, 0)),
                  pl.BlockSpec((1, nb_h * LANES), lambda i: (0, 0)),
                  pl.BlockSpec((1, d), lambda i: (0, 0)),
                  pl.BlockSpec(w_out_bf16.shape, lambda i: (0, 0), pipeline_mode=pl.Buffered(1)),
                  pl.BlockSpec((d, 2 * ne), lambda i: (0, 0)),
                  pl.BlockSpec((1, ne), lambda i: (0, 0))],
        out_specs=[pl.BlockSpec((tm, d), lambda i: (i, 0)),
                   pl.BlockSpec((tm, d), lambda i: (i, 0)),
                   pl.BlockSpec((tm, ne), lambda i: (i, 0))],
        compiler_params=_cparams("arbitrary"),
        name="outproj",
    )(y_lru, hy, x2d, mod3, gn_lru.reshape(1, -1), gn_hy.reshape(1, -1), norm2_g.reshape(1, d),
      w_out_bf16, rw2, router_b.reshape(1, ne))


def _route_kernel(lg_ref, idx_ref, gate_ref, rank_ref, cnt_ref, run_ref, *, tb, ne):
    @pl.when(pl.program_id(0) == 0)
    def _():
        run_ref[...] = jnp.zeros(run_ref.shape, F32)

    l = lg_ref[...]
    lane = lax.broadcasted_iota(I32, (tb, ne), 1)
    vals, idxs, ohs = [], [], []
    for _ in range(TOP_K):
        m = jnp.max(l, axis=-1, keepdims=True)
        ix = jnp.min(jnp.where(l == m, lane, ne), axis=-1, keepdims=True)
        sel = lane == ix
        vals.append(m)
        idxs.append(ix)
        ohs.append(sel.astype(F32))
        l = jnp.where(sel, -jnp.inf, l)
    es = [jnp.exp(v - vals[0]) for v in vals]
    den = es[0] + es[1] + es[2] + es[3]
    oh_all = ohs[0] + ohs[1] + ohs[2] + ohs[3]
    ri = lax.broadcasted_iota(I32, (tb, tb), 0)
    ci = lax.broadcasted_iota(I32, (tb, tb), 1)
    ltri = (ci < ri).astype(BF16)
    before = jnp.dot(ltri, oh_all.astype(BF16), preferred_element_type=F32) + run_ref[...]
    for k in range(TOP_K):
        idx_ref[:, k:k + 1] = idxs[k]
        gate_ref[:, k:k + 1] = es[k] / den
        rank_ref[:, k:k + 1] = jnp.sum(ohs[k] * before, axis=-1, keepdims=True).astype(I32)
    run_ref[...] += jnp.sum(oh_all, axis=0, keepdims=True)
    cnt_ref[...] = run_ref[...].astype(I32)


def _route_call(logits, tb):
    t, ne = logits.shape
    kern = functools.partial(_route_kernel, tb=tb, ne=ne)
    blk = lambda: pl.BlockSpec((tb, TOP_K), lambda i: (i, 0))
    return pl.pallas_call(
        kern,
        out_shape=[jax.ShapeDtypeStruct((t, TOP_K), I32), jax.ShapeDtypeStruct((t, TOP_K), F32),
                   jax.ShapeDtypeStruct((t, TOP_K), I32), jax.ShapeDtypeStruct((1, ne), I32)],
        grid=(t // tb,),
        in_specs=[pl.BlockSpec((tb, ne), lambda i: (i, 0))],
        out_specs=[blk(), blk(), blk(), pl.BlockSpec((1, ne), lambda i: (0, 0))],
        scratch_shapes=[pltpu.VMEM((1, ne), F32)],
        compiler_params=_cparams("arbitrary"),
        name="route",
    )(logits)


def _dispatch_kernel(dest_ref, pend_ref, pcnt_ref, nu_ref, m_ref, xs_hbm, zbuf, sem, zsem, *, tbd, tm, ne, nblk):
    i = pl.program_id(0)

    @pl.when(i == 0)
    def _():
        zbuf[...] = jnp.zeros(zbuf.shape, F32)

        def zero_block(row0):
            cp = pltpu.make_async_copy(zbuf, xs_hbm.at[pl.ds(pl.multiple_of(row0, tm), tm), :], zsem)
            cp.start()
            cp.wait()

        def zfill(e, c):
            @pl.when(pcnt_ref[e] > 0)
            def _():
                zero_block(pend_ref[e] - tm)
            return c
        lax.fori_loop(0, ne, zfill, 0)

        def ztail(bi, c):
            zero_block(bi * tm)
            return c
        lax.fori_loop(nu_ref[0], nblk, ztail, 0)

    def body(t, c):
        for k in range(TOP_K):
            d = dest_ref[t * TOP_K + k]
            pltpu.make_async_copy(m_ref.at[pl.ds(t, 1), :], xs_hbm.at[pl.ds(d, 1), :], sem).start()
        return c
    lax.fori_loop(0, tbd, body, 0, unroll=4)
    for k in range(TOP_K):
        pltpu.make_async_copy(m_ref, xs_hbm.at[pl.ds(0, tbd), :], sem).wait()


def _dispatch_call(dest_flat, pend, pcnt, n_used, m, cap, tm, tbd):
    t, d = m.shape
    ne = pend.shape[0]
    kern = functools.partial(_dispatch_kernel, tbd=tbd, tm=tm, ne=ne, nblk=cap // tm)
    return pl.pallas_call(
        kern,
        out_shape=jax.ShapeDtypeStruct((cap, d), F32),
        grid=(t // tbd,),
        in_specs=[pl.BlockSpec((tbd * TOP_K,), lambda i: (i,), memory_space=pltpu.SMEM),
                  pl.BlockSpec(memory_space=pltpu.SMEM),
                  pl.BlockSpec(memory_space=pltpu.SMEM),
                  pl.BlockSpec(memory_space=pltpu.SMEM),
                  pl.BlockSpec((tbd, d), lambda i: (i, 0))],
        out_specs=pl.BlockSpec(memory_space=pl.ANY),
        scratch_shapes=[pltpu.VMEM((tm, d), F32), pltpu.SemaphoreType.DMA(()), pltpu.SemaphoreType.DMA(())],
        compiler_params=_cparams("arbitrary"),
        name="dispatch",
    )(dest_flat, pend, pcnt, n_used, m)


def _moe_kernel(be_ref, nu_ref, x_ref, wg_ref, wl_ref, bg_ref, bl_ref, wd_ref, bd_ref, y_ref, xb_ref):
    i = pl.program_id(0)
    j = pl.program_id(1)
    nj = pl.num_programs(1)

    @pl.when(i < nu_ref[0])
    def _():
        @pl.when(j == 0)
        def _():
            xb_ref[...] = x_ref[...].astype(BF16)
            y_ref[...] = jnp.broadcast_to(bd_ref[0], y_ref.shape)

        xb = xb_ref[...]
        glu = jnp.dot(xb, wg_ref[0].astype(BF16), preferred_element_type=F32) + bg_ref[0]
        lin = jnp.dot(xb, wl_ref[0].astype(BF16), preferred_element_type=F32) + bl_ref[0]
        glu = jnp.minimum(glu, SWIGLU_LIMIT)
        lin = jnp.clip(lin, -SWIGLU_LIMIT, SWIGLU_LIMIT)
        act = glu * jax.nn.sigmoid(SWIGLU_ALPHA * glu) * (lin + 1.0)
        y_ref[...] += jnp.dot(act.astype(BF16), wd_ref[0].astype(BF16), preferred_element_type=F32)

    @pl.when((i >= nu_ref[0]) & (j == 0))
    def _():
        y_ref[...] = jnp.zeros(y_ref.shape, F32)


def _moe_call(block_expert, n_used, xs, w_gu, b_gu, w_down, b_down, tm, tf):
    cap, d = xs.shape
    ne, _, two_ff = w_gu.shape
    d_ff = two_ff // 2
    nf = d_ff // tf
    nblk = cap // tm

    def jj(i, j, nu):
        return jnp.where(i < nu[0], j, nf - 1)

    grid_spec = pltpu.PrefetchScalarGridSpec(
        num_scalar_prefetch=2,
        grid=(nblk, nf),
        in_specs=[pl.BlockSpec((tm, d), lambda i, j, be, nu: (i, 0)),
                  pl.BlockSpec((1, d, tf), lambda i, j, be, nu: (be[i], 0, jj(i, j, nu))),
                  pl.BlockSpec((1, d, tf), lambda i, j, be, nu: (be[i], 0, nf + jj(i, j, nu))),
                  pl.BlockSpec((1, 1, tf), lambda i, j, be, nu: (be[i], 0, jj(i, j, nu))),
                  pl.BlockSpec((1, 1, tf), lambda i, j, be, nu: (be[i], 0, nf + jj(i, j, nu))),
                  pl.BlockSpec((1, tf, d), lambda i, j, be, nu: (be[i], jj(i, j, nu), 0)),
                  pl.BlockSpec((1, 1, d), lambda i, j, be, nu: (be[i], 0, 0))],
        out_specs=pl.BlockSpec((tm, d), lambda i, j, be, nu: (i, 0)),
        scratch_shapes=[pltpu.VMEM((tm, d), BF16)],
    )
    return pl.pallas_call(
        _moe_kernel,
        out_shape=jax.ShapeDtypeStruct((cap, d), F32),
        grid_spec=grid_spec,
        compiler_params=_cparams("arbitrary", "arbitrary"),
        name="moe",
    )(block_expert, n_used, xs, w_gu, w_gu, b_gu.reshape(ne, 1, two_ff), b_gu.reshape(ne, 1, two_ff),
      w_down, b_down.reshape(ne, 1, d))


def _combine_kernel(dest_ref, gate_ref, hx_ref, mod_ref, fg_ref, y_hbm, o_ref, buf, sem, *, tbc, d):
    def body(t, c):
        for k in range(TOP_K):
            dd = dest_ref[t * TOP_K + k]
            pltpu.make_async_copy(y_hbm.at[pl.ds(dd, 1), :], buf.at[k, pl.ds(t, 1), :], sem).start()
        return c
    lax.fori_loop(0, tbc, body, 0)
    for k in range(TOP_K):
        pltpu.make_async_copy(y_hbm.at[pl.ds(0, tbc), :], buf.at[k], sem).wait()
    gate = gate_ref[...]
    moe = gate[:, 0:1] * buf[0]
    for k in range(1, TOP_K):
        moe = moe + gate[:, k:k + 1] * buf[k]
    g2 = mod_ref[0, :, 5 * d:6 * d]
    hx = hx_ref[...] + g2 * moe
    o_ref[...] = _rms(hx) * fg_ref[...]


def _combine_call(dest_flat, gate, hx1, mod3, rows_per_batch, final_g, y, tbc):
    t, d = hx1.shape
    bpb = rows_per_batch // tbc
    kern = functools.partial(_combine_kernel, tbc=tbc, d=d)
    return pl.pallas_call(
        kern,
        out_shape=jax.ShapeDtypeStruct((t, d), F32),
        grid=(t // tbc,),
        in_specs=[pl.BlockSpec((tbc * TOP_K,), lambda i: (i,), memory_space=pltpu.SMEM),
                  pl.BlockSpec((tbc, TOP_K), lambda i: (i, 0)),
                  pl.BlockSpec((tbc, d), lambda i: (i, 0)),
                  pl.BlockSpec((1, 1, N_MOD * d), lambda i: (i // bpb, 0, 0)),
                  pl.BlockSpec((1, d), lambda i: (0, 0)),
                  pl.BlockSpec(memory_space=pl.ANY)],
        out_specs=pl.BlockSpec((tbc, d), lambda i: (i, 0)),
        scratch_shapes=[pltpu.VMEM((TOP_K, tbc, d), F32), pltpu.SemaphoreType.DMA(())],
        compiler_params=_cparams("arbitrary"),
        name="combine",
    )(dest_flat, gate, hx1, mod3, final_g.reshape(1, d), y)


def _tiles(t_tokens, seq):
    tok = 512 if seq % 512 == 0 else seq
    moe_tm = 1024 if t_tokens * TOP_K >= 32 * 1024 else 256
    return dict(tok=tok, route=tok, moe_tm=moe_tm,
                disp=tok, comb=256 if seq % 256 == 0 else seq)


def kernel(x, c, ctx, c_ctx, w_mod, b_mod, norm1_g, norm2_g, w_in, lru_conv_w, lru_conv_b, lru_wa, lru_ba,
           lru_wi, lru_bi, lru_lambda, hy_conv_w, hy_conv_b, hy_w1, hy_b1, hy_w2, hy_b2, hy_w3, hy_b3,
           hy_freq, hy_w4, hy_d, gn_lru, gn_hy, w_out, router_w, router_b, exp_w_gu, exp_b_gu,
           exp_w_down, exp_b_down, final_g):
    depth = w_mod.shape[0]
    assert depth == 1, "single-layer stack: the context stream only feeds the latent scan states"
    n_batch, seq, d = x.shape
    ctx_len = ctx.shape[1]
    w_lru = lru_conv_w.shape[-1]
    w_hy = hy_d.shape[-1]
    nb_l, nb_h = w_lru // LANES, w_hy // LANES
    assert lru_wa.shape[2] == nb_l and lru_wa.shape[3] == LANES, "gate blocks must be 128 wide"
    assert seq % (GRID_W * SUBLANES) == 0
    ne = router_w.shape[-1]
    t_tokens = n_batch * seq
    tl = _tiles(t_tokens, seq)
    l = 0

    n_rows = -(-(n_batch + 1) // SUBLANES) * SUBLANES
    cstack = jnp.concatenate([c, c_ctx[None, :], jnp.zeros((n_rows - n_batch - 1, d), F32)], axis=0)
    mod = _mod_call(cstack, w_mod[l], b_mod[l])
    mod3 = mod.reshape(n_rows, 1, N_MOD * d)

    w_in_b = w_in[l].astype(BF16)
    x2d = x.reshape(t_tokens, d)
    bpb = seq // tl["tok"]
    p_x = _inproj_call(x2d, mod3, lambda i: i // bpb, norm1_g[l], w_in_b, tl["tok"])
    ctx_tm = ctx_len if ctx_len <= 512 else 256
    p_c = _inproj_call(ctx.reshape(n_batch * ctx_len, d), mod3, lambda i: n_batch, norm1_g[l],
                       w_in_b[:, :2 * w_lru], ctx_tm)

    lru_args = (lru_conv_w[l], lru_conv_b[l], lru_wa[l], lru_ba[l], lru_wi[l], lru_bi[l], lru_lambda[l])
    _, h_ctx = _lru_call(p_c, nb_l, n_batch, ctx_len, *lru_args, jnp.zeros((n_batch, 2, w_lru), F32))
    y_lru, _ = _lru_call(p_x, nb_l, n_batch, seq, *lru_args, h_ctx)

    rows = seq // GRID_W
    tables = _dft_tables(rows)
    hfilt = _filt_call(seq, w_hy, hy_w1[l], hy_b1[l], hy_w2[l], hy_b2[l], hy_w3[l], hy_b3[l], hy_freq[l], hy_w4[l])
    kf = _spec_call(hfilt, seq, nb_h, tables[0], tables[1])
    cw, cb = hy_conv_w[l], hy_conv_b[l]
    s0 = 2 * nb_l
    z = _hyena_call(p_x, s0, p_x, s0 + nb_h, cw[:, :w_hy], cb[:w_hy], cw[:, w_hy:2 * w_hy], cb[w_hy:2 * w_hy],
                    hy_d[l, 0], kf, 0, tables, n_batch, seq, nb_h, True)
    hy = _hyena_call(z, 0, p_x, s0 + 2 * nb_h, cw[:, :w_hy], cb[:w_hy], cw[:, 2 * w_hy:], cb[2 * w_hy:],
                     hy_d[l, 1], kf, 1, tables, n_batch, seq, nb_h, False)

    hx1, m, logits = _outproj_call(y_lru, hy, x2d, mod3, seq, gn_lru[l], gn_hy[l], norm2_g[l],
                                   w_out[l].astype(BF16), router_w[l], router_b[l], tl["tok"])

    idx, gate, rank, counts = _route_call(logits, tl["route"])
    tm = tl["moe_tm"]
    counts = counts[0]
    pcnt = (counts + tm - 1) // tm * tm
    pend = jnp.cumsum(pcnt)
    pstart = pend - pcnt
    dest = (pstart[idx] + rank).reshape(-1).astype(I32)
    nblk = t_tokens * TOP_K // tm + ne
    cap = nblk * tm
    n_used = (pend[-1] // tm).astype(I32).reshape(1)
    blk_ids = jnp.arange(nblk, dtype=I32)
    block_expert = jnp.minimum(jnp.sum((pend[None, :] <= (blk_ids * tm)[:, None]).astype(I32), axis=1), ne - 1)
    block_expert = jnp.where(blk_ids < n_used[0], block_expert, block_expert[n_used[0] - 1])

    xs = _dispatch_call(dest, pend.astype(I32), pcnt.astype(I32), n_used, m, cap, tm, tl["disp"])
    d_ff = exp_w_down.shape[2]
    tf = 256 if d_ff % 256 == 0 else d_ff
    y = _moe_call(block_expert, n_used, xs, exp_w_gu[l], exp_b_gu[l], exp_w_down[l], exp_b_down[l], tm, tf)
    out = _combine_call(dest, gate, hx1, mod3, seq, final_g, y, tl["comb"])
    return out.reshape(n_batch, seq, d)
```

```python
import functools
import math

import jax
import jax.numpy as jnp
from jax import lax
from jax.experimental import pallas as pl
from jax.experimental.pallas import tpu as pltpu

F32 = jnp.float32
BF16 = jnp.bfloat16
I32 = jnp.int32
U32 = jnp.uint32
HI = lax.Precision.HIGHEST

GRID_W = 64
EPS = 1e-6
N_MOD = 6
LRU_CONV = 4
LRU_C = 8.0
HY_CONV = 3
HY_ORDER = 2
HY_BANDS = 16
HY_TARGET = 1e-2
HY_FAST = 0.3
HY_SLOW = 1.5
TOP_K = 4
SWIGLU_LIMIT = 7.0
SWIGLU_ALPHA = 1.702

LANES = 128
SUBLANES = 8
VMEM_LIMIT_BYTES = 56 * 1024 * 1024

DFT_N1 = 2 * GRID_W
K1_USED = DFT_N1 // 2 + 1
K1_PAD = 72
assert K1_PAD % SUBLANES == 0 and K1_PAD >= K1_USED
LOOP_UNROLL = 16
K1_UNROLL = 8
assert (K1_USED - 1) % K1_UNROLL == 0


def _cparams(*sem):
    return pltpu.CompilerParams(dimension_semantics=sem, vmem_limit_bytes=VMEM_LIMIT_BYTES)


def _rms(x):
    return x * lax.rsqrt(jnp.mean(x * x, axis=-1, keepdims=True) + EPS)


def _pack_halves(v):
    h = v.shape[-1] // 2
    a = pltpu.bitcast(v[:, :h].astype(BF16).astype(F32), U32)
    b = pltpu.bitcast(v[:, h:].astype(BF16).astype(F32), U32)
    return a | (b >> 16)


def _unpack_halves(w):
    return pltpu.bitcast(w & jnp.uint32(0xFFFF0000), F32), pltpu.bitcast(w << 16, F32)


def _mod_kernel(c_ref, w_ref, b_ref, o_ref):
    c = c_ref[...]
    s = c * jax.nn.sigmoid(c)
    o_ref[...] = jnp.dot(s, w_ref[...], preferred_element_type=F32, precision=HI) + b_ref[...]


def _mod_call(cstack, w_mod, b_mod):
    rows, d = cstack.shape
    n = w_mod.shape[1]
    tn = 1536 if n % 1536 == 0 else n
    return pl.pallas_call(
        _mod_kernel,
        out_shape=jax.ShapeDtypeStruct((rows, n), F32),
        grid=(n // tn,),
        in_specs=[pl.BlockSpec((rows, d), lambda j: (0, 0)),
                  pl.BlockSpec((d, tn), lambda j: (0, j)),
                  pl.BlockSpec((1, tn), lambda j: (0, j))],
        out_specs=pl.BlockSpec((rows, tn), lambda j: (0, j)),
        compiler_params=_cparams("arbitrary"),
        name="mod",
    )(cstack, w_mod, b_mod.reshape(1, n))


def _inproj_kernel(x_ref, mod_ref, g_ref, w_ref, o_ref, *, d, n_slabs, nchunk):
    x = x_ref[...]
    sh = mod_ref[0, :, 0 * d:1 * d]
    sc = mod_ref[0, :, 1 * d:2 * d]
    a = (_rms(x) * g_ref[...]) * (1.0 + sc) + sh
    ab = a.astype(BF16)
    spc = nchunk // LANES
    for j in range(n_slabs // spc):
        p = jnp.dot(ab, w_ref[:, j * nchunk:(j + 1) * nchunk], preferred_element_type=F32)
        for s in range(spc):
            o_ref[j * spc + s] = p[:, s * LANES:(s + 1) * LANES].astype(BF16)


def _inproj_call(x2d, mod3, mod_row_of_block, g, w_bf16, tm):
    t, d = x2d.shape
    n = w_bf16.shape[1]
    n_slabs = n // LANES
    nchunk = 512 if n % 512 == 0 else LANES
    kern = functools.partial(_inproj_kernel, d=d, n_slabs=n_slabs, nchunk=nchunk)
    return pl.pallas_call(
        kern,
        out_shape=jax.ShapeDtypeStruct((n_slabs, t, LANES), BF16),
        grid=(t // tm,),
        in_specs=[pl.BlockSpec((tm, d), lambda i: (i, 0)),
                  pl.BlockSpec((1, 1, N_MOD * d), lambda i: (mod_row_of_block(i), 0, 0)),
                  pl.BlockSpec((1, d), lambda i: (0, 0)),
                  pl.BlockSpec((d, n), lambda i: (0, 0), pipeline_mode=pl.Buffered(1))],
        out_specs=pl.BlockSpec((n_slabs, tm, LANES), lambda i: (0, i, 0)),
        compiler_params=_cparams("arbitrary"),
        name="inproj",
    )(x2d, mod3, g.reshape(1, d), w_bf16)


def _gelu_tanh(x):
    return 0.5 * x * (1.0 + jnp.tanh(math.sqrt(2.0 / math.pi) * (x + 0.044715 * (x * x * x))))


def _scan_tile(a, b, row, reverse):
    for s in (1, 2, 4):
        if reverse:
            m = row < SUBLANES - s
            sh = SUBLANES - s
        else:
            m = row >= s
            sh = s
        a_s = jnp.where(m, pltpu.roll(a, sh, 0), 1.0)
        b_s = jnp.where(m, pltpu.roll(b, sh, 0), 0.0)
        b = a * b_s + b
        a = a * a_s
    return a, b


def _lru_kernel(r_ref, g_ref, cw_ref, cb_ref, wa_ref, wi_ref, ba_ref, bi_ref, lam_ref, h0_ref,
                y_ref, hl_ref, xp_ref, hf_ref, ab_ref, bb_ref, *, t_len, tc):
    nc = t_len // tc
    ntile = tc // SUBLANES
    zero8 = jnp.zeros((SUBLANES, LANES), F32)
    xp_ref[0:SUBLANES, :] = zero8
    xp_ref[t_len + SUBLANES:t_len + 2 * SUBLANES, :] = zero8

    def copy_body(ci, c):
        t0 = pl.multiple_of(ci * tc, tc)
        xp_ref[pl.ds(t0 + SUBLANES, tc), :] = r_ref[0, pl.ds(t0, tc), :].astype(F32)
        return c
    lax.fori_loop(0, nc, copy_body, 0)

    row = lax.broadcasted_iota(I32, (SUBLANES, LANES), 0)
    cw = cw_ref[...]
    cb = cb_ref[...]
    lam = lam_ref[...]
    sp = jnp.log1p(jnp.exp(-lam))
    wab = [wa_ref[d, 0].astype(BF16) for d in range(2)]
    wib = [wi_ref[d, 0].astype(BF16) for d in range(2)]

    def coeffs(u, ub, d):
        ga = jnp.dot(ub, wab[d], preferred_element_type=F32) + ba_ref[d:d + 1, :]
        gi = jnp.dot(ub, wib[d], preferred_element_type=F32) + bi_ref[d:d + 1, :]
        rg = jax.nn.sigmoid(ga)
        ig = jax.nn.sigmoid(gi)
        a = jnp.exp((-LRU_C) * rg * sp[d:d + 1, :])
        b = jnp.sqrt(1.0 - a * a) * (ig * u)
        return a, b

    def fwd_body(ci, hcar):
        t0 = pl.multiple_of(ci * tc, tc)
        blk = xp_ref[pl.ds(t0, tc + 2 * SUBLANES), :]
        u = (cw[0:1, :] * blk[6:6 + tc] + cw[1:2, :] * blk[7:7 + tc]
             + cw[2:3, :] * blk[8:8 + tc] + cw[3:4, :] * blk[9:9 + tc]) + cb
        ub = u.astype(BF16)
        a0, b0 = coeffs(u, ub, 0)
        a1, b1 = coeffs(u, ub, 1)
        ab_ref[pl.ds(t0, tc), :] = a1
        bb_ref[pl.ds(t0, tc), :] = b1
        for j in range(ntile):
            at, bt = _scan_tile(a0[j * 8:(j + 1) * 8], b0[j * 8:(j + 1) * 8], row, False)
            hf_ref[pl.ds(t0 + j * 8, 8), :] = at * hcar + bt
            atot = jnp.broadcast_to(at[7:8, :], (SUBLANES, LANES))
            btot = jnp.broadcast_to(bt[7:8, :], (SUBLANES, LANES))
            hcar = atot * hcar + btot
        return hcar

    h0f = jnp.broadcast_to(h0_ref[0, 0:1, :], (SUBLANES, LANES))
    hfin = lax.fori_loop(0, nc, fwd_body, h0f)
    hl_ref[0, 0:1, :] = hfin[0:1, :]

    def bwd_body(k, hcar):
        ci = nc - 1 - k
        t0 = pl.multiple_of(ci * tc, tc)
        a1 = ab_ref[pl.ds(t0, tc), :]
        b1 = bb_ref[pl.ds(t0, tc), :]
        gg = g_ref[0, pl.ds(t0, tc), :].astype(F32)
        gl = _gelu_tanh(gg)
        for j in reversed(range(ntile)):
            at, bt = _scan_tile(a1[j * 8:(j + 1) * 8], b1[j * 8:(j + 1) * 8], row, True)
            hb = at * hcar + bt
            hf = hf_ref[pl.ds(t0 + j * 8, 8), :]
            y_ref[0, pl.ds(t0 + j * 8, 8), :] = ((hf + hb) * gl[j * 8:(j + 1) * 8]).astype(y_ref.dtype)
            atot = jnp.broadcast_to(at[0:1, :], (SUBLANES, LANES))
            btot = jnp.broadcast_to(bt[0:1, :], (SUBLANES, LANES))
            hcar = atot * hcar + btot
        return hcar

    h0b = jnp.broadcast_to(h0_ref[0, 1:2, :], (SUBLANES, LANES))
    hfin_b = lax.fori_loop(0, nc, bwd_body, h0b)
    hl_ref[0, 1:2, :] = hfin_b[0:1, :]


def _lru_call(p_slabs, nb, n_batch, t_len, conv_w, conv_b, wa, ba, wi, bi, lam, h0):
    tc = 128 if t_len % 128 == 0 else t_len
    kern = functools.partial(_lru_kernel, t_len=t_len, tc=tc)
    w_lru = nb * LANES
    return pl.pallas_call(
        kern,
        out_shape=[jax.ShapeDtypeStruct((nb, n_batch * t_len, LANES), F32),
                   jax.ShapeDtypeStruct((n_batch, 2, w_lru), F32)],
        grid=(nb, n_batch),
        in_specs=[pl.BlockSpec((1, t_len, LANES), lambda n, b: (n, b, 0)),
                  pl.BlockSpec((1, t_len, LANES), lambda n, b: (nb + n, b, 0)),
                  pl.BlockSpec((LRU_CONV, LANES), lambda n, b: (0, n)),
                  pl.BlockSpec((1, LANES), lambda n, b: (0, n)),
                  pl.BlockSpec((2, 1, LANES, LANES), lambda n, b: (0, n, 0, 0)),
                  pl.BlockSpec((2, 1, LANES, LANES), lambda n, b: (0, n, 0, 0)),
                  pl.BlockSpec((2, LANES), lambda n, b: (0, n)),
                  pl.BlockSpec((2, LANES), lambda n, b: (0, n)),
                  pl.BlockSpec((2, LANES), lambda n, b: (0, n)),
                  pl.BlockSpec((1, 2, LANES), lambda n, b: (b, 0, n))],
        out_specs=[pl.BlockSpec((1, t_len, LANES), lambda n, b: (n, b, 0)),
                   pl.BlockSpec((1, 2, LANES), lambda n, b: (b, 0, n))],
        scratch_shapes=[pltpu.VMEM((t_len + 2 * SUBLANES, LANES), F32),
                        pltpu.VMEM((t_len, LANES), F32),
                        pltpu.VMEM((t_len, LANES), F32),
                        pltpu.VMEM((t_len, LANES), F32)],
        compiler_params=_cparams("arbitrary", "arbitrary"),
        name="lru",
    )(p_slabs, p_slabs, conv_w, conv_b.reshape(1, w_lru), wa, wi, ba, bi, lam, h0)


def _filt_kernel(w1_ref, b1_ref, w2_ref, b2_ref, w3_ref, b3_ref, fr_ref, w4_ref, o_ref,
                 *, l_len, rows, tl, w_hy):
    i = pl.program_id(0)
    rho = i * tl + lax.broadcasted_iota(I32, (tl, 1), 0)
    tidx = ((rho % GRID_W) * rows + rho // GRID_W).astype(F32)
    tt = tidx * (1.0 / (l_len - 1))
    ww = (2.0 * math.pi) * tidx / l_len
    band = lax.broadcasted_iota(I32, (1, HY_BANDS), 1).astype(F32)
    f = 1e-4 + band * ((HY_BANDS - 1 - 1e-4) / (HY_BANDS - 1))
    fw = f * ww
    w1 = w1_ref[...]
    pre = (tt * w1[0:1, :]
           + jnp.dot(jnp.cos(fw), w1[1:1 + HY_BANDS, :], preferred_element_type=F32, precision=HI)
           + jnp.dot(-jnp.sin(fw), w1[1 + HY_BANDS:1 + 2 * HY_BANDS, :], preferred_element_type=F32, precision=HI))
    fr = fr_ref[...]
    h = jnp.sin(fr * (pre + b1_ref[...]))
    h = jnp.sin(fr * (jnp.dot(h, w2_ref[...], preferred_element_type=F32, precision=HI) + b2_ref[...]))
    h = jnp.sin(fr * (jnp.dot(h, w3_ref[...], preferred_element_type=F32, precision=HI) + b3_ref[...]))
    max_decay = math.log(HY_TARGET) / HY_FAST
    min_decay = math.log(HY_TARGET) / HY_SLOW
    nblk = w_hy // LANES
    h_hi = h.astype(BF16)
    h_lo = (h - h_hi.astype(F32)).astype(BF16)
    for cb in range(nblk):
        ch = (cb * LANES + lax.broadcasted_iota(I32, (1, LANES), 1)).astype(F32)
        delta = jnp.abs(min_decay + ch * ((max_decay - min_decay) / (w_hy - 1)))
        decay = jnp.exp(-tt * delta)
        for od in range(HY_ORDER * 2):
            col = od * w_hy + cb * LANES
            w4 = w4_ref[:, col:col + LANES]
            w_hi = w4.astype(BF16)
            w_lo = (w4 - w_hi.astype(F32)).astype(BF16)
            v = (jnp.dot(h_hi, w_hi, preferred_element_type=F32) + jnp.dot(h_lo, w_hi, preferred_element_type=F32)
                 + jnp.dot(h_hi, w_lo, preferred_element_type=F32))
            o_ref[od * nblk + cb] = v * decay


def _filt_call(l_len, w_hy, w1, b1, w2, b2, w3, b3, freq, w4):
    rows = l_len // GRID_W
    tl = 512 if l_len % 512 == 0 else l_len
    fh = w2.shape[0]
    nslab = HY_ORDER * 2 * (w_hy // LANES)
    kern = functools.partial(_filt_kernel, l_len=l_len, rows=rows, tl=tl, w_hy=w_hy)
    full = lambda a: pl.BlockSpec(a.shape, lambda i: (0,) * a.ndim)
    args = (w1, b1.reshape(1, fh), w2, b2.reshape(1, fh), w3, b3.reshape(1, fh), freq.reshape(1, fh), w4)
    return pl.pallas_call(
        kern,
        out_shape=jax.ShapeDtypeStruct((nslab, l_len, LANES), F32),
        grid=(l_len // tl,),
        in_specs=[full(a) for a in args],
        out_specs=pl.BlockSpec((nslab, tl, LANES), lambda i: (0, i, 0)),
        compiler_params=_cparams("arbitrary"),
        name="filt",
    )(*args)


def _dft_tables(rows):
    n_len = DFT_N1 * rows
    k1 = jnp.arange(K1_PAD, dtype=I32)
    r = jnp.arange(rows, dtype=I32)
    w = jnp.arange(GRID_W, dtype=I32)
    idx = (k1[None, :, None] * (r[:, None, None] + rows * w[None, None, :])) % n_len
    ang = idx.astype(F32) * (2.0 * math.pi / n_len)
    valid = (k1 < K1_USED)[None, :, None]
    ca = jnp.where(valid, jnp.cos(ang), 0.0)
    sa = jnp.where(valid, jnp.sin(ang), 0.0)
    ta = jnp.stack([ca, -sa], axis=2).reshape(rows, 2 * K1_PAD, GRID_W)
    td = jnp.transpose(ta, (0, 2, 1))
    k2 = jnp.arange(rows, dtype=I32)
    th = ((k2[:, None] * r[None, :]) % rows).astype(F32) * (2.0 * math.pi / rows)
    c, s = jnp.cos(th), jnp.sin(th)
    tb = jnp.concatenate([jnp.stack([c, s], axis=2).reshape(rows, 2 * rows),
                          jnp.stack([-s, c], axis=2).reshape(rows, 2 * rows)], axis=0)
    tcm = jnp.stack([jnp.concatenate([c, -s], axis=1), jnp.concatenate([s, c], axis=1)], axis=1).reshape(2 * rows, 2 * rows)
    return ta.astype(BF16), tb.astype(BF16), tcm.astype(BF16), td.astype(BF16)


def _pitch_r(rows):
    return rows + SUBLANES


def _stage_a(tile_fn, ta_ref, s1_ref, rows):
    pitch = _pitch_r(rows)

    def body(r, c):
        out = jnp.dot(ta_ref[r], tile_fn(r), preferred_element_type=F32)
        words = pltpu.bitcast(out.astype(BF16), U32)
        s1_ref[pl.ds(r, K1_PAD, stride=pitch), :] = words
        return c
    lax.fori_loop(0, rows, body, 0, unroll=min(LOOP_UNROLL, rows))


def _loop_k1(body):
    lax.fori_loop(0, K1_USED - 1, body, 0, unroll=K1_UNROLL)
    body(K1_USED - 1, 0)


def _stage_b(k1, tb_ref, s1_ref, rows):
    pitch = _pitch_r(rows)
    t = s1_ref[pl.ds(pl.multiple_of(k1 * pitch, SUBLANES), rows), :]
    y = jnp.dot(tb_ref[...], pltpu.bitcast(t, BF16), preferred_element_type=F32)
    return y[:rows], y[rows:]


def _spec_kernel(hf_ref, hb_ref, ta_ref, tb_ref, kf_ref, s1_ref, *, rows):
    n_len = DFT_N1 * rows

    def scale(k1):
        return jnp.where((k1 == 0) | (k1 == K1_USED - 1), 1.0, 2.0) * (1.0 / n_len)

    _stage_a(lambda r: hf_ref[0, pl.ds(pl.multiple_of(r * GRID_W, GRID_W), GRID_W), :].astype(BF16),
             ta_ref, s1_ref, rows)

    def b_fwd(k1, c):
        yre, yim = _stage_b(k1, tb_ref, s1_ref, rows)
        sc = scale(k1)
        kf_ref[0, 0, k1, 0] = yre * sc
        kf_ref[0, 0, k1, 1] = yim * sc
        return c
    _loop_k1(b_fwd)

    rid = lax.broadcasted_iota(I32, (GRID_W, LANES), 0)

    def tile_b(r):
        x = hb_ref[0, pl.ds(pl.multiple_of(r * GRID_W, GRID_W), GRID_W), :]
        x = jnp.where((rid == 0) & (r == 0), 0.0, x)
        return x.astype(BF16)
    _stage_a(tile_b, ta_ref, s1_ref, rows)

    def b_bwd(k1, c):
        yre, yim = _stage_b(k1, tb_ref, s1_ref, rows)
        sc = scale(k1)
        kf_ref[0, 0, k1, 0] += yre * sc
        kf_ref[0, 0, k1, 1] -= yim * sc
        return c
    _loop_k1(b_bwd)


def _spec_call(hfilt, l_len, nblk, ta, tb):
    rows = l_len // GRID_W
    kern = functools.partial(_spec_kernel, rows=rows)
    return pl.pallas_call(
        kern,
        out_shape=jax.ShapeDtypeStruct((HY_ORDER, nblk, K1_USED, 2, rows, LANES), F32),
        grid=(HY_ORDER, nblk),
        in_specs=[pl.BlockSpec((1, l_len, LANES), lambda o, c: (o * 2 * nblk + c, 0, 0)),
                  pl.BlockSpec((1, l_len, LANES), lambda o, c: (o * 2 * nblk + nblk + c, 0, 0)),
                  pl.BlockSpec(ta.shape, lambda o, c: (0, 0, 0), pipeline_mode=pl.Buffered(1)),
                  pl.BlockSpec(tb.shape, lambda o, c: (0, 0), pipeline_mode=pl.Buffered(1))],
        out_specs=pl.BlockSpec((1, 1, K1_USED, 2, rows, LANES), lambda o, c: (o, c, 0, 0, 0, 0)),
        scratch_shapes=[pltpu.VMEM((K1_PAD * _pitch_r(rows), LANES), U32)],
        compiler_params=_cparams("arbitrary", "arbitrary"),
        name="spec",
    )(hfilt, hfilt, ta, tb)


def _conv_grid_order(src_ref, dst_ref, w_ref, b_ref, rows):
    w0, w1, w2, bb = w_ref[0:1, :], w_ref[1:2, :], w_ref[2:3, :], b_ref[...]
    g = GRID_W
    rid = lax.broadcasted_iota(I32, (g, LANES), 0)

    def grp(r0):
        return src_ref[0, pl.ds(r0, g), :].astype(F32)

    last = grp((rows - 1) * g)
    prev0 = jnp.where(rid >= 1, pltpu.roll(last, 1, 0), 0.0)
    dst_ref[pl.ds(0, g), :] = (w0 * prev0 + w1 * grp(0) + w2 * grp(g) + bb).astype(dst_ref.dtype)
    first = grp(0)
    nxt = jnp.where(rid < g - 1, pltpu.roll(first, g - 1, 0), 0.0)
    dst_ref[pl.ds((rows - 1) * g, g), :] = (w0 * grp((rows - 2) * g) + w1 * last + w2 * nxt + bb).astype(dst_ref.dtype)

    def body(r, c):
        rm = pl.multiple_of((r - 1) * g, g)
        r0 = pl.multiple_of(r * g, g)
        rp = pl.multiple_of((r + 1) * g, g)
        v = w0 * grp(rm) + w1 * grp(r0) + w2 * grp(rp) + bb
        dst_ref[pl.ds(r0, g), :] = v.astype(dst_ref.dtype)
        return c
    lax.fori_loop(1, rows - 1, body, 0, unroll=2)


def _hyena_kernel(sig_ref, gate_ref, cws_ref, cbs_ref, cwg_ref, cbg_ref, d_ref, kf_ref,
                  ta_ref, tb_ref, tc_ref, td_ref, o_ref, u_ref, gt_ref, s1_ref, s2_ref,
                  *, rows, conv_sig):
    g = GRID_W

    @pl.when((pl.program_id(0) == 0) & (pl.program_id(1) == 0))
    def _():
        s2_ref[...] = jnp.zeros(s2_ref.shape, U32)

    if conv_sig:
        _conv_grid_order(sig_ref, u_ref, cws_ref, cbs_ref, rows)
    else:
        def cp(r, c):
            r0 = pl.multiple_of(r * g, g)
            u_ref[pl.ds(r0, g), :] = sig_ref[0, pl.ds(r0, g), :]
            return c
        lax.fori_loop(0, rows, cp, 0, unroll=min(LOOP_UNROLL, rows))
    _conv_grid_order(gate_ref, gt_ref, cwg_ref, cbg_ref, rows)

    _stage_a(lambda r: u_ref[pl.ds(pl.multiple_of(r * g, g), g), :], ta_ref, s1_ref, rows)

    def bc_body(k1, c):
        yre, yim = _stage_b(k1, tb_ref, s1_ref, rows)
        kre = kf_ref[0, 0, k1, 0]
        kim = kf_ref[0, 0, k1, 1]
        zre = yre * kre - yim * kim
        zim = yre * kim + yim * kre
        z = jnp.concatenate([zre, zim], axis=0).astype(BF16)
        v = jnp.dot(tc_ref[...], z, preferred_element_type=F32)
        s2_ref[pl.ds(k1, rows, stride=K1_PAD), :] = pltpu.bitcast(v.astype(BF16), U32)
        return c
    _loop_k1(bc_body)

    dd = d_ref[...]

    def d_body(r, c):
        r0 = pl.multiple_of(r * g, g)
        t = s2_ref[pl.ds(pl.multiple_of(r * K1_PAD, SUBLANES), K1_PAD), :]
        y = jnp.dot(td_ref[r], pltpu.bitcast(t, BF16), preferred_element_type=F32)
        uu = u_ref[pl.ds(r0, g), :].astype(F32)
        o_ref[0, pl.ds(r0, g), :] = (gt_ref[pl.ds(r0, g), :] * (y + dd * uu)).astype(o_ref.dtype)
        return c
    lax.fori_loop(0, rows, d_body, 0, unroll=min(LOOP_UNROLL, rows))


def _hyena_call(sig_arr, sig_slab0, gate_arr, gate_slab0, conv_w_sig, conv_b_sig, conv_w_gate, conv_b_gate,
                d_vec, kf, order, tables, n_batch, l_len, nblk, conv_sig):
    rows = l_len // GRID_W
    ta, tb, tcm, td = tables
    kern = functools.partial(_hyena_kernel, rows=rows, conv_sig=conv_sig)
    w_hy = nblk * LANES
    const = lambda a: pl.BlockSpec(a.shape, lambda c, b: (0,) * a.ndim, pipeline_mode=pl.Buffered(1))
    return pl.pallas_call(
        kern,
        out_shape=jax.ShapeDtypeStruct((nblk, n_batch * l_len, LANES), BF16),
        grid=(nblk, n_batch),
        in_specs=[pl.BlockSpec((1, l_len, LANES), lambda c, b: (sig_slab0 + c, b, 0)),
                  pl.BlockSpec((1, l_len, LANES), lambda c, b: (gate_slab0 + c, b, 0)),
                  pl.BlockSpec((HY_CONV, LANES), lambda c, b: (0, c)),
                  pl.BlockSpec((1, LANES), lambda c, b: (0, c)),
                  pl.BlockSpec((HY_CONV, LANES), lambda c, b: (0, c)),
                  pl.BlockSpec((1, LANES), lambda c, b: (0, c)),
                  pl.BlockSpec((1, LANES), lambda c, b: (0, c)),
                  pl.BlockSpec((1, 1, K1_USED, 2, rows, LANES), lambda c, b: (order, c, 0, 0, 0, 0),
                               pipeline_mode=pl.Buffered(1)),
                  const(ta), const(tb), const(tcm), const(td)],
        out_specs=pl.BlockSpec((1, l_len, LANES), lambda c, b: (c, b, 0)),
        scratch_shapes=[pltpu.VMEM((l_len, LANES), BF16),
                        pltpu.VMEM((l_len, LANES), F32),
                        pltpu.VMEM((K1_PAD * _pitch_r(rows), LANES), U32),
                        pltpu.VMEM((rows * K1_PAD, LANES), U32)],
        compiler_params=_cparams("arbitrary", "arbitrary"),
        name="hyena%d" % order,
    )(sig_arr, gate_arr, conv_w_sig, conv_b_sig.reshape(1, w_hy), conv_w_gate, conv_b_gate.reshape(1, w_hy),
      d_vec.reshape(1, w_hy), kf, ta, tb, tcm, td)


def _outproj_kernel(yl_ref, hy_ref, x_ref, mod_ref, gl_ref, gh_ref, n2_ref, w_ref, rw2_ref, rb_ref,
                    hx_ref, m_ref, lg_ref, *, d, nb_l, nb_h, ne):
    yl = jnp.concatenate([yl_ref[j].astype(F32) for j in range(nb_l)], axis=-1)
    hy = jnp.concatenate([hy_ref[j].astype(F32) for j in range(nb_h)], axis=-1)
    cat = jnp.concatenate([_rms(yl) * gl_ref[...], _rms(hy) * gh_ref[...]], axis=-1).astype(BF16)
    out = jnp.dot(cat, w_ref[...], preferred_element_type=F32)
    g1 = mod_ref[0, :, 2 * d:3 * d]
    sh2 = mod_ref[0, :, 3 * d:4 * d]
    sc2 = mod_ref[0, :, 4 * d:5 * d]
    hx = x_ref[...] + g1 * out
    hx_ref[...] = hx
    m = (_rms(hx) * n2_ref[...]) * (1.0 + sc2) + sh2
    m_ref[...] = _pack_halves(m)
    m_hi = m.astype(BF16)
    m_lo = (m - m_hi.astype(F32)).astype(BF16)
    a = jnp.dot(m_hi, rw2_ref[...], preferred_element_type=F32)
    b = jnp.dot(m_lo, rw2_ref[:, :ne], preferred_element_type=F32)
    lg_ref[...] = (a[:, :ne] + b) + a[:, ne:] + rb_ref[...]


def _outproj_call(y_lru, hy, x2d, mod3, rows_per_batch, gn_lru, gn_hy, norm2_g, w_out_bf16, router_w, router_b, tm):
    t, d = x2d.shape
    nb_l, nb_h = y_lru.shape[0], hy.shape[0]
    ne = router_w.shape[1]
    bpb = rows_per_batch // tm
    rw_hi = router_w.astype(BF16)
    rw_lo = (router_w - rw_hi.astype(F32)).astype(BF16)
    rw2 = jnp.concatenate([rw_hi, rw_lo], axis=1)
    kern = functools.partial(_outproj_kernel, d=d, nb_l=nb_l, nb_h=nb_h, ne=ne)
    return pl.pallas_call(
        kern,
        out_shape=[jax.ShapeDtypeStruct((t, d), F32), jax.ShapeDtypeStruct((t, d // 2), U32),
                   jax.ShapeDtypeStruct((t, ne), F32)],
        grid=(t // tm,),
        in_specs=[pl.BlockSpec((nb_l, tm, LANES), lambda i: (0, i, 0)),
                  pl.BlockSpec((nb_h, tm, LANES), lambda i: (0, i, 0)),
                  pl.BlockSpec((tm, d), lambda i: (i, 0)),
                  pl.BlockSpec((1, 1, N_MOD * d), lambda i: (i // bpb, 0, 0)),
                  pl.BlockSpec((1, nb_l * LANES), lambda i: (0, 0)),
                  pl.BlockSpec((1, nb_h * LANES), lambda i: (0, 0)),
                  pl.BlockSpec((1, d), lambda i: (0, 0)),
                  pl.BlockSpec(w_out_bf16.shape, lambda i: (0, 0), pipeline_mode=pl.Buffered(1)),
                  pl.BlockSpec((d, 2 * ne), lambda i: (0, 0)),
                  pl.BlockSpec((1, ne), lambda i: (0, 0))],
        out_specs=[pl.BlockSpec((tm, d), lambda i: (i, 0)),
                   pl.BlockSpec((tm, d // 2), lambda i: (i, 0)),
                   pl.BlockSpec((tm, ne), lambda i: (i, 0))],
        compiler_params=_cparams("arbitrary"),
        name="outproj",
    )(y_lru, hy, x2d, mod3, gn_lru.reshape(1, -1), gn_hy.reshape(1, -1), norm2_g.reshape(1, d),
      w_out_bf16, rw2, router_b.reshape(1, ne))


def _route_kernel(lg_ref, idx_ref, gate_ref, rank_ref, cnt_ref, run_ref, *, tb, ne):
    @pl.when(pl.program_id(0) == 0)
    def _():
        run_ref[...] = jnp.zeros(run_ref.shape, F32)

    l = lg_ref[...]
    lane = lax.broadcasted_iota(I32, (tb, ne), 1)
    vals, idxs, ohs = [], [], []
    for _ in range(TOP_K):
        m = jnp.max(l, axis=-1, keepdims=True)
        ix = jnp.min(jnp.where(l == m, lane, ne), axis=-1, keepdims=True)
        sel = lane == ix
        vals.append(m)
        idxs.append(ix)
        ohs.append(sel.astype(F32))
        l = jnp.where(sel, -jnp.inf, l)
    es = [jnp.exp(v - vals[0]) for v in vals]
    den = es[0] + es[1] + es[2] + es[3]
    oh_all = ohs[0] + ohs[1] + ohs[2] + ohs[3]
    ri = lax.broadcasted_iota(I32, (tb, tb), 0)
    ci = lax.broadcasted_iota(I32, (tb, tb), 1)
    ltri = (ci < ri).astype(BF16)
    before = jnp.dot(ltri, oh_all.astype(BF16), preferred_element_type=F32) + run_ref[...]
    for k in range(TOP_K):
        idx_ref[:, k:k + 1] = idxs[k]
        gate_ref[:, k:k + 1] = es[k] / den
        rank_ref[:, k:k + 1] = jnp.sum(ohs[k] * before, axis=-1, keepdims=True).astype(I32)
    run_ref[...] += jnp.sum(oh_all, axis=0, keepdims=True)
    cnt_ref[...] = run_ref[...].astype(I32)


def _route_call(logits, tb):
    t, ne = logits.shape
    kern = functools.partial(_route_kernel, tb=tb, ne=ne)
    blk = lambda: pl.BlockSpec((tb, TOP_K), lambda i: (i, 0))
    return pl.pallas_call(
        kern,
        out_shape=[jax.ShapeDtypeStruct((t, TOP_K), I32), jax.ShapeDtypeStruct((t, TOP_K), F32),
                   jax.ShapeDtypeStruct((t, TOP_K), I32), jax.ShapeDtypeStruct((1, ne), I32)],
        grid=(t // tb,),
        in_specs=[pl.BlockSpec((tb, ne), lambda i: (i, 0))],
        out_specs=[blk(), blk(), blk(), pl.BlockSpec((1, ne), lambda i: (0, 0))],
        scratch_shapes=[pltpu.VMEM((1, ne), F32)],
        compiler_params=_cparams("arbitrary"),
        name="route",
    )(logits)


def _dispatch_kernel(dest_ref, pend_ref, pcnt_ref, nu_ref, m_ref, xs_hbm, zbuf, sem, zsem, *, tbd, tm, ne, nblk):
    i = pl.program_id(0)

    @pl.when(i == 0)
    def _():
        zbuf[...] = jnp.zeros(zbuf.shape, U32)

        def zero_block(row0):
            cp = pltpu.make_async_copy(zbuf, xs_hbm.at[pl.ds(pl.multiple_of(row0, tm), tm), :], zsem)
            cp.start()
            cp.wait()

        def zfill(e, c):
            @pl.when(pcnt_ref[e] > 0)
            def _():
                zero_block(pend_ref[e] - tm)
            return c
        lax.fori_loop(0, ne, zfill, 0)

        def ztail(bi, c):
            zero_block(bi * tm)
            return c
        lax.fori_loop(nu_ref[0], nblk, ztail, 0)

    def body(t, c):
        for k in range(TOP_K):
            d = dest_ref[t * TOP_K + k]
            pltpu.make_async_copy(m_ref.at[pl.ds(t, 1), :], xs_hbm.at[pl.ds(d, 1), :], sem).start(priority=k % 2)
        return c
    lax.fori_loop(0, tbd, body, 0, unroll=4)
    for k in range(TOP_K):
        pltpu.make_async_copy(m_ref, xs_hbm.at[pl.ds(0, tbd), :], sem).wait()


def _dispatch_call(dest_flat, pend, pcnt, n_used, m, cap, tm, tbd):
    t, d = m.shape
    ne = pend.shape[0]
    kern = functools.partial(_dispatch_kernel, tbd=tbd, tm=tm, ne=ne, nblk=cap // tm)
    return pl.pallas_call(
        kern,
        out_shape=jax.ShapeDtypeStruct((cap, d), U32),
        grid=(t // tbd,),
        in_specs=[pl.BlockSpec((tbd * TOP_K,), lambda i: (i,), memory_space=pltpu.SMEM),
                  pl.BlockSpec(memory_space=pltpu.SMEM),
                  pl.BlockSpec(memory_space=pltpu.SMEM),
                  pl.BlockSpec(memory_space=pltpu.SMEM),
                  pl.BlockSpec((tbd, d), lambda i: (i, 0))],
        out_specs=pl.BlockSpec(memory_space=pl.ANY),
        scratch_shapes=[pltpu.VMEM((tm, d), U32), pltpu.SemaphoreType.DMA(()), pltpu.SemaphoreType.DMA(())],
        compiler_params=_cparams("arbitrary"),
        name="dispatch",
    )(dest_flat, pend, pcnt, n_used, m)


def _moe_kernel(be_ref, nu_ref, x_ref, wg_ref, wl_ref, bg_ref, bl_ref, wd_ref, bd_ref, y_ref, xb_ref, acc_ref):
    i = pl.program_id(0)
    j = pl.program_id(1)
    nj = pl.num_programs(1)
    half = x_ref.shape[-1]

    @pl.when(i < nu_ref[0])
    def _():
        @pl.when(j == 0)
        def _():
            hi, lo = _unpack_halves(x_ref[...])
            xb_ref[:, :half] = hi.astype(BF16)
            xb_ref[:, half:] = lo.astype(BF16)
            acc_ref[...] = jnp.broadcast_to(bd_ref[0], acc_ref.shape)

        xb = xb_ref[...]
        glu = jnp.dot(xb, wg_ref[0].astype(BF16), preferred_element_type=F32) + bg_ref[0]
        lin = jnp.dot(xb, wl_ref[0].astype(BF16), preferred_element_type=F32) + bl_ref[0]
        glu = jnp.minimum(glu, SWIGLU_LIMIT)
        lin = jnp.clip(lin, -SWIGLU_LIMIT, SWIGLU_LIMIT)
        act = glu * jax.nn.sigmoid(SWIGLU_ALPHA * glu) * (lin + 1.0)
        acc_ref[...] += jnp.dot(act.astype(BF16), wd_ref[0].astype(BF16), preferred_element_type=F32)

        @pl.when(j == nj - 1)
        def _():
            y_ref[...] = _pack_halves(acc_ref[...])

    @pl.when((i >= nu_ref[0]) & (j == 0))
    def _():
        y_ref[...] = jnp.zeros(y_ref.shape, U32)


def _moe_call(block_expert, n_used, xs, w_gu, b_gu, w_down, b_down, tm, tf):
    cap, half = xs.shape
    d = 2 * half
    ne, _, two_ff = w_gu.shape
    d_ff = two_ff // 2
    nf = d_ff // tf
    nblk = cap // tm

    def jj(i, j, nu):
        return jnp.where(i < nu[0], j, nf - 1)

    grid_spec = pltpu.PrefetchScalarGridSpec(
        num_scalar_prefetch=2,
        grid=(nblk, nf),
        in_specs=[pl.BlockSpec((tm, half), lambda i, j, be, nu: (i, 0), pipeline_mode=pl.Buffered(1)),
                  pl.BlockSpec((1, d, tf), lambda i, j, be, nu: (be[i], 0, jj(i, j, nu))),
                  pl.BlockSpec((1, d, tf), lambda i, j, be, nu: (be[i], 0, nf + jj(i, j, nu))),
                  pl.BlockSpec((1, 1, tf), lambda i, j, be, nu: (be[i], 0, jj(i, j, nu))),
                  pl.BlockSpec((1, 1, tf), lambda i, j, be, nu: (be[i], 0, nf + jj(i, j, nu))),
                  pl.BlockSpec((1, tf, d), lambda i, j, be, nu: (be[i], jj(i, j, nu), 0)),
                  pl.BlockSpec((1, 1, d), lambda i, j, be, nu: (be[i], 0, 0))],
        out_specs=pl.BlockSpec((tm, half), lambda i, j, be, nu: (i, 0)),
        scratch_shapes=[pltpu.VMEM((tm, d), BF16), pltpu.VMEM((tm, d), F32)],
    )
    return pl.pallas_call(
        _moe_kernel,
        out_shape=jax.ShapeDtypeStruct((cap, half), U32),
        grid_spec=grid_spec,
        compiler_params=_cparams("arbitrary", "arbitrary"),
        name="moe",
    )(block_expert, n_used, xs, w_gu, w_gu, b_gu.reshape(ne, 1, two_ff), b_gu.reshape(ne, 1, two_ff),
      w_down, b_down.reshape(ne, 1, d))


def _combine_kernel(dest_ref, gate_ref, hx_ref, mod_ref, fg_ref, y_hbm, o_ref, buf, sem, *, tbc, d):
    def body(t, c):
        for k in range(TOP_K):
            dd = dest_ref[t * TOP_K + k]
            pltpu.make_async_copy(y_hbm.at[pl.ds(dd, 1), :], buf.at[k, pl.ds(t, 1), :], sem).start(priority=k % 2)
        return c
    lax.fori_loop(0, tbc, body, 0, unroll=4)
    for k in range(TOP_K):
        pltpu.make_async_copy(y_hbm.at[pl.ds(0, tbc), :], buf.at[k], sem).wait()
    gate = gate_ref[...]
    moe_a, moe_b = None, None
    for k in range(TOP_K):
        ya, yb = _unpack_halves(buf[k])
        gk = gate[:, k:k + 1]
        moe_a = gk * ya if k == 0 else moe_a + gk * ya
        moe_b = gk * yb if k == 0 else moe_b + gk * yb
    moe = jnp.concatenate([moe_a, moe_b], axis=-1)
    g2 = mod_ref[0, :, 5 * d:6 * d]
    hx = hx_ref[...] + g2 * moe
    o_ref[...] = _rms(hx) * fg_ref[...]


def _combine_call(dest_flat, gate, hx1, mod3, rows_per_batch, final_g, y, tbc):
    t, d = hx1.shape
    bpb = rows_per_batch // tbc
    kern = functools.partial(_combine_kernel, tbc=tbc, d=d)
    return pl.pallas_call(
        kern,
        out_shape=jax.ShapeDtypeStruct((t, d), F32),
        grid=(t // tbc,),
        in_specs=[pl.BlockSpec((tbc * TOP_K,), lambda i: (i,), memory_space=pltpu.SMEM),
                  pl.BlockSpec((tbc, TOP_K), lambda i: (i, 0)),
                  pl.BlockSpec((tbc, d), lambda i: (i, 0)),
                  pl.BlockSpec((1, 1, N_MOD * d), lambda i: (i // bpb, 0, 0)),
                  pl.BlockSpec((1, d), lambda i: (0, 0)),
                  pl.BlockSpec(memory_space=pl.ANY)],
        out_specs=pl.BlockSpec((tbc, d), lambda i: (i, 0)),
        scratch_shapes=[pltpu.VMEM((TOP_K, tbc, d // 2), U32), pltpu.SemaphoreType.DMA(())],
        compiler_params=_cparams("arbitrary"),
        name="combine",
    )(dest_flat, gate, hx1, mod3, final_g.reshape(1, d), y)


def _tiles(t_tokens, seq):
    tok = 512 if seq % 512 == 0 else seq
    moe_tm = 1024 if t_tokens * TOP_K >= 32 * 1024 else 256
    return dict(tok=tok, route=tok, moe_tm=moe_tm,
                disp=tok, comb=tok)


def kernel(x, c, ctx, c_ctx, w_mod, b_mod, norm1_g, norm2_g, w_in, lru_conv_w, lru_conv_b, lru_wa, lru_ba,
           lru_wi, lru_bi, lru_lambda, hy_conv_w, hy_conv_b, hy_w1, hy_b1, hy_w2, hy_b2, hy_w3, hy_b3,
           hy_freq, hy_w4, hy_d, gn_lru, gn_hy, w_out, router_w, router_b, exp_w_gu, exp_b_gu,
           exp_w_down, exp_b_down, final_g):
    depth = w_mod.shape[0]
    assert depth == 1, "single-layer stack: the context stream only feeds the latent scan states"
    n_batch, seq, d = x.shape
    ctx_len = ctx.shape[1]
    w_lru = lru_conv_w.shape[-1]
    w_hy = hy_d.shape[-1]
    nb_l, nb_h = w_lru // LANES, w_hy // LANES
    assert lru_wa.shape[2] == nb_l and lru_wa.shape[3] == LANES, "gate blocks must be 128 wide"
    assert seq % (GRID_W * SUBLANES) == 0
    ne = router_w.shape[-1]
    t_tokens = n_batch * seq
    tl = _tiles(t_tokens, seq)
    l = 0

    n_rows = -(-(n_batch + 1) // SUBLANES) * SUBLANES
    cstack = jnp.concatenate([c, c_ctx[None, :], jnp.zeros((n_rows - n_batch - 1, d), F32)], axis=0)
    mod = _mod_call(cstack, w_mod[l], b_mod[l])
    mod3 = mod.reshape(n_rows, 1, N_MOD * d)

    w_in_b = w_in[l].astype(BF16)
    x2d = x.reshape(t_tokens, d)
    bpb = seq // tl["tok"]
    p_x = _inproj_call(x2d, mod3, lambda i: i // bpb, norm1_g[l], w_in_b, tl["tok"])
    ctx_tm = ctx_len if ctx_len <= 512 else 256
    p_c = _inproj_call(ctx.reshape(n_batch * ctx_len, d), mod3, lambda i: n_batch, norm1_g[l],
                       w_in_b[:, :2 * w_lru], ctx_tm)

    lru_args = (lru_conv_w[l], lru_conv_b[l], lru_wa[l], lru_ba[l], lru_wi[l], lru_bi[l], lru_lambda[l])
    _, h_ctx = _lru_call(p_c, nb_l, n_batch, ctx_len, *lru_args, jnp.zeros((n_batch, 2, w_lru), F32))
    y_lru, _ = _lru_call(p_x, nb_l, n_batch, seq, *lru_args, h_ctx)

    rows = seq // GRID_W
    tables = _dft_tables(rows)
    hfilt = _filt_call(seq, w_hy, hy_w1[l], hy_b1[l], hy_w2[l], hy_b2[l], hy_w3[l], hy_b3[l], hy_freq[l], hy_w4[l])
    kf = _spec_call(hfilt, seq, nb_h, tables[0], tables[1])
    cw, cb = hy_conv_w[l], hy_conv_b[l]
    s0 = 2 * nb_l
    z = _hyena_call(p_x, s0, p_x, s0 + nb_h, cw[:, :w_hy], cb[:w_hy], cw[:, w_hy:2 * w_hy], cb[w_hy:2 * w_hy],
                    hy_d[l, 0], kf, 0, tables, n_batch, seq, nb_h, True)
    hy = _hyena_call(z, 0, p_x, s0 + 2 * nb_h, cw[:, :w_hy], cb[:w_hy], cw[:, 2 * w_hy:], cb[2 * w_hy:],
                     hy_d[l, 1], kf, 1, tables, n_batch, seq, nb_h, False)

    hx1, m, logits = _outproj_call(y_lru, hy, x2d, mod3, seq, gn_lru[l], gn_hy[l], norm2_g[l],
                                   w_out[l].astype(BF16), router_w[l], router_b[l], tl["tok"])

    idx, gate, rank, counts = _route_call(logits, tl["route"])
    tm = tl["moe_tm"]
    counts = counts[0]
    pcnt = (counts + tm - 1) // tm * tm
    pend = jnp.cumsum(pcnt)
    pstart = pend - pcnt
    dest = (pstart[idx] + rank).reshape(-1).astype(I32)
    nblk = t_tokens * TOP_K // tm + ne
    cap = nblk * tm
    n_used = (pend[-1] // tm).astype(I32).reshape(1)
    blk_ids = jnp.arange(nblk, dtype=I32)
    block_expert = jnp.minimum(jnp.sum((pend[None, :] <= (blk_ids * tm)[:, None]).astype(I32), axis=1), ne - 1)
    block_expert = jnp.where(blk_ids < n_used[0], block_expert, block_expert[n_used[0] - 1])

    xs = _dispatch_call(dest, pend.astype(I32), pcnt.astype(I32), n_used, m, cap, tm, tl["disp"])
    d_ff = exp_w_down.shape[2]
    tf = next((c for c in (512, 256) if d_ff % c == 0), d_ff)
    y = _moe_call(block_expert, n_used, xs, exp_w_gu[l], exp_b_gu[l], exp_w_down[l], exp_b_down[l], tm, tf)
    out = _combine_call(dest, gate, hx1, mod3, seq, final_g, y, tl["comb"])
    return out.reshape(n_batch, seq, d)
```

```python
import functools
import math

import jax
import jax.numpy as jnp
from jax import lax
from jax.experimental import pallas as pl
from jax.experimental.pallas import tpu as pltpu

F32 = jnp.float32
BF16 = jnp.bfloat16
I32 = jnp.int32
U32 = jnp.uint32
HI = lax.Precision.HIGHEST

GRID_W = 64
EPS = 1e-6
N_MOD = 6
LRU_CONV = 4
LRU_C = 8.0
HY_CONV = 3
HY_ORDER = 2
HY_BANDS = 16
HY_TARGET = 1e-2
HY_FAST = 0.3
HY_SLOW = 1.5
TOP_K = 4
SWIGLU_LIMIT = 7.0
SWIGLU_ALPHA = 1.702

LANES = 128
SUBLANES = 8
VMEM_LIMIT_BYTES = 56 * 1024 * 1024

DFT_N1 = 2 * GRID_W
K1_USED = DFT_N1 // 2 + 1
K1_PAD = 72
assert K1_PAD % SUBLANES == 0 and K1_PAD >= K1_USED
LOOP_UNROLL = 32
K1_UNROLL = 16
assert (K1_USED - 1) % K1_UNROLL == 0


def _cparams(*sem):
    return pltpu.CompilerParams(dimension_semantics=sem, vmem_limit_bytes=VMEM_LIMIT_BYTES)


def _rms(x):
    return x * lax.rsqrt(jnp.mean(x * x, axis=-1, keepdims=True) + EPS)


def _pack_pair(a, b):
    wa = pltpu.bitcast(a.astype(BF16).astype(F32), U32)
    wb = pltpu.bitcast(b.astype(BF16).astype(F32), U32)
    return wa | (wb >> 16)


def _pack_halves(v):
    h = v.shape[-1] // 2
    return _pack_pair(v[:, :h], v[:, h:])


def _unpack_halves(w):
    return pltpu.bitcast(w & jnp.uint32(0xFFFF0000), F32), pltpu.bitcast(w << 16, F32)


def _mod_kernel(c_ref, w_ref, b_ref, o_ref):
    c = c_ref[...]
    s = c * jax.nn.sigmoid(c)
    o_ref[...] = jnp.dot(s, w_ref[...], preferred_element_type=F32, precision=HI) + b_ref[...]


def _mod_call(cstack, w_mod, b_mod):
    rows, d = cstack.shape
    n = w_mod.shape[1]
    tn = 1536 if n % 1536 == 0 else n
    return pl.pallas_call(
        _mod_kernel,
        out_shape=jax.ShapeDtypeStruct((rows, n), F32),
        grid=(n // tn,),
        in_specs=[pl.BlockSpec((rows, d), lambda j: (0, 0)),
                  pl.BlockSpec((d, tn), lambda j: (0, j)),
                  pl.BlockSpec((1, tn), lambda j: (0, j))],
        out_specs=pl.BlockSpec((rows, tn), lambda j: (0, j)),
        compiler_params=_cparams("arbitrary"),
        name="mod",
    )(cstack, w_mod, b_mod.reshape(1, n))


def _inproj_kernel(x_ref, mod_ref, g_ref, w_ref, o_ref, *, d, n_slabs, nchunk):
    x = x_ref[...]
    sh = mod_ref[0, :, 0 * d:1 * d]
    sc = mod_ref[0, :, 1 * d:2 * d]
    a = (_rms(x) * g_ref[...]) * (1.0 + sc) + sh
    ab = a.astype(BF16)
    spc = nchunk // LANES
    for j in range(n_slabs // spc):
        p = jnp.dot(ab, w_ref[:, j * nchunk:(j + 1) * nchunk], preferred_element_type=F32)
        for s in range(spc):
            o_ref[j * spc + s] = p[:, s * LANES:(s + 1) * LANES].astype(BF16)


def _inproj_call(x2d, mod3, mod_row_of_block, g, w_bf16, tm):
    t, d = x2d.shape
    n = w_bf16.shape[1]
    n_slabs = n // LANES
    nchunk = 512 if n % 512 == 0 else LANES
    kern = functools.partial(_inproj_kernel, d=d, n_slabs=n_slabs, nchunk=nchunk)
    return pl.pallas_call(
        kern,
        out_shape=jax.ShapeDtypeStruct((n_slabs, t, LANES), BF16),
        grid=(t // tm,),
        in_specs=[pl.BlockSpec((tm, d), lambda i: (i, 0)),
                  pl.BlockSpec((1, 1, N_MOD * d), lambda i: (mod_row_of_block(i), 0, 0)),
                  pl.BlockSpec((1, d), lambda i: (0, 0)),
                  pl.BlockSpec((d, n), lambda i: (0, 0), pipeline_mode=pl.Buffered(1))],
        out_specs=pl.BlockSpec((n_slabs, tm, LANES), lambda i: (0, i, 0)),
        compiler_params=_cparams("arbitrary"),
        name="inproj",
    )(x2d, mod3, g.reshape(1, d), w_bf16)


def _gelu_tanh(x):
    return 0.5 * x * (1.0 + jnp.tanh(math.sqrt(2.0 / math.pi) * (x + 0.044715 * (x * x * x))))


def _scan_tile(a, b, row, reverse):
    for s in (1, 2, 4):
        if reverse:
            m = row < SUBLANES - s
            sh = SUBLANES - s
        else:
            m = row >= s
            sh = s
        a_s = jnp.where(m, pltpu.roll(a, sh, 0), 1.0)
        b_s = jnp.where(m, pltpu.roll(b, sh, 0), 0.0)
        b = a * b_s + b
        a = a * a_s
    return a, b


def _lru_kernel(r_ref, g_ref, cw_ref, cb_ref, wa_ref, wi_ref, ba_ref, bi_ref, lam_ref, h0_ref,
                y_ref, hl_ref, xp_ref, hf_ref, ab_ref, bb_ref, *, t_len, tc):
    nc = t_len // tc
    ntile = tc // SUBLANES
    zero8 = jnp.zeros((SUBLANES, LANES), F32)
    xp_ref[0:SUBLANES, :] = zero8
    xp_ref[t_len + SUBLANES:t_len + 2 * SUBLANES, :] = zero8

    def copy_body(ci, c):
        t0 = pl.multiple_of(ci * tc, tc)
        xp_ref[pl.ds(t0 + SUBLANES, tc), :] = r_ref[0, pl.ds(t0, tc), :].astype(F32)
        return c
    lax.fori_loop(0, nc, copy_body, 0)

    row = lax.broadcasted_iota(I32, (SUBLANES, LANES), 0)
    cw = cw_ref[...]
    cb = cb_ref[...]
    lam = lam_ref[...]
    sp = jnp.log1p(jnp.exp(-lam))
    wab = [wa_ref[d, 0].astype(BF16) for d in range(2)]
    wib = [wi_ref[d, 0].astype(BF16) for d in range(2)]

    def coeffs(u, ub, d):
        ga = jnp.dot(ub, wab[d], preferred_element_type=F32) + ba_ref[d:d + 1, :]
        gi = jnp.dot(ub, wib[d], preferred_element_type=F32) + bi_ref[d:d + 1, :]
        rg = jax.nn.sigmoid(ga)
        ig = jax.nn.sigmoid(gi)
        a = jnp.exp((-LRU_C) * rg * sp[d:d + 1, :])
        b = jnp.sqrt(1.0 - a * a) * (ig * u)
        return a, b

    def fwd_body(ci, hcar):
        t0 = pl.multiple_of(ci * tc, tc)
        blk = xp_ref[pl.ds(t0, tc + 2 * SUBLANES), :]
        u = (cw[0:1, :] * blk[6:6 + tc] + cw[1:2, :] * blk[7:7 + tc]
             + cw[2:3, :] * blk[8:8 + tc] + cw[3:4, :] * blk[9:9 + tc]) + cb
        ub = u.astype(BF16)
        a0, b0 = coeffs(u, ub, 0)
        a1, b1 = coeffs(u, ub, 1)
        ab_ref[pl.ds(t0, tc), :] = a1
        bb_ref[pl.ds(t0, tc), :] = b1
        for j in range(ntile):
            at, bt = _scan_tile(a0[j * 8:(j + 1) * 8], b0[j * 8:(j + 1) * 8], row, False)
            hf_ref[pl.ds(t0 + j * 8, 8), :] = at * hcar + bt
            atot = jnp.broadcast_to(at[7:8, :], (SUBLANES, LANES))
            btot = jnp.broadcast_to(bt[7:8, :], (SUBLANES, LANES))
            hcar = atot * hcar + btot
        return hcar

    h0f = jnp.broadcast_to(h0_ref[0, 0:1, :], (SUBLANES, LANES))
    hfin = lax.fori_loop(0, nc, fwd_body, h0f)
    hl_ref[0, 0:1, :] = hfin[0:1, :]

    def bwd_body(k, hcar):
        ci = nc - 1 - k
        t0 = pl.multiple_of(ci * tc, tc)
        a1 = ab_ref[pl.ds(t0, tc), :]
        b1 = bb_ref[pl.ds(t0, tc), :]
        gg = g_ref[0, pl.ds(t0, tc), :].astype(F32)
        gl = _gelu_tanh(gg)
        for j in reversed(range(ntile)):
            at, bt = _scan_tile(a1[j * 8:(j + 1) * 8], b1[j * 8:(j + 1) * 8], row, True)
            hb = at * hcar + bt
            hf = hf_ref[pl.ds(t0 + j * 8, 8), :]
            y_ref[0, pl.ds(t0 + j * 8, 8), :] = ((hf + hb) * gl[j * 8:(j + 1) * 8]).astype(y_ref.dtype)
            atot = jnp.broadcast_to(at[0:1, :], (SUBLANES, LANES))
            btot = jnp.broadcast_to(bt[0:1, :], (SUBLANES, LANES))
            hcar = atot * hcar + btot
        return hcar

    h0b = jnp.broadcast_to(h0_ref[0, 1:2, :], (SUBLANES, LANES))
    hfin_b = lax.fori_loop(0, nc, bwd_body, h0b)
    hl_ref[0, 1:2, :] = hfin_b[0:1, :]


def _lru_call(p_slabs, nb, n_batch, t_len, conv_w, conv_b, wa, ba, wi, bi, lam, h0):
    tc = 128 if t_len % 128 == 0 else t_len
    kern = functools.partial(_lru_kernel, t_len=t_len, tc=tc)
    w_lru = nb * LANES
    return pl.pallas_call(
        kern,
        out_shape=[jax.ShapeDtypeStruct((nb, n_batch * t_len, LANES), F32),
                   jax.ShapeDtypeStruct((n_batch, 2, w_lru), F32)],
        grid=(nb, n_batch),
        in_specs=[pl.BlockSpec((1, t_len, LANES), lambda n, b: (n, b, 0)),
                  pl.BlockSpec((1, t_len, LANES), lambda n, b: (nb + n, b, 0)),
                  pl.BlockSpec((LRU_CONV, LANES), lambda n, b: (0, n)),
                  pl.BlockSpec((1, LANES), lambda n, b: (0, n)),
                  pl.BlockSpec((2, 1, LANES, LANES), lambda n, b: (0, n, 0, 0)),
                  pl.BlockSpec((2, 1, LANES, LANES), lambda n, b: (0, n, 0, 0)),
                  pl.BlockSpec((2, LANES), lambda n, b: (0, n)),
                  pl.BlockSpec((2, LANES), lambda n, b: (0, n)),
                  pl.BlockSpec((2, LANES), lambda n, b: (0, n)),
                  pl.BlockSpec((1, 2, LANES), lambda n, b: (b, 0, n))],
        out_specs=[pl.BlockSpec((1, t_len, LANES), lambda n, b: (n, b, 0)),
                   pl.BlockSpec((1, 2, LANES), lambda n, b: (b, 0, n))],
        scratch_shapes=[pltpu.VMEM((t_len + 2 * SUBLANES, LANES), F32),
                        pltpu.VMEM((t_len, LANES), F32),
                        pltpu.VMEM((t_len, LANES), F32),
                        pltpu.VMEM((t_len, LANES), F32)],
        compiler_params=_cparams("arbitrary", "arbitrary"),
        name="lru",
    )(p_slabs, p_slabs, conv_w, conv_b.reshape(1, w_lru), wa, wi, ba, bi, lam, h0)


def _filt_kernel(w1_ref, b1_ref, w2_ref, b2_ref, w3_ref, b3_ref, fr_ref, w4_ref, o_ref,
                 *, l_len, rows, tl, w_hy):
    i = pl.program_id(0)
    rho = i * tl + lax.broadcasted_iota(I32, (tl, 1), 0)
    tidx = ((rho % GRID_W) * rows + rho // GRID_W).astype(F32)
    tt = tidx * (1.0 / (l_len - 1))
    ww = (2.0 * math.pi) * tidx / l_len
    band = lax.broadcasted_iota(I32, (1, HY_BANDS), 1).astype(F32)
    f = 1e-4 + band * ((HY_BANDS - 1 - 1e-4) / (HY_BANDS - 1))
    fw = f * ww
    w1 = w1_ref[...]
    pre = (tt * w1[0:1, :]
           + jnp.dot(jnp.cos(fw), w1[1:1 + HY_BANDS, :], preferred_element_type=F32, precision=HI)
           + jnp.dot(-jnp.sin(fw), w1[1 + HY_BANDS:1 + 2 * HY_BANDS, :], preferred_element_type=F32, precision=HI))
    fr = fr_ref[...]
    h = jnp.sin(fr * (pre + b1_ref[...]))
    h = jnp.sin(fr * (jnp.dot(h, w2_ref[...], preferred_element_type=F32, precision=HI) + b2_ref[...]))
    h = jnp.sin(fr * (jnp.dot(h, w3_ref[...], preferred_element_type=F32, precision=HI) + b3_ref[...]))
    max_decay = math.log(HY_TARGET) / HY_FAST
    min_decay = math.log(HY_TARGET) / HY_SLOW
    nblk = w_hy // LANES
    h_hi = h.astype(BF16)
    h_lo = (h - h_hi.astype(F32)).astype(BF16)
    for cb in range(nblk):
        ch = (cb * LANES + lax.broadcasted_iota(I32, (1, LANES), 1)).astype(F32)
        delta = jnp.abs(min_decay + ch * ((max_decay - min_decay) / (w_hy - 1)))
        decay = jnp.exp(-tt * delta)
        for od in range(HY_ORDER * 2):
            col = od * w_hy + cb * LANES
            w4 = w4_ref[:, col:col + LANES]
            w_hi = w4.astype(BF16)
            w_lo = (w4 - w_hi.astype(F32)).astype(BF16)
            v = (jnp.dot(h_hi, w_hi, preferred_element_type=F32) + jnp.dot(h_lo, w_hi, preferred_element_type=F32)
                 + jnp.dot(h_hi, w_lo, preferred_element_type=F32))
            o_ref[od * nblk + cb] = v * decay


def _filt_call(l_len, w_hy, w1, b1, w2, b2, w3, b3, freq, w4):
    rows = l_len // GRID_W
    tl = 512 if l_len % 512 == 0 else l_len
    fh = w2.shape[0]
    nslab = HY_ORDER * 2 * (w_hy // LANES)
    kern = functools.partial(_filt_kernel, l_len=l_len, rows=rows, tl=tl, w_hy=w_hy)
    full = lambda a: pl.BlockSpec(a.shape, lambda i: (0,) * a.ndim)
    args = (w1, b1.reshape(1, fh), w2, b2.reshape(1, fh), w3, b3.reshape(1, fh), freq.reshape(1, fh), w4)
    return pl.pallas_call(
        kern,
        out_shape=jax.ShapeDtypeStruct((nslab, l_len, LANES), F32),
        grid=(l_len // tl,),
        in_specs=[full(a) for a in args],
        out_specs=pl.BlockSpec((nslab, tl, LANES), lambda i: (0, i, 0)),
        compiler_params=_cparams("arbitrary"),
        name="filt",
    )(*args)


def _dft_tables(rows):
    n_len = DFT_N1 * rows
    k1 = jnp.arange(K1_PAD, dtype=I32)
    r = jnp.arange(rows, dtype=I32)
    w = jnp.arange(GRID_W, dtype=I32)
    idx = (k1[None, :, None] * (r[:, None, None] + rows * w[None, None, :])) % n_len
    ang = idx.astype(F32) * (2.0 * math.pi / n_len)
    valid = (k1 < K1_USED)[None, :, None]
    ca = jnp.where(valid, jnp.cos(ang), 0.0)
    sa = jnp.where(valid, jnp.sin(ang), 0.0)
    ta = jnp.stack([ca, -sa], axis=2).reshape(rows, 2 * K1_PAD, GRID_W)
    td = jnp.transpose(ta, (0, 2, 1))
    k2 = jnp.arange(rows, dtype=I32)
    th = ((k2[:, None] * r[None, :]) % rows).astype(F32) * (2.0 * math.pi / rows)
    c, s = jnp.cos(th), jnp.sin(th)
    tb = jnp.concatenate([jnp.stack([c, s], axis=2).reshape(rows, 2 * rows),
                          jnp.stack([-s, c], axis=2).reshape(rows, 2 * rows)], axis=0)
    tcm = jnp.stack([jnp.concatenate([c, -s], axis=1), jnp.concatenate([s, c], axis=1)], axis=1).reshape(2 * rows, 2 * rows)
    return ta.astype(BF16), tb.astype(BF16), tcm.astype(BF16), td.astype(BF16)


def _pitch_r(rows):
    return rows + SUBLANES


def _stage_a(tile_fn, ta_ref, s1_ref, rows):
    pitch = _pitch_r(rows)

    def body(r, c):
        out = jnp.dot(ta_ref[r], tile_fn(r), preferred_element_type=F32)
        words = pltpu.bitcast(out.astype(BF16), U32)
        s1_ref[pl.ds(r, K1_PAD, stride=pitch), :] = words
        return c
    lax.fori_loop(0, rows, body, 0, unroll=min(LOOP_UNROLL, rows))


def _loop_k1(body):
    lax.fori_loop(0, K1_USED - 1, body, 0, unroll=K1_UNROLL)
    body(K1_USED - 1, 0)


def _stage_b(k1, tb_ref, s1_ref, rows):
    pitch = _pitch_r(rows)
    t = s1_ref[pl.ds(pl.multiple_of(k1 * pitch, SUBLANES), rows), :]
    y = jnp.dot(tb_ref[...], pltpu.bitcast(t, BF16), preferred_element_type=F32)
    return y[:rows], y[rows:]


def _spec_kernel(hf_ref, hb_ref, ta_ref, tb_ref, kf_ref, s1_ref, *, rows):
    n_len = DFT_N1 * rows

    def scale(k1):
        return jnp.where((k1 == 0) | (k1 == K1_USED - 1), 1.0, 2.0) * (1.0 / n_len)

    _stage_a(lambda r: hf_ref[0, pl.ds(pl.multiple_of(r * GRID_W, GRID_W), GRID_W), :].astype(BF16),
             ta_ref, s1_ref, rows)

    def b_fwd(k1, c):
        yre, yim = _stage_b(k1, tb_ref, s1_ref, rows)
        sc = scale(k1)
        kf_ref[0, 0, k1, 0] = yre * sc
        kf_ref[0, 0, k1, 1] = yim * sc
        return c
    _loop_k1(b_fwd)

    rid = lax.broadcasted_iota(I32, (GRID_W, LANES), 0)

    def tile_b(r):
        x = hb_ref[0, pl.ds(pl.multiple_of(r * GRID_W, GRID_W), GRID_W), :]
        x = jnp.where((rid == 0) & (r == 0), 0.0, x)
        return x.astype(BF16)
    _stage_a(tile_b, ta_ref, s1_ref, rows)

    def b_bwd(k1, c):
        yre, yim = _stage_b(k1, tb_ref, s1_ref, rows)
        sc = scale(k1)
        kf_ref[0, 0, k1, 0] += yre * sc
        kf_ref[0, 0, k1, 1] -= yim * sc
        return c
    _loop_k1(b_bwd)


def _spec_call(hfilt, l_len, nblk, ta, tb):
    rows = l_len // GRID_W
    kern = functools.partial(_spec_kernel, rows=rows)
    return pl.pallas_call(
        kern,
        out_shape=jax.ShapeDtypeStruct((HY_ORDER, nblk, K1_USED, 2, rows, LANES), F32),
        grid=(HY_ORDER, nblk),
        in_specs=[pl.BlockSpec((1, l_len, LANES), lambda o, c: (o * 2 * nblk + c, 0, 0)),
                  pl.BlockSpec((1, l_len, LANES), lambda o, c: (o * 2 * nblk + nblk + c, 0, 0)),
                  pl.BlockSpec(ta.shape, lambda o, c: (0, 0, 0), pipeline_mode=pl.Buffered(1)),
                  pl.BlockSpec(tb.shape, lambda o, c: (0, 0), pipeline_mode=pl.Buffered(1))],
        out_specs=pl.BlockSpec((1, 1, K1_USED, 2, rows, LANES), lambda o, c: (o, c, 0, 0, 0, 0)),
        scratch_shapes=[pltpu.VMEM((K1_PAD * _pitch_r(rows), LANES), U32)],
        compiler_params=_cparams("arbitrary", "arbitrary"),
        name="spec",
    )(hfilt, hfilt, ta, tb)


def _conv_grid_order(src_ref, dst_ref, w_ref, b_ref, rows):
    w0, w1, w2, bb = w_ref[0:1, :], w_ref[1:2, :], w_ref[2:3, :], b_ref[...]
    g = GRID_W
    rid = lax.broadcasted_iota(I32, (g, LANES), 0)

    def grp(r0):
        return src_ref[0, pl.ds(r0, g), :].astype(F32)

    last = grp((rows - 1) * g)
    prev0 = jnp.where(rid >= 1, pltpu.roll(last, 1, 0), 0.0)
    dst_ref[pl.ds(0, g), :] = (w0 * prev0 + w1 * grp(0) + w2 * grp(g) + bb).astype(dst_ref.dtype)
    first = grp(0)
    nxt = jnp.where(rid < g - 1, pltpu.roll(first, g - 1, 0), 0.0)
    dst_ref[pl.ds((rows - 1) * g, g), :] = (w0 * grp((rows - 2) * g) + w1 * last + w2 * nxt + bb).astype(dst_ref.dtype)

    def body(r, c):
        rm = pl.multiple_of((r - 1) * g, g)
        r0 = pl.multiple_of(r * g, g)
        rp = pl.multiple_of((r + 1) * g, g)
        v = w0 * grp(rm) + w1 * grp(r0) + w2 * grp(rp) + bb
        dst_ref[pl.ds(r0, g), :] = v.astype(dst_ref.dtype)
        return c
    lax.fori_loop(1, rows - 1, body, 0, unroll=2)


def _hyena_kernel(sig_ref, gate_ref, cws_ref, cbs_ref, cwg_ref, cbg_ref, d_ref, kf_ref,
                  ta_ref, tb_ref, tc_ref, td_ref, o_ref, u_ref, gt_ref, s1_ref, s2_ref,
                  *, rows, conv_sig):
    g = GRID_W

    @pl.when((pl.program_id(0) == 0) & (pl.program_id(1) == 0))
    def _():
        s2_ref[...] = jnp.zeros(s2_ref.shape, U32)

    if conv_sig:
        _conv_grid_order(sig_ref, u_ref, cws_ref, cbs_ref, rows)
    else:
        def cp(r, c):
            r0 = pl.multiple_of(r * g, g)
            u_ref[pl.ds(r0, g), :] = sig_ref[0, pl.ds(r0, g), :]
            return c
        lax.fori_loop(0, rows, cp, 0, unroll=min(LOOP_UNROLL, rows))
    _conv_grid_order(gate_ref, gt_ref, cwg_ref, cbg_ref, rows)

    _stage_a(lambda r: u_ref[pl.ds(pl.multiple_of(r * g, g), g), :], ta_ref, s1_ref, rows)

    def bc_body(k1, c):
        yre, yim = _stage_b(k1, tb_ref, s1_ref, rows)
        kre = kf_ref[0, 0, k1, 0]
        kim = kf_ref[0, 0, k1, 1]
        zre = yre * kre - yim * kim
        zim = yre * kim + yim * kre
        z = jnp.concatenate([zre, zim], axis=0).astype(BF16)
        v = jnp.dot(tc_ref[...], z, preferred_element_type=F32)
        s2_ref[pl.ds(k1, rows, stride=K1_PAD), :] = pltpu.bitcast(v.astype(BF16), U32)
        return c
    _loop_k1(bc_body)

    dd = d_ref[...]

    def d_body(r, c):
        r0 = pl.multiple_of(r * g, g)
        t = s2_ref[pl.ds(pl.multiple_of(r * K1_PAD, SUBLANES), K1_PAD), :]
        y = jnp.dot(td_ref[r], pltpu.bitcast(t, BF16), preferred_element_type=F32)
        uu = u_ref[pl.ds(r0, g), :].astype(F32)
        o_ref[0, pl.ds(r0, g), :] = (gt_ref[pl.ds(r0, g), :] * (y + dd * uu)).astype(o_ref.dtype)
        return c
    lax.fori_loop(0, rows, d_body, 0, unroll=min(LOOP_UNROLL, rows))


def _hyena_call(sig_arr, sig_slab0, gate_arr, gate_slab0, conv_w_sig, conv_b_sig, conv_w_gate, conv_b_gate,
                d_vec, kf, order, tables, n_batch, l_len, nblk, conv_sig):
    rows = l_len // GRID_W
    ta, tb, tcm, td = tables
    kern = functools.partial(_hyena_kernel, rows=rows, conv_sig=conv_sig)
    w_hy = nblk * LANES
    const = lambda a: pl.BlockSpec(a.shape, lambda c, b: (0,) * a.ndim, pipeline_mode=pl.Buffered(1))
    return pl.pallas_call(
        kern,
        out_shape=jax.ShapeDtypeStruct((nblk, n_batch * l_len, LANES), BF16),
        grid=(nblk, n_batch),
        in_specs=[pl.BlockSpec((1, l_len, LANES), lambda c, b: (sig_slab0 + c, b, 0)),
                  pl.BlockSpec((1, l_len, LANES), lambda c, b: (gate_slab0 + c, b, 0)),
                  pl.BlockSpec((HY_CONV, LANES), lambda c, b: (0, c)),
                  pl.BlockSpec((1, LANES), lambda c, b: (0, c)),
                  pl.BlockSpec((HY_CONV, LANES), lambda c, b: (0, c)),
                  pl.BlockSpec((1, LANES), lambda c, b: (0, c)),
                  pl.BlockSpec((1, LANES), lambda c, b: (0, c)),
                  pl.BlockSpec((1, 1, K1_USED, 2, rows, LANES), lambda c, b: (order, c, 0, 0, 0, 0),
                               pipeline_mode=pl.Buffered(1)),
                  const(ta), const(tb), const(tcm), const(td)],
        out_specs=pl.BlockSpec((1, l_len, LANES), lambda c, b: (c, b, 0)),
        scratch_shapes=[pltpu.VMEM((l_len, LANES), BF16),
                        pltpu.VMEM((l_len, LANES), F32),
                        pltpu.VMEM((K1_PAD * _pitch_r(rows), LANES), U32),
                        pltpu.VMEM((rows * K1_PAD, LANES), U32)],
        compiler_params=_cparams("arbitrary", "arbitrary"),
        name="hyena%d" % order,
    )(sig_arr, gate_arr, conv_w_sig, conv_b_sig.reshape(1, w_hy), conv_w_gate, conv_b_gate.reshape(1, w_hy),
      d_vec.reshape(1, w_hy), kf, ta, tb, tcm, td)


def _outproj_kernel(yl_ref, hy_ref, x_ref, mod_ref, gl_ref, gh_ref, n2_ref, w_ref, rw2_ref, rb_ref,
                    hx_ref, m_ref, lg_ref, *, d, nb_l, nb_h, ne):
    yl = jnp.concatenate([yl_ref[j].astype(F32) for j in range(nb_l)], axis=-1)
    hy = jnp.concatenate([hy_ref[j].astype(F32) for j in range(nb_h)], axis=-1)
    cat = jnp.concatenate([_rms(yl) * gl_ref[...], _rms(hy) * gh_ref[...]], axis=-1).astype(BF16)
    out = jnp.dot(cat, w_ref[...], preferred_element_type=F32)
    g1 = mod_ref[0, :, 2 * d:3 * d]
    sh2 = mod_ref[0, :, 3 * d:4 * d]
    sc2 = mod_ref[0, :, 4 * d:5 * d]
    hx = x_ref[...] + g1 * out
    hx_ref[...] = hx
    m = (_rms(hx) * n2_ref[...]) * (1.0 + sc2) + sh2
    m_ref[...] = _pack_halves(m)
    m_hi = m.astype(BF16)
    m_lo = (m - m_hi.astype(F32)).astype(BF16)
    a = jnp.dot(m_hi, rw2_ref[...], preferred_element_type=F32)
    b = jnp.dot(m_lo, rw2_ref[:, :ne], preferred_element_type=F32)
    lg_ref[...] = (a[:, :ne] + b) + a[:, ne:] + rb_ref[...]


def _outproj_call(y_lru, hy, x2d, mod3, rows_per_batch, gn_lru, gn_hy, norm2_g, w_out_bf16, router_w, router_b, tm):
    t, d = x2d.shape
    nb_l, nb_h = y_lru.shape[0], hy.shape[0]
    ne = router_w.shape[1]
    bpb = rows_per_batch // tm
    rw_hi = router_w.astype(BF16)
    rw_lo = (router_w - rw_hi.astype(F32)).astype(BF16)
    rw2 = jnp.concatenate([rw_hi, rw_lo], axis=1)
    kern = functools.partial(_outproj_kernel, d=d, nb_l=nb_l, nb_h=nb_h, ne=ne)
    return pl.pallas_call(
        kern,
        out_shape=[jax.ShapeDtypeStruct((t, d), F32), jax.ShapeDtypeStruct((t, d // 2), U32),
                   jax.ShapeDtypeStruct((t, ne), F32)],
        grid=(t // tm,),
        in_specs=[pl.BlockSpec((nb_l, tm, LANES), lambda i: (0, i, 0)),
                  pl.BlockSpec((nb_h, tm, LANES), lambda i: (0, i, 0)),
                  pl.BlockSpec((tm, d), lambda i: (i, 0)),
                  pl.BlockSpec((1, 1, N_MOD * d), lambda i: (i // bpb, 0, 0)),
                  pl.BlockSpec((1, nb_l * LANES), lambda i: (0, 0)),
                  pl.BlockSpec((1, nb_h * LANES), lambda i: (0, 0)),
                  pl.BlockSpec((1, d), lambda i: (0, 0)),
                  pl.BlockSpec(w_out_bf16.shape, lambda i: (0, 0), pipeline_mode=pl.Buffered(1)),
                  pl.BlockSpec((d, 2 * ne), lambda i: (0, 0)),
                  pl.BlockSpec((1, ne), lambda i: (0, 0))],
        out_specs=[pl.BlockSpec((tm, d), lambda i: (i, 0)),
                   pl.BlockSpec((tm, d // 2), lambda i: (i, 0)),
                   pl.BlockSpec((tm, ne), lambda i: (i, 0))],
        compiler_params=_cparams("arbitrary"),
        name="outproj",
    )(y_lru, hy, x2d, mod3, gn_lru.reshape(1, -1), gn_hy.reshape(1, -1), norm2_g.reshape(1, d),
      w_out_bf16, rw2, router_b.reshape(1, ne))


def _route_kernel(lg_ref, idx_ref, gate_ref, rank_ref, cnt_ref, run_ref, *, tb, ne):
    @pl.when(pl.program_id(0) == 0)
    def _():
        run_ref[...] = jnp.zeros(run_ref.shape, F32)

    l = lg_ref[...]
    lane = lax.broadcasted_iota(I32, (tb, ne), 1)
    vals, idxs, ohs = [], [], []
    for _ in range(TOP_K):
        m = jnp.max(l, axis=-1, keepdims=True)
        ix = jnp.min(jnp.where(l == m, lane, ne), axis=-1, keepdims=True)
        sel = lane == ix
        vals.append(m)
        idxs.append(ix)
        ohs.append(sel.astype(F32))
        l = jnp.where(sel, -jnp.inf, l)
    es = [jnp.exp(v - vals[0]) for v in vals]
    den = es[0] + es[1] + es[2] + es[3]
    oh_all = ohs[0] + ohs[1] + ohs[2] + ohs[3]
    ri = lax.broadcasted_iota(I32, (tb, tb), 0)
    ci = lax.broadcasted_iota(I32, (tb, tb), 1)
    ltri = (ci < ri).astype(BF16)
    before = jnp.dot(ltri, oh_all.astype(BF16), preferred_element_type=F32) + run_ref[...]
    for k in range(TOP_K):
        idx_ref[:, k:k + 1] = idxs[k]
        gate_ref[:, k:k + 1] = es[k] / den
        rank_ref[:, k:k + 1] = jnp.sum(ohs[k] * before, axis=-1, keepdims=True).astype(I32)
    run_ref[...] += jnp.sum(oh_all, axis=0, keepdims=True)
    cnt_ref[...] = run_ref[...].astype(I32)


def _route_call(logits, tb):
    t, ne = logits.shape
    kern = functools.partial(_route_kernel, tb=tb, ne=ne)
    blk = lambda: pl.BlockSpec((tb, TOP_K), lambda i: (i, 0))
    return pl.pallas_call(
        kern,
        out_shape=[jax.ShapeDtypeStruct((t, TOP_K), I32), jax.ShapeDtypeStruct((t, TOP_K), F32),
                   jax.ShapeDtypeStruct((t, TOP_K), I32), jax.ShapeDtypeStruct((1, ne), I32)],
        grid=(t // tb,),
        in_specs=[pl.BlockSpec((tb, ne), lambda i: (i, 0))],
        out_specs=[blk(), blk(), blk(), pl.BlockSpec((1, ne), lambda i: (0, 0))],
        scratch_shapes=[pltpu.VMEM((1, ne), F32)],
        compiler_params=_cparams("arbitrary"),
        name="route",
    )(logits)


def _dispatch_kernel(dest_ref, pend_ref, pcnt_ref, nu_ref, m_ref, xs_hbm, zbuf, sem, zsem, *, tbd, tm, ne, nblk):
    i = pl.program_id(0)

    @pl.when(i == 0)
    def _():
        zbuf[...] = jnp.zeros(zbuf.shape, U32)

        def zero_block(row0):
            cp = pltpu.make_async_copy(zbuf, xs_hbm.at[pl.ds(pl.multiple_of(row0, tm), tm), :], zsem)
            cp.start()
            cp.wait()

        def zfill(e, c):
            @pl.when(pcnt_ref[e] > 0)
            def _():
                zero_block(pend_ref[e] - tm)
            return c
        lax.fori_loop(0, ne, zfill, 0)

        def ztail(bi, c):
            zero_block(bi * tm)
            return c
        lax.fori_loop(nu_ref[0], nblk, ztail, 0)

    def body(tt, c):
        tile = m_ref.at[pl.ds(pl.multiple_of(tt * SUBLANES, SUBLANES), SUBLANES), :]
        for s in range(SUBLANES):
            for k in range(TOP_K):
                d = dest_ref[(tt * SUBLANES + s) * TOP_K + k]
                pltpu.make_async_copy(tile.at[pl.ds(s, 1), :], xs_hbm.at[pl.ds(d, 1), :], sem).start(priority=k % 2)
        return c
    lax.fori_loop(0, tbd // SUBLANES, body, 0)
    for k in range(TOP_K):
        pltpu.make_async_copy(m_ref, xs_hbm.at[pl.ds(0, tbd), :], sem).wait()


def _dispatch_call(dest_flat, pend, pcnt, n_used, m, cap, tm, tbd):
    t, d = m.shape
    ne = pend.shape[0]
    kern = functools.partial(_dispatch_kernel, tbd=tbd, tm=tm, ne=ne, nblk=cap // tm)
    return pl.pallas_call(
        kern,
        out_shape=jax.ShapeDtypeStruct((cap, d), U32),
        grid=(t // tbd,),
        in_specs=[pl.BlockSpec((tbd * TOP_K,), lambda i: (i,), memory_space=pltpu.SMEM),
                  pl.BlockSpec(memory_space=pltpu.SMEM),
                  pl.BlockSpec(memory_space=pltpu.SMEM),
                  pl.BlockSpec(memory_space=pltpu.SMEM),
                  pl.BlockSpec((tbd, d), lambda i: (i, 0))],
        out_specs=pl.BlockSpec(memory_space=pl.ANY),
        scratch_shapes=[pltpu.VMEM((tm, d), U32), pltpu.SemaphoreType.DMA(()), pltpu.SemaphoreType.DMA(())],
        compiler_params=_cparams("arbitrary"),
        name="dispatch",
    )(dest_flat, pend, pcnt, n_used, m)


def _moe_up_kernel(be_ref, nu_ref, x_ref, wg_ref, wl_ref, bg_ref, bl_ref, a_ref, xb_ref):
    i = pl.program_id(0)
    j = pl.program_id(1)
    half = x_ref.shape[-1]

    @pl.when(i < nu_ref[0])
    def _():
        @pl.when(j == 0)
        def _():
            hi, lo = _unpack_halves(x_ref[...])
            xb_ref[:, :half] = hi.astype(BF16)
            xb_ref[:, half:] = lo.astype(BF16)

        xb = xb_ref[...]
        glu = jnp.dot(xb, wg_ref[0].astype(BF16), preferred_element_type=F32) + bg_ref[0]
        lin = jnp.dot(xb, wl_ref[0].astype(BF16), preferred_element_type=F32) + bl_ref[0]
        glu = jnp.minimum(glu, SWIGLU_LIMIT)
        lin = jnp.clip(lin, -SWIGLU_LIMIT, SWIGLU_LIMIT)
        a_ref[...] = (glu * jax.nn.sigmoid(SWIGLU_ALPHA * glu) * (lin + 1.0)).astype(BF16)

    @pl.when(i >= nu_ref[0])
    def _():
        a_ref[...] = jnp.zeros(a_ref.shape, BF16)


def _moe_down_kernel(be_ref, nu_ref, a_ref, wa_ref, wb_ref, ba_ref, bb_ref, y_ref):
    i = pl.program_id(0)

    @pl.when(i < nu_ref[0])
    def _():
        a = a_ref[...]
        ya = jnp.dot(a, wa_ref[0].astype(BF16), preferred_element_type=F32) + ba_ref[0]
        yb = jnp.dot(a, wb_ref[0].astype(BF16), preferred_element_type=F32) + bb_ref[0]
        y_ref[...] = _pack_pair(ya, yb)

    @pl.when(i >= nu_ref[0])
    def _():
        y_ref[...] = jnp.zeros(y_ref.shape, U32)


def _moe_call(block_expert, n_used, xs, w_gu, b_gu, w_down, b_down, tm, tf, tn):
    cap, half = xs.shape
    d = 2 * half
    ne, _, two_ff = w_gu.shape
    d_ff = two_ff // 2
    nf = d_ff // tf
    nn = half // tn
    nblk = cap // tm

    def keep(i, j, nu, last):
        return jnp.where(i < nu[0], j, last)

    up_spec = pltpu.PrefetchScalarGridSpec(
        num_scalar_prefetch=2,
        grid=(nblk, nf),
        in_specs=[pl.BlockSpec((tm, half), lambda i, j, be, nu: (i, 0)),
                  pl.BlockSpec((1, d, tf), lambda i, j, be, nu: (be[i], 0, keep(i, j, nu, nf - 1))),
                  pl.BlockSpec((1, d, tf), lambda i, j, be, nu: (be[i], 0, nf + keep(i, j, nu, nf - 1))),
                  pl.BlockSpec((1, 1, tf), lambda i, j, be, nu: (be[i], 0, keep(i, j, nu, nf - 1))),
                  pl.BlockSpec((1, 1, tf), lambda i, j, be, nu: (be[i], 0, nf + keep(i, j, nu, nf - 1)))],
        out_specs=pl.BlockSpec((tm, tf), lambda i, j, be, nu: (i, j)),
        scratch_shapes=[pltpu.VMEM((tm, d), BF16)],
    )
    b_gu3 = b_gu.reshape(ne, 1, two_ff)
    act = pl.pallas_call(
        _moe_up_kernel,
        out_shape=jax.ShapeDtypeStruct((cap, d_ff), BF16),
        grid_spec=up_spec,
        compiler_params=_cparams("arbitrary", "arbitrary"),
        name="moe_up",
    )(block_expert, n_used, xs, w_gu, w_gu, b_gu3, b_gu3)

    down_spec = pltpu.PrefetchScalarGridSpec(
        num_scalar_prefetch=2,
        grid=(nblk, nn),
        in_specs=[pl.BlockSpec((tm, d_ff), lambda i, n, be, nu: (i, 0)),
                  pl.BlockSpec((1, d_ff, tn), lambda i, n, be, nu: (be[i], 0, keep(i, n, nu, nn - 1))),
                  pl.BlockSpec((1, d_ff, tn), lambda i, n, be, nu: (be[i], 0, nn + keep(i, n, nu, nn - 1))),
                  pl.BlockSpec((1, 1, tn), lambda i, n, be, nu: (be[i], 0, keep(i, n, nu, nn - 1))),
                  pl.BlockSpec((1, 1, tn), lambda i, n, be, nu: (be[i], 0, nn + keep(i, n, nu, nn - 1)))],
        out_specs=pl.BlockSpec((tm, tn), lambda i, n, be, nu: (i, n)),
    )
    b_d3 = b_down.reshape(ne, 1, d)
    return pl.pallas_call(
        _moe_down_kernel,
        out_shape=jax.ShapeDtypeStruct((cap, half), U32),
        grid_spec=down_spec,
        compiler_params=_cparams("arbitrary", "arbitrary"),
        name="moe_down",
    )(block_expert, n_used, act, w_down, w_down, b_d3, b_d3)


def _combine_kernel(dest_ref, gate_ref, hx_ref, mod_ref, fg_ref, y_hbm, o_ref, buf, sem, *, tbc, d):
    def body(tt, c):
        for k in range(TOP_K):
            tile = buf.at[k, pl.ds(pl.multiple_of(tt * SUBLANES, SUBLANES), SUBLANES), :]
            for s in range(SUBLANES):
                dd = dest_ref[(tt * SUBLANES + s) * TOP_K + k]
                pltpu.make_async_copy(y_hbm.at[pl.ds(dd, 1), :], tile.at[pl.ds(s, 1), :], sem).start(priority=k % 2)
        return c
    lax.fori_loop(0, tbc // SUBLANES, body, 0)
    for k in range(TOP_K):
        pltpu.make_async_copy(y_hbm.at[pl.ds(0, tbc), :], buf.at[k], sem).wait()
    gate = gate_ref[...]
    moe_a, moe_b = None, None
    for k in range(TOP_K):
        ya, yb = _unpack_halves(buf[k])
        gk = gate[:, k:k + 1]
        moe_a = gk * ya if k == 0 else moe_a + gk * ya
        moe_b = gk * yb if k == 0 else moe_b + gk * yb
    moe = jnp.concatenate([moe_a, moe_b], axis=-1)
    g2 = mod_ref[0, :, 5 * d:6 * d]
    hx = hx_ref[...] + g2 * moe
    o_ref[...] = _rms(hx) * fg_ref[...]


def _combine_call(dest_flat, gate, hx1, mod3, rows_per_batch, final_g, y, tbc):
    t, d = hx1.shape
    bpb = rows_per_batch // tbc
    kern = functools.partial(_combine_kernel, tbc=tbc, d=d)
    return pl.pallas_call(
        kern,
        out_shape=jax.ShapeDtypeStruct((t, d), F32),
        grid=(t // tbc,),
        in_specs=[pl.BlockSpec((tbc * TOP_K,), lambda i: (i,), memory_space=pltpu.SMEM),
                  pl.BlockSpec((tbc, TOP_K), lambda i: (i, 0)),
                  pl.BlockSpec((tbc, d), lambda i: (i, 0)),
                  pl.BlockSpec((1, 1, N_MOD * d), lambda i: (i // bpb, 0, 0)),
                  pl.BlockSpec((1, d), lambda i: (0, 0)),
                  pl.BlockSpec(memory_space=pl.ANY)],
        out_specs=pl.BlockSpec((tbc, d), lambda i: (i, 0)),
        scratch_shapes=[pltpu.VMEM((TOP_K, tbc, d // 2), U32), pltpu.SemaphoreType.DMA(())],
        compiler_params=_cparams("arbitrary"),
        name="combine",
    )(dest_flat, gate, hx1, mod3, final_g.reshape(1, d), y)


def _tiles(t_tokens, seq):
    tok = 512 if seq % 512 == 0 else seq
    moe_tm = 1024 if t_tokens * TOP_K >= 32 * 1024 else 256
    return dict(tok=tok, route=tok, moe_tm=moe_tm,
                disp=tok, comb=tok)


def kernel(x, c, ctx, c_ctx, w_mod, b_mod, norm1_g, norm2_g, w_in, lru_conv_w, lru_conv_b, lru_wa, lru_ba,
           lru_wi, lru_bi, lru_lambda, hy_conv_w, hy_conv_b, hy_w1, hy_b1, hy_w2, hy_b2, hy_w3, hy_b3,
           hy_freq, hy_w4, hy_d, gn_lru, gn_hy, w_out, router_w, router_b, exp_w_gu, exp_b_gu,
           exp_w_down, exp_b_down, final_g):
    depth = w_mod.shape[0]
    assert depth == 1, "single-layer stack: the context stream only feeds the latent scan states"
    n_batch, seq, d = x.shape
    ctx_len = ctx.shape[1]
    w_lru = lru_conv_w.shape[-1]
    w_hy = hy_d.shape[-1]
    nb_l, nb_h = w_lru // LANES, w_hy // LANES
    assert lru_wa.shape[2] == nb_l and lru_wa.shape[3] == LANES, "gate blocks must be 128 wide"
    assert seq % (GRID_W * SUBLANES) == 0
    ne = router_w.shape[-1]
    t_tokens = n_batch * seq
    tl = _tiles(t_tokens, seq)
    l = 0

    n_rows = -(-(n_batch + 1) // SUBLANES) * SUBLANES
    cstack = jnp.concatenate([c, c_ctx[None, :], jnp.zeros((n_rows - n_batch - 1, d), F32)], axis=0)
    mod = _mod_call(cstack, w_mod[l], b_mod[l])
    mod3 = mod.reshape(n_rows, 1, N_MOD * d)

    w_in_b = w_in[l].astype(BF16)
    x2d = x.reshape(t_tokens, d)
    bpb = seq // tl["tok"]
    p_x = _inproj_call(x2d, mod3, lambda i: i // bpb, norm1_g[l], w_in_b, tl["tok"])
    ctx_tm = ctx_len if ctx_len <= 512 else 256
    p_c = _inproj_call(ctx.reshape(n_batch * ctx_len, d), mod3, lambda i: n_batch, norm1_g[l],
                       w_in_b[:, :2 * w_lru], ctx_tm)

    lru_args = (lru_conv_w[l], lru_conv_b[l], lru_wa[l], lru_ba[l], lru_wi[l], lru_bi[l], lru_lambda[l])
    _, h_ctx = _lru_call(p_c, nb_l, n_batch, ctx_len, *lru_args, jnp.zeros((n_batch, 2, w_lru), F32))
    y_lru, _ = _lru_call(p_x, nb_l, n_batch, seq, *lru_args, h_ctx)

    rows = seq // GRID_W
    tables = _dft_tables(rows)
    hfilt = _filt_call(seq, w_hy, hy_w1[l], hy_b1[l], hy_w2[l], hy_b2[l], hy_w3[l], hy_b3[l], hy_freq[l], hy_w4[l])
    kf = _spec_call(hfilt, seq, nb_h, tables[0], tables[1])
    cw, cb = hy_conv_w[l], hy_conv_b[l]
    s0 = 2 * nb_l
    z = _hyena_call(p_x, s0, p_x, s0 + nb_h, cw[:, :w_hy], cb[:w_hy], cw[:, w_hy:2 * w_hy], cb[w_hy:2 * w_hy],
                    hy_d[l, 0], kf, 0, tables, n_batch, seq, nb_h, True)
    hy = _hyena_call(z, 0, p_x, s0 + 2 * nb_h, cw[:, :w_hy], cb[:w_hy], cw[:, 2 * w_hy:], cb[2 * w_hy:],
                     hy_d[l, 1], kf, 1, tables, n_batch, seq, nb_h, False)

    hx1, m, logits = _outproj_call(y_lru, hy, x2d, mod3, seq, gn_lru[l], gn_hy[l], norm2_g[l],
                                   w_out[l].astype(BF16), router_w[l], router_b[l], tl["tok"])

    idx, gate, rank, counts = _route_call(logits, tl["route"])
    tm = tl["moe_tm"]
    counts = counts[0]
    pcnt = (counts + tm - 1) // tm * tm
    pend = jnp.cumsum(pcnt)
    pstart = pend - pcnt
    e_ids = jnp.arange(ne, dtype=I32)
    dest = (jnp.sum(jnp.where(idx[:, :, None] == e_ids, pstart.astype(I32), 0), axis=-1) + rank).reshape(-1)
    nblk = t_tokens * TOP_K // tm + ne
    cap = nblk * tm
    n_used = (pend[-1] // tm).astype(I32).reshape(1)
    blk_ids = jnp.arange(nblk, dtype=I32)
    block_expert = jnp.minimum(jnp.sum((pend[None, :] <= (blk_ids * tm)[:, None]).astype(I32), axis=1), ne - 1)
    block_expert = jnp.where(blk_ids < n_used[0], block_expert, block_expert[n_used[0] - 1])

    xs = _dispatch_call(dest, pend.astype(I32), pcnt.astype(I32), n_used, m, cap, tm, tl["disp"])
    d_ff = exp_w_down.shape[2]
    tf = next((c for c in (512, 256) if d_ff % c == 0), d_ff)
    tn = 512 if (d // 2) % 512 == 0 else d // 2
    y = _moe_call(block_expert, n_used, xs, exp_w_gu[l], exp_b_gu[l], exp_w_down[l], exp_b_down[l], tm, tf, tn)
    out = _combine_call(dest, gate, hx1, mod3, seq, final_g, y, tl["comb"])
    return out.reshape(n_batch, seq, d)
```

```python
import functools
import math

import jax
import jax.numpy as jnp
from jax import lax
from jax.experimental import pallas as pl
from jax.experimental.pallas import tpu as pltpu

F32 = jnp.float32
BF16 = jnp.bfloat16
I32 = jnp.int32
U32 = jnp.uint32
HI = lax.Precision.HIGHEST

GRID_W = 64
EPS = 1e-6
N_MOD = 6
LRU_CONV = 4
LRU_C = 8.0
HY_CONV = 3
HY_ORDER = 2
HY_BANDS = 16
HY_TARGET = 1e-2
HY_FAST = 0.3
HY_SLOW = 1.5
TOP_K = 4
SWIGLU_LIMIT = 7.0
SWIGLU_ALPHA = 1.702

LANES = 128
SUBLANES = 8
VMEM_LIMIT_BYTES = 56 * 1024 * 1024

DFT_N1 = 2 * GRID_W
K1_USED = DFT_N1 // 2 + 1
K1_PAD = 72
assert K1_PAD % SUBLANES == 0 and K1_PAD >= K1_USED
LOOP_UNROLL = 32
K1_UNROLL = 16
assert (K1_USED - 1) % K1_UNROLL == 0


def _cparams(*sem):
    return pltpu.CompilerParams(dimension_semantics=sem, vmem_limit_bytes=VMEM_LIMIT_BYTES)


def _rms(x):
    return x * lax.rsqrt(jnp.mean(x * x, axis=-1, keepdims=True) + EPS)


def _pack_pair(a, b):
    wa = pltpu.bitcast(a.astype(BF16).astype(F32), U32)
    wb = pltpu.bitcast(b.astype(BF16).astype(F32), U32)
    return wa | (wb >> 16)


def _pack_halves(v):
    h = v.shape[-1] // 2
    return _pack_pair(v[:, :h], v[:, h:])


def _unpack_halves(w):
    return pltpu.bitcast(w & jnp.uint32(0xFFFF0000), F32), pltpu.bitcast(w << 16, F32)


def _mod_kernel(c_ref, w_ref, b_ref, o_ref):
    c = c_ref[...]
    s = c * jax.nn.sigmoid(c)
    o_ref[...] = jnp.dot(s, w_ref[...], preferred_element_type=F32, precision=HI) + b_ref[...]


def _mod_call(cstack, w_mod, b_mod):
    rows, d = cstack.shape
    n = w_mod.shape[1]
    tn = 1536 if n % 1536 == 0 else n
    return pl.pallas_call(
        _mod_kernel,
        out_shape=jax.ShapeDtypeStruct((rows, n), F32),
        grid=(n // tn,),
        in_specs=[pl.BlockSpec((rows, d), lambda j: (0, 0)),
                  pl.BlockSpec((d, tn), lambda j: (0, j)),
                  pl.BlockSpec((1, tn), lambda j: (0, j))],
        out_specs=pl.BlockSpec((rows, tn), lambda j: (0, j)),
        compiler_params=_cparams("arbitrary"),
        name="mod",
    )(cstack, w_mod, b_mod.reshape(1, n))


def _inproj_kernel(x_ref, mod_ref, g_ref, w_ref, o_ref, *, d, n_slabs, nchunk):
    x = x_ref[...]
    sh = mod_ref[0, :, 0 * d:1 * d]
    sc = mod_ref[0, :, 1 * d:2 * d]
    a = (_rms(x) * g_ref[...]) * (1.0 + sc) + sh
    ab = a.astype(BF16)
    spc = nchunk // LANES
    for j in range(n_slabs // spc):
        p = jnp.dot(ab, w_ref[:, j * nchunk:(j + 1) * nchunk], preferred_element_type=F32)
        for s in range(spc):
            o_ref[j * spc + s] = p[:, s * LANES:(s + 1) * LANES].astype(BF16)


def _inproj_call(x2d, mod3, mod_row_of_block, g, w_bf16, tm):
    t, d = x2d.shape
    n = w_bf16.shape[1]
    n_slabs = n // LANES
    nchunk = 512 if n % 512 == 0 else LANES
    kern = functools.partial(_inproj_kernel, d=d, n_slabs=n_slabs, nchunk=nchunk)
    return pl.pallas_call(
        kern,
        out_shape=jax.ShapeDtypeStruct((n_slabs, t, LANES), BF16),
        grid=(t // tm,),
        in_specs=[pl.BlockSpec((tm, d), lambda i: (i, 0)),
                  pl.BlockSpec((1, 1, N_MOD * d), lambda i: (mod_row_of_block(i), 0, 0)),
                  pl.BlockSpec((1, d), lambda i: (0, 0)),
                  pl.BlockSpec((d, n), lambda i: (0, 0), pipeline_mode=pl.Buffered(1))],
        out_specs=pl.BlockSpec((n_slabs, tm, LANES), lambda i: (0, i, 0)),
        compiler_params=_cparams("arbitrary"),
        name="inproj",
    )(x2d, mod3, g.reshape(1, d), w_bf16)


def _gelu_tanh(x):
    return 0.5 * x * (1.0 + jnp.tanh(math.sqrt(2.0 / math.pi) * (x + 0.044715 * (x * x * x))))


def _scan_tile(a, b, row, reverse):
    for s in (1, 2, 4):
        if reverse:
            m = row < SUBLANES - s
            sh = SUBLANES - s
        else:
            m = row >= s
            sh = s
        a_s = jnp.where(m, pltpu.roll(a, sh, 0), 1.0)
        b_s = jnp.where(m, pltpu.roll(b, sh, 0), 0.0)
        b = a * b_s + b
        a = a * a_s
    return a, b


def _lru_kernel(r_ref, g_ref, cw_ref, cb_ref, wa_ref, wi_ref, ba_ref, bi_ref, lam_ref, h0_ref,
                y_ref, hl_ref, xp_ref, hf_ref, ab_ref, bb_ref, *, t_len, tc):
    nc = t_len // tc
    ntile = tc // SUBLANES
    zero8 = jnp.zeros((SUBLANES, LANES), F32)
    xp_ref[0:SUBLANES, :] = zero8
    xp_ref[t_len + SUBLANES:t_len + 2 * SUBLANES, :] = zero8

    def copy_body(ci, c):
        t0 = pl.multiple_of(ci * tc, tc)
        xp_ref[pl.ds(t0 + SUBLANES, tc), :] = r_ref[0, pl.ds(t0, tc), :].astype(F32)
        return c
    lax.fori_loop(0, nc, copy_body, 0)

    row = lax.broadcasted_iota(I32, (SUBLANES, LANES), 0)
    cw = cw_ref[...]
    cb = cb_ref[...]
    lam = lam_ref[...]
    sp = jnp.log1p(jnp.exp(-lam))
    wab = [wa_ref[d, 0].astype(BF16) for d in range(2)]
    wib = [wi_ref[d, 0].astype(BF16) for d in range(2)]

    def coeffs(u, ub, d):
        ga = jnp.dot(ub, wab[d], preferred_element_type=F32) + ba_ref[d:d + 1, :]
        gi = jnp.dot(ub, wib[d], preferred_element_type=F32) + bi_ref[d:d + 1, :]
        rg = jax.nn.sigmoid(ga)
        ig = jax.nn.sigmoid(gi)
        a = jnp.exp((-LRU_C) * rg * sp[d:d + 1, :])
        b = jnp.sqrt(1.0 - a * a) * (ig * u)
        return a, b

    def fwd_body(ci, hcar):
        t0 = pl.multiple_of(ci * tc, tc)
        blk = xp_ref[pl.ds(t0, tc + 2 * SUBLANES), :]
        u = (cw[0:1, :] * blk[6:6 + tc] + cw[1:2, :] * blk[7:7 + tc]
             + cw[2:3, :] * blk[8:8 + tc] + cw[3:4, :] * blk[9:9 + tc]) + cb
        ub = u.astype(BF16)
        a0, b0 = coeffs(u, ub, 0)
        a1, b1 = coeffs(u, ub, 1)
        ab_ref[pl.ds(t0, tc), :] = a1
        bb_ref[pl.ds(t0, tc), :] = b1
        for j in range(ntile):
            at, bt = _scan_tile(a0[j * 8:(j + 1) * 8], b0[j * 8:(j + 1) * 8], row, False)
            hf_ref[pl.ds(t0 + j * 8, 8), :] = at * hcar + bt
            atot = jnp.broadcast_to(at[7:8, :], (SUBLANES, LANES))
            btot = jnp.broadcast_to(bt[7:8, :], (SUBLANES, LANES))
            hcar = atot * hcar + btot
        return hcar

    h0f = jnp.broadcast_to(h0_ref[0, 0:1, :], (SUBLANES, LANES))
    hfin = lax.fori_loop(0, nc, fwd_body, h0f)
    hl_ref[0, 0:1, :] = hfin[0:1, :]

    def bwd_body(k, hcar):
        ci = nc - 1 - k
        t0 = pl.multiple_of(ci * tc, tc)
        a1 = ab_ref[pl.ds(t0, tc), :]
        b1 = bb_ref[pl.ds(t0, tc), :]
        gg = g_ref[0, pl.ds(t0, tc), :].astype(F32)
        gl = _gelu_tanh(gg)
        for j in reversed(range(ntile)):
            at, bt = _scan_tile(a1[j * 8:(j + 1) * 8], b1[j * 8:(j + 1) * 8], row, True)
            hb = at * hcar + bt
            hf = hf_ref[pl.ds(t0 + j * 8, 8), :]
            y_ref[0, pl.ds(t0 + j * 8, 8), :] = ((hf + hb) * gl[j * 8:(j + 1) * 8]).astype(y_ref.dtype)
            atot = jnp.broadcast_to(at[0:1, :], (SUBLANES, LANES))
            btot = jnp.broadcast_to(bt[0:1, :], (SUBLANES, LANES))
            hcar = atot * hcar + btot
        return hcar

    h0b = jnp.broadcast_to(h0_ref[0, 1:2, :], (SUBLANES, LANES))
    hfin_b = lax.fori_loop(0, nc, bwd_body, h0b)
    hl_ref[0, 1:2, :] = hfin_b[0:1, :]


def _lru_call(p_slabs, nb, n_batch, t_len, conv_w, conv_b, wa, ba, wi, bi, lam, h0):
    tc = 128 if t_len % 128 == 0 else t_len
    kern = functools.partial(_lru_kernel, t_len=t_len, tc=tc)
    w_lru = nb * LANES
    return pl.pallas_call(
        kern,
        out_shape=[jax.ShapeDtypeStruct((nb, n_batch * t_len, LANES), F32),
                   jax.ShapeDtypeStruct((n_batch, 2, w_lru), F32)],
        grid=(nb, n_batch),
        in_specs=[pl.BlockSpec((1, t_len, LANES), lambda n, b: (n, b, 0)),
                  pl.BlockSpec((1, t_len, LANES), lambda n, b: (nb + n, b, 0)),
                  pl.BlockSpec((LRU_CONV, LANES), lambda n, b: (0, n)),
                  pl.BlockSpec((1, LANES), lambda n, b: (0, n)),
                  pl.BlockSpec((2, 1, LANES, LANES), lambda n, b: (0, n, 0, 0)),
                  pl.BlockSpec((2, 1, LANES, LANES), lambda n, b: (0, n, 0, 0)),
                  pl.BlockSpec((2, LANES), lambda n, b: (0, n)),
                  pl.BlockSpec((2, LANES), lambda n, b: (0, n)),
                  pl.BlockSpec((2, LANES), lambda n, b: (0, n)),
                  pl.BlockSpec((1, 2, LANES), lambda n, b: (b, 0, n))],
        out_specs=[pl.BlockSpec((1, t_len, LANES), lambda n, b: (n, b, 0)),
                   pl.BlockSpec((1, 2, LANES), lambda n, b: (b, 0, n))],
        scratch_shapes=[pltpu.VMEM((t_len + 2 * SUBLANES, LANES), F32),
                        pltpu.VMEM((t_len, LANES), F32),
                        pltpu.VMEM((t_len, LANES), F32),
                        pltpu.VMEM((t_len, LANES), F32)],
        compiler_params=_cparams("arbitrary", "arbitrary"),
        name="lru",
    )(p_slabs, p_slabs, conv_w, conv_b.reshape(1, w_lru), wa, wi, ba, bi, lam, h0)


def _filt_kernel(w1_ref, b1_ref, w2_ref, b2_ref, w3_ref, b3_ref, fr_ref, w4_ref, o_ref,
                 *, l_len, rows, tl, w_hy):
    i = pl.program_id(0)
    rho = i * tl + lax.broadcasted_iota(I32, (tl, 1), 0)
    tidx = ((rho % GRID_W) * rows + rho // GRID_W).astype(F32)
    tt = tidx * (1.0 / (l_len - 1))
    ww = (2.0 * math.pi) * tidx / l_len
    band = lax.broadcasted_iota(I32, (1, HY_BANDS), 1).astype(F32)
    f = 1e-4 + band * ((HY_BANDS - 1 - 1e-4) / (HY_BANDS - 1))
    fw = f * ww
    w1 = w1_ref[...]
    pre = (tt * w1[0:1, :]
           + jnp.dot(jnp.cos(fw), w1[1:1 + HY_BANDS, :], preferred_element_type=F32, precision=HI)
           + jnp.dot(-jnp.sin(fw), w1[1 + HY_BANDS:1 + 2 * HY_BANDS, :], preferred_element_type=F32, precision=HI))
    fr = fr_ref[...]
    h = jnp.sin(fr * (pre + b1_ref[...]))
    h = jnp.sin(fr * (jnp.dot(h, w2_ref[...], preferred_element_type=F32, precision=HI) + b2_ref[...]))
    h = jnp.sin(fr * (jnp.dot(h, w3_ref[...], preferred_element_type=F32, precision=HI) + b3_ref[...]))
    max_decay = math.log(HY_TARGET) / HY_FAST
    min_decay = math.log(HY_TARGET) / HY_SLOW
    nblk = w_hy // LANES
    h_hi = h.astype(BF16)
    h_lo = (h - h_hi.astype(F32)).astype(BF16)
    for cb in range(nblk):
        ch = (cb * LANES + lax.broadcasted_iota(I32, (1, LANES), 1)).astype(F32)
        delta = jnp.abs(min_decay + ch * ((max_decay - min_decay) / (w_hy - 1)))
        decay = jnp.exp(-tt * delta)
        for od in range(HY_ORDER * 2):
            col = od * w_hy + cb * LANES
            w4 = w4_ref[:, col:col + LANES]
            w_hi = w4.astype(BF16)
            w_lo = (w4 - w_hi.astype(F32)).astype(BF16)
            v = (jnp.dot(h_hi, w_hi, preferred_element_type=F32) + jnp.dot(h_lo, w_hi, preferred_element_type=F32)
                 + jnp.dot(h_hi, w_lo, preferred_element_type=F32))
            o_ref[od * nblk + cb] = v * decay


def _filt_call(l_len, w_hy, w1, b1, w2, b2, w3, b3, freq, w4):
    rows = l_len // GRID_W
    tl = 512 if l_len % 512 == 0 else l_len
    fh = w2.shape[0]
    nslab = HY_ORDER * 2 * (w_hy // LANES)
    kern = functools.partial(_filt_kernel, l_len=l_len, rows=rows, tl=tl, w_hy=w_hy)
    full = lambda a: pl.BlockSpec(a.shape, lambda i: (0,) * a.ndim)
    args = (w1, b1.reshape(1, fh), w2, b2.reshape(1, fh), w3, b3.reshape(1, fh), freq.reshape(1, fh), w4)
    return pl.pallas_call(
        kern,
        out_shape=jax.ShapeDtypeStruct((nslab, l_len, LANES), F32),
        grid=(l_len // tl,),
        in_specs=[full(a) for a in args],
        out_specs=pl.BlockSpec((nslab, tl, LANES), lambda i: (0, i, 0)),
        compiler_params=_cparams("arbitrary"),
        name="filt",
    )(*args)


def _dft_tables(rows):
    n_len = DFT_N1 * rows
    k1 = jnp.arange(K1_PAD, dtype=I32)
    r = jnp.arange(rows, dtype=I32)
    w = jnp.arange(GRID_W, dtype=I32)
    idx = (k1[None, :, None] * (r[:, None, None] + rows * w[None, None, :])) % n_len
    ang = idx.astype(F32) * (2.0 * math.pi / n_len)
    valid = (k1 < K1_USED)[None, :, None]
    ca = jnp.where(valid, jnp.cos(ang), 0.0)
    sa = jnp.where(valid, jnp.sin(ang), 0.0)
    ta = jnp.stack([ca, -sa], axis=2).reshape(rows, 2 * K1_PAD, GRID_W)
    td = jnp.transpose(ta, (0, 2, 1))
    k2 = jnp.arange(rows, dtype=I32)
    th = ((k2[:, None] * r[None, :]) % rows).astype(F32) * (2.0 * math.pi / rows)
    c, s = jnp.cos(th), jnp.sin(th)
    tb = jnp.concatenate([jnp.stack([c, s], axis=2).reshape(rows, 2 * rows),
                          jnp.stack([-s, c], axis=2).reshape(rows, 2 * rows)], axis=0)
    tcm = jnp.stack([jnp.concatenate([c, -s], axis=1), jnp.concatenate([s, c], axis=1)], axis=1).reshape(2 * rows, 2 * rows)
    return ta.astype(BF16), tb.astype(BF16), tcm.astype(BF16), td.astype(BF16)


def _pitch_r(rows):
    return rows + SUBLANES


def _stage_a(tile_fn, ta_ref, s1_ref, rows):
    pitch = _pitch_r(rows)

    def body(r, c):
        out = jnp.dot(ta_ref[r], tile_fn(r), preferred_element_type=F32)
        words = pltpu.bitcast(out.astype(BF16), U32)
        s1_ref[pl.ds(r, K1_PAD, stride=pitch), :] = words
        return c
    lax.fori_loop(0, rows, body, 0, unroll=min(LOOP_UNROLL, rows))


def _loop_k1(body):
    lax.fori_loop(0, K1_USED - 1, body, 0, unroll=K1_UNROLL)
    body(K1_USED - 1, 0)


def _stage_b(k1, tb_ref, s1_ref, rows):
    pitch = _pitch_r(rows)
    t = s1_ref[pl.ds(pl.multiple_of(k1 * pitch, SUBLANES), rows), :]
    y = jnp.dot(tb_ref[...], pltpu.bitcast(t, BF16), preferred_element_type=F32)
    return y[:rows], y[rows:]


def _spec_kernel(hf_ref, hb_ref, ta_ref, tb_ref, kf_ref, s1_ref, *, rows):
    n_len = DFT_N1 * rows

    def scale(k1):
        return jnp.where((k1 == 0) | (k1 == K1_USED - 1), 1.0, 2.0) * (1.0 / n_len)

    _stage_a(lambda r: hf_ref[0, pl.ds(pl.multiple_of(r * GRID_W, GRID_W), GRID_W), :].astype(BF16),
             ta_ref, s1_ref, rows)

    def b_fwd(k1, c):
        yre, yim = _stage_b(k1, tb_ref, s1_ref, rows)
        sc = scale(k1)
        kf_ref[0, 0, k1, 0] = yre * sc
        kf_ref[0, 0, k1, 1] = yim * sc
        return c
    _loop_k1(b_fwd)

    rid = lax.broadcasted_iota(I32, (GRID_W, LANES), 0)

    def tile_b(r):
        x = hb_ref[0, pl.ds(pl.multiple_of(r * GRID_W, GRID_W), GRID_W), :]
        x = jnp.where((rid == 0) & (r == 0), 0.0, x)
        return x.astype(BF16)
    _stage_a(tile_b, ta_ref, s1_ref, rows)

    def b_bwd(k1, c):
        yre, yim = _stage_b(k1, tb_ref, s1_ref, rows)
        sc = scale(k1)
        kf_ref[0, 0, k1, 0] += yre * sc
        kf_ref[0, 0, k1, 1] -= yim * sc
        return c
    _loop_k1(b_bwd)


def _spec_call(hfilt, l_len, nblk, ta, tb):
    rows = l_len // GRID_W
    kern = functools.partial(_spec_kernel, rows=rows)
    return pl.pallas_call(
        kern,
        out_shape=jax.ShapeDtypeStruct((HY_ORDER, nblk, K1_USED, 2, rows, LANES), F32),
        grid=(HY_ORDER, nblk),
        in_specs=[pl.BlockSpec((1, l_len, LANES), lambda o, c: (o * 2 * nblk + c, 0, 0)),
                  pl.BlockSpec((1, l_len, LANES), lambda o, c: (o * 2 * nblk + nblk + c, 0, 0)),
                  pl.BlockSpec(ta.shape, lambda o, c: (0, 0, 0), pipeline_mode=pl.Buffered(1)),
                  pl.BlockSpec(tb.shape, lambda o, c: (0, 0), pipeline_mode=pl.Buffered(1))],
        out_specs=pl.BlockSpec((1, 1, K1_USED, 2, rows, LANES), lambda o, c: (o, c, 0, 0, 0, 0)),
        scratch_shapes=[pltpu.VMEM((K1_PAD * _pitch_r(rows), LANES), U32)],
        compiler_params=_cparams("arbitrary", "arbitrary"),
        name="spec",
    )(hfilt, hfilt, ta, tb)


def _conv_grid_order(src_ref, dst_ref, w_ref, b_ref, rows):
    w0, w1, w2, bb = w_ref[0:1, :], w_ref[1:2, :], w_ref[2:3, :], b_ref[...]
    g = GRID_W
    rid = lax.broadcasted_iota(I32, (g, LANES), 0)

    def grp(r0):
        return src_ref[0, pl.ds(r0, g), :].astype(F32)

    last = grp((rows - 1) * g)
    prev0 = jnp.where(rid >= 1, pltpu.roll(last, 1, 0), 0.0)
    dst_ref[pl.ds(0, g), :] = (w0 * prev0 + w1 * grp(0) + w2 * grp(g) + bb).astype(dst_ref.dtype)
    first = grp(0)
    nxt = jnp.where(rid < g - 1, pltpu.roll(first, g - 1, 0), 0.0)
    dst_ref[pl.ds((rows - 1) * g, g), :] = (w0 * grp((rows - 2) * g) + w1 * last + w2 * nxt + bb).astype(dst_ref.dtype)

    def body(r, c):
        rm = pl.multiple_of((r - 1) * g, g)
        r0 = pl.multiple_of(r * g, g)
        rp = pl.multiple_of((r + 1) * g, g)
        v = w0 * grp(rm) + w1 * grp(r0) + w2 * grp(rp) + bb
        dst_ref[pl.ds(r0, g), :] = v.astype(dst_ref.dtype)
        return c
    lax.fori_loop(1, rows - 1, body, 0, unroll=2)


def _hyena_kernel(sig_ref, gate_ref, cws_ref, cbs_ref, cwg_ref, cbg_ref, d_ref, kf_ref,
                  ta_ref, tb_ref, tc_ref, td_ref, o_ref, u_ref, gt_ref, s1_ref, s2_ref,
                  *, rows, conv_sig):
    g = GRID_W

    @pl.when((pl.program_id(0) == 0) & (pl.program_id(1) == 0))
    def _():
        s2_ref[...] = jnp.zeros(s2_ref.shape, U32)

    if conv_sig:
        _conv_grid_order(sig_ref, u_ref, cws_ref, cbs_ref, rows)
    else:
        def cp(r, c):
            r0 = pl.multiple_of(r * g, g)
            u_ref[pl.ds(r0, g), :] = sig_ref[0, pl.ds(r0, g), :]
            return c
        lax.fori_loop(0, rows, cp, 0, unroll=min(LOOP_UNROLL, rows))
    _conv_grid_order(gate_ref, gt_ref, cwg_ref, cbg_ref, rows)

    _stage_a(lambda r: u_ref[pl.ds(pl.multiple_of(r * g, g), g), :], ta_ref, s1_ref, rows)

    def bc_body(k1, c):
        yre, yim = _stage_b(k1, tb_ref, s1_ref, rows)
        kre = kf_ref[0, 0, k1, 0]
        kim = kf_ref[0, 0, k1, 1]
        zre = yre * kre - yim * kim
        zim = yre * kim + yim * kre
        z = jnp.concatenate([zre, zim], axis=0).astype(BF16)
        v = jnp.dot(tc_ref[...], z, preferred_element_type=F32)
        s2_ref[pl.ds(k1, rows, stride=K1_PAD), :] = pltpu.bitcast(v.astype(BF16), U32)
        return c
    _loop_k1(bc_body)

    dd = d_ref[...]

    def d_body(r, c):
        r0 = pl.multiple_of(r * g, g)
        t = s2_ref[pl.ds(pl.multiple_of(r * K1_PAD, SUBLANES), K1_PAD), :]
        y = jnp.dot(td_ref[r], pltpu.bitcast(t, BF16), preferred_element_type=F32)
        uu = u_ref[pl.ds(r0, g), :].astype(F32)
        o_ref[0, pl.ds(r0, g), :] = (gt_ref[pl.ds(r0, g), :] * (y + dd * uu)).astype(o_ref.dtype)
        return c
    lax.fori_loop(0, rows, d_body, 0, unroll=min(LOOP_UNROLL, rows))


def _hyena_call(sig_arr, sig_slab0, gate_arr, gate_slab0, conv_w_sig, conv_b_sig, conv_w_gate, conv_b_gate,
                d_vec, kf, order, tables, n_batch, l_len, nblk, conv_sig):
    rows = l_len // GRID_W
    ta, tb, tcm, td = tables
    kern = functools.partial(_hyena_kernel, rows=rows, conv_sig=conv_sig)
    w_hy = nblk * LANES
    const = lambda a: pl.BlockSpec(a.shape, lambda c, b: (0,) * a.ndim, pipeline_mode=pl.Buffered(1))
    return pl.pallas_call(
        kern,
        out_shape=jax.ShapeDtypeStruct((nblk, n_batch * l_len, LANES), BF16),
        grid=(nblk, n_batch),
        in_specs=[pl.BlockSpec((1, l_len, LANES), lambda c, b: (sig_slab0 + c, b, 0)),
                  pl.BlockSpec((1, l_len, LANES), lambda c, b: (gate_slab0 + c, b, 0)),
                  pl.BlockSpec((HY_CONV, LANES), lambda c, b: (0, c)),
                  pl.BlockSpec((1, LANES), lambda c, b: (0, c)),
                  pl.BlockSpec((HY_CONV, LANES), lambda c, b: (0, c)),
                  pl.BlockSpec((1, LANES), lambda c, b: (0, c)),
                  pl.BlockSpec((1, LANES), lambda c, b: (0, c)),
                  pl.BlockSpec((1, 1, K1_USED, 2, rows, LANES), lambda c, b: (order, c, 0, 0, 0, 0),
                               pipeline_mode=pl.Buffered(1)),
                  const(ta), const(tb), const(tcm), const(td)],
        out_specs=pl.BlockSpec((1, l_len, LANES), lambda c, b: (c, b, 0)),
        scratch_shapes=[pltpu.VMEM((l_len, LANES), BF16),
                        pltpu.VMEM((l_len, LANES), F32),
                        pltpu.VMEM((K1_PAD * _pitch_r(rows), LANES), U32),
                        pltpu.VMEM((rows * K1_PAD, LANES), U32)],
        compiler_params=_cparams("arbitrary", "arbitrary"),
        name="hyena%d" % order,
    )(sig_arr, gate_arr, conv_w_sig, conv_b_sig.reshape(1, w_hy), conv_w_gate, conv_b_gate.reshape(1, w_hy),
      d_vec.reshape(1, w_hy), kf, ta, tb, tcm, td)


def _outproj_kernel(yl_ref, hy_ref, x_ref, mod_ref, gl_ref, gh_ref, n2_ref, w_ref, rw2_ref, rb_ref,
                    hx_ref, m_ref, lg_ref, *, d, nb_l, nb_h, ne):
    yl = jnp.concatenate([yl_ref[j].astype(F32) for j in range(nb_l)], axis=-1)
    hy = jnp.concatenate([hy_ref[j].astype(F32) for j in range(nb_h)], axis=-1)
    cat = jnp.concatenate([_rms(yl) * gl_ref[...], _rms(hy) * gh_ref[...]], axis=-1).astype(BF16)
    out = jnp.dot(cat, w_ref[...], preferred_element_type=F32)
    g1 = mod_ref[0, :, 2 * d:3 * d]
    sh2 = mod_ref[0, :, 3 * d:4 * d]
    sc2 = mod_ref[0, :, 4 * d:5 * d]
    hx = x_ref[...] + g1 * out
    hx_ref[...] = hx
    m = (_rms(hx) * n2_ref[...]) * (1.0 + sc2) + sh2
    m_ref[...] = _pack_halves(m)
    m_hi = m.astype(BF16)
    m_lo = (m - m_hi.astype(F32)).astype(BF16)
    a = jnp.dot(m_hi, rw2_ref[...], preferred_element_type=F32)
    b = jnp.dot(m_lo, rw2_ref[:, :ne], preferred_element_type=F32)
    lg_ref[...] = (a[:, :ne] + b) + a[:, ne:] + rb_ref[...]


def _outproj_call(y_lru, hy, x2d, mod3, rows_per_batch, gn_lru, gn_hy, norm2_g, w_out_bf16, router_w, router_b, tm):
    t, d = x2d.shape
    nb_l, nb_h = y_lru.shape[0], hy.shape[0]
    ne = router_w.shape[1]
    bpb = rows_per_batch // tm
    rw_hi = router_w.astype(BF16)
    rw_lo = (router_w - rw_hi.astype(F32)).astype(BF16)
    rw2 = jnp.concatenate([rw_hi, rw_lo], axis=1)
    kern = functools.partial(_outproj_kernel, d=d, nb_l=nb_l, nb_h=nb_h, ne=ne)
    return pl.pallas_call(
        kern,
        out_shape=[jax.ShapeDtypeStruct((t, d), F32), jax.ShapeDtypeStruct((t, d // 2), U32),
                   jax.ShapeDtypeStruct((t, ne), F32)],
        grid=(t // tm,),
        in_specs=[pl.BlockSpec((nb_l, tm, LANES), lambda i: (0, i, 0)),
                  pl.BlockSpec((nb_h, tm, LANES), lambda i: (0, i, 0)),
                  pl.BlockSpec((tm, d), lambda i: (i, 0)),
                  pl.BlockSpec((1, 1, N_MOD * d), lambda i: (i // bpb, 0, 0)),
                  pl.BlockSpec((1, nb_l * LANES), lambda i: (0, 0)),
                  pl.BlockSpec((1, nb_h * LANES), lambda i: (0, 0)),
                  pl.BlockSpec((1, d), lambda i: (0, 0)),
                  pl.BlockSpec(w_out_bf16.shape, lambda i: (0, 0), pipeline_mode=pl.Buffered(1)),
                  pl.BlockSpec((d, 2 * ne), lambda i: (0, 0)),
                  pl.BlockSpec((1, ne), lambda i: (0, 0))],
        out_specs=[pl.BlockSpec((tm, d), lambda i: (i, 0)),
                   pl.BlockSpec((tm, d // 2), lambda i: (i, 0)),
                   pl.BlockSpec((tm, ne), lambda i: (i, 0))],
        compiler_params=_cparams("arbitrary"),
        name="outproj",
    )(y_lru, hy, x2d, mod3, gn_lru.reshape(1, -1), gn_hy.reshape(1, -1), norm2_g.reshape(1, d),
      w_out_bf16, rw2, router_b.reshape(1, ne))


def _route_kernel(lg_ref, idx_ref, gate_ref, rank_ref, cnt_ref, run_ref, *, tb, ne):
    @pl.when(pl.program_id(0) == 0)
    def _():
        run_ref[...] = jnp.zeros(run_ref.shape, F32)

    l = lg_ref[...]
    lane = lax.broadcasted_iota(I32, (tb, ne), 1)
    vals, idxs, ohs = [], [], []
    for _ in range(TOP_K):
        m = jnp.max(l, axis=-1, keepdims=True)
        ix = jnp.min(jnp.where(l == m, lane, ne), axis=-1, keepdims=True)
        sel = lane == ix
        vals.append(m)
        idxs.append(ix)
        ohs.append(sel.astype(F32))
        l = jnp.where(sel, -jnp.inf, l)
    es = [jnp.exp(v - vals[0]) for v in vals]
    den = es[0] + es[1] + es[2] + es[3]
    oh_all = ohs[0] + ohs[1] + ohs[2] + ohs[3]
    ri = lax.broadcasted_iota(I32, (tb, tb), 0)
    ci = lax.broadcasted_iota(I32, (tb, tb), 1)
    ltri = (ci < ri).astype(BF16)
    before = jnp.dot(ltri, oh_all.astype(BF16), preferred_element_type=F32) + run_ref[...]
    for k in range(TOP_K):
        idx_ref[:, k:k + 1] = idxs[k]
        gate_ref[:, k:k + 1] = es[k] / den
        rank_ref[:, k:k + 1] = jnp.sum(ohs[k] * before, axis=-1, keepdims=True).astype(I32)
    run_ref[...] += jnp.sum(oh_all, axis=0, keepdims=True)
    cnt_ref[...] = run_ref[...].astype(I32)


def _route_call(logits, tb):
    t, ne = logits.shape
    kern = functools.partial(_route_kernel, tb=tb, ne=ne)
    blk = lambda: pl.BlockSpec((tb, TOP_K), lambda i: (i, 0))
    return pl.pallas_call(
        kern,
        out_shape=[jax.ShapeDtypeStruct((t, TOP_K), I32), jax.ShapeDtypeStruct((t, TOP_K), F32),
                   jax.ShapeDtypeStruct((t, TOP_K), I32), jax.ShapeDtypeStruct((1, ne), I32)],
        grid=(t // tb,),
        in_specs=[pl.BlockSpec((tb, ne), lambda i: (i, 0))],
        out_specs=[blk(), blk(), blk(), pl.BlockSpec((1, ne), lambda i: (0, 0))],
        scratch_shapes=[pltpu.VMEM((1, ne), F32)],
        compiler_params=_cparams("arbitrary"),
        name="route",
    )(logits)


def _dispatch_kernel(dest_ref, pend_ref, pcnt_ref, nu_ref, m_ref, xs_hbm, zbuf, sem, zsem, *, tbd, tm, ne, nblk):
    i = pl.program_id(0)

    @pl.when(i == 0)
    def _():
        zbuf[...] = jnp.zeros(zbuf.shape, U32)

        def zero_block(row0):
            cp = pltpu.make_async_copy(zbuf, xs_hbm.at[pl.ds(pl.multiple_of(row0, tm), tm), :], zsem)
            cp.start()
            cp.wait()

        def zfill(e, c):
            @pl.when(pcnt_ref[e] > 0)
            def _():
                zero_block(pend_ref[e] - tm)
            return c
        lax.fori_loop(0, ne, zfill, 0)

        def ztail(bi, c):
            zero_block(bi * tm)
            return c
        lax.fori_loop(nu_ref[0], nblk, ztail, 0)

    def body(tt, c):
        tile = m_ref.at[pl.ds(pl.multiple_of(tt * SUBLANES, SUBLANES), SUBLANES), :]
        for s in range(SUBLANES):
            for k in range(TOP_K):
                d = dest_ref[(tt * SUBLANES + s) * TOP_K + k]
                pltpu.make_async_copy(tile.at[pl.ds(s, 1), :], xs_hbm.at[pl.ds(d, 1), :], sem).start(priority=k % 2)
        return c
    lax.fori_loop(0, tbd // SUBLANES, body, 0)
    for k in range(TOP_K):
        pltpu.make_async_copy(m_ref, xs_hbm.at[pl.ds(0, tbd), :], sem).wait()


def _dispatch_call(dest_flat, pend, pcnt, n_used, m, cap, tm, tbd):
    t, d = m.shape
    ne = pend.shape[0]
    kern = functools.partial(_dispatch_kernel, tbd=tbd, tm=tm, ne=ne, nblk=cap // tm)
    return pl.pallas_call(
        kern,
        out_shape=jax.ShapeDtypeStruct((cap, d), U32),
        grid=(t // tbd,),
        in_specs=[pl.BlockSpec((tbd * TOP_K,), lambda i: (i,), memory_space=pltpu.SMEM),
                  pl.BlockSpec(memory_space=pltpu.SMEM),
                  pl.BlockSpec(memory_space=pltpu.SMEM),
                  pl.BlockSpec(memory_space=pltpu.SMEM),
                  pl.BlockSpec((tbd, d), lambda i: (i, 0))],
        out_specs=pl.BlockSpec(memory_space=pl.ANY),
        scratch_shapes=[pltpu.VMEM((tm, d), U32), pltpu.SemaphoreType.DMA(()), pltpu.SemaphoreType.DMA(())],
        compiler_params=_cparams("arbitrary"),
        name="dispatch",
    )(dest_flat, pend, pcnt, n_used, m)


MOE_SUB = 4


def _row_blocks(nv, tm, compute, fill_zero):
    @pl.when(nv == tm)
    def _():
        compute(0, tm)

    tq = tm // MOE_SUB
    for q in range(MOE_SUB):
        @pl.when((nv < tm) & (nv > q * tq))
        def _():
            compute(q * tq, tq)

        @pl.when(nv <= q * tq)
        def _():
            fill_zero(q * tq, tq)


def _moe_up_kernel(be_ref, nu_ref, nv_ref, x_ref, wg_ref, wl_ref, bg_ref, bl_ref, a_ref, xb_ref):
    i = pl.program_id(0)
    j = pl.program_id(1)
    half = x_ref.shape[-1]
    tm = x_ref.shape[0]
    nv = nv_ref[i]

    @pl.when((nv > 0) & (j == 0))
    def _():
        hi, lo = _unpack_halves(x_ref[...])
        xb_ref[:, :half] = hi.astype(BF16)
        xb_ref[:, half:] = lo.astype(BF16)

    def compute(r0, nr):
        xb = xb_ref[r0:r0 + nr, :]
        glu = jnp.dot(xb, wg_ref[0].astype(BF16), preferred_element_type=F32) + bg_ref[0]
        lin = jnp.dot(xb, wl_ref[0].astype(BF16), preferred_element_type=F32) + bl_ref[0]
        glu = jnp.minimum(glu, SWIGLU_LIMIT)
        lin = jnp.clip(lin, -SWIGLU_LIMIT, SWIGLU_LIMIT)
        a_ref[r0:r0 + nr, :] = (glu * jax.nn.sigmoid(SWIGLU_ALPHA * glu) * (lin + 1.0)).astype(BF16)

    def fill_zero(r0, nr):
        a_ref[r0:r0 + nr, :] = jnp.zeros((nr, a_ref.shape[1]), BF16)

    _row_blocks(nv, tm, compute, fill_zero)


def _moe_down_kernel(be_ref, nu_ref, nv_ref, a_ref, wa_ref, wb_ref, ba_ref, bb_ref, y_ref):
    i = pl.program_id(1)
    tm = a_ref.shape[0]

    def compute(r0, nr):
        a = a_ref[r0:r0 + nr, :]
        ya = jnp.dot(a, wa_ref[0].astype(BF16), preferred_element_type=F32) + ba_ref[0]
        yb = jnp.dot(a, wb_ref[0].astype(BF16), preferred_element_type=F32) + bb_ref[0]
        y_ref[r0:r0 + nr, :] = _pack_pair(ya, yb)

    def fill_zero(r0, nr):
        y_ref[r0:r0 + nr, :] = jnp.zeros((nr, y_ref.shape[1]), U32)

    _row_blocks(nv_ref[i], tm, compute, fill_zero)


def _moe_call(block_expert, n_used, n_valid, xs, w_gu, b_gu, w_down, b_down, tm, tf, tn):
    cap, half = xs.shape
    d = 2 * half
    ne, _, two_ff = w_gu.shape
    d_ff = two_ff // 2
    nf = d_ff // tf
    nn = half // tn
    nblk = cap // tm
    assert tm % (MOE_SUB * 2 * SUBLANES) == 0

    def keep(i, j, nu, last):
        return jnp.where(i < nu[0], j, last)

    up_spec = pltpu.PrefetchScalarGridSpec(
        num_scalar_prefetch=3,
        grid=(nblk, nf),
        in_specs=[pl.BlockSpec((tm, half), lambda i, j, be, nu, nv: (i, 0)),
                  pl.BlockSpec((1, d, tf), lambda i, j, be, nu, nv: (be[i], 0, keep(i, j, nu, nf - 1))),
                  pl.BlockSpec((1, d, tf), lambda i, j, be, nu, nv: (be[i], 0, nf + keep(i, j, nu, nf - 1))),
                  pl.BlockSpec((1, 1, tf), lambda i, j, be, nu, nv: (be[i], 0, keep(i, j, nu, nf - 1))),
                  pl.BlockSpec((1, 1, tf), lambda i, j, be, nu, nv: (be[i], 0, nf + keep(i, j, nu, nf - 1)))],
        out_specs=pl.BlockSpec((tm, tf), lambda i, j, be, nu, nv: (i, j)),
        scratch_shapes=[pltpu.VMEM((tm, d), BF16)],
    )
    b_gu3 = b_gu.reshape(ne, 1, two_ff)
    act = pl.pallas_call(
        _moe_up_kernel,
        out_shape=jax.ShapeDtypeStruct((cap, d_ff), BF16),
        grid_spec=up_spec,
        compiler_params=_cparams("arbitrary", "arbitrary"),
        name="moe_up",
    )(block_expert, n_used, n_valid, xs, w_gu, w_gu, b_gu3, b_gu3)

    down_spec = pltpu.PrefetchScalarGridSpec(
        num_scalar_prefetch=3,
        grid=(nn, nblk),
        in_specs=[pl.BlockSpec((tm, d_ff), lambda n, i, be, nu, nv: (i, 0)),
                  pl.BlockSpec((1, d_ff, tn), lambda n, i, be, nu, nv: (be[i], 0, n)),
                  pl.BlockSpec((1, d_ff, tn), lambda n, i, be, nu, nv: (be[i], 0, nn + n)),
                  pl.BlockSpec((1, 1, tn), lambda n, i, be, nu, nv: (be[i], 0, n)),
                  pl.BlockSpec((1, 1, tn), lambda n, i, be, nu, nv: (be[i], 0, nn + n))],
        out_specs=pl.BlockSpec((tm, tn), lambda n, i, be, nu, nv: (i, n)),
    )
    b_d3 = b_down.reshape(ne, 1, d)
    return pl.pallas_call(
        _moe_down_kernel,
        out_shape=jax.ShapeDtypeStruct((cap, half), U32),
        grid_spec=down_spec,
        compiler_params=_cparams("arbitrary", "arbitrary"),
        name="moe_down",
    )(block_expert, n_used, n_valid, act, w_down, w_down, b_d3, b_d3)


def _combine_kernel(dest_ref, gate_ref, hx_ref, mod_ref, fg_ref, y_hbm, o_ref, buf, sem, *, tbc, d, nsteps):
    s = pl.program_id(0)
    slot_in = s % 2
    slot_out = (s + 1) % 2
    g2 = mod_ref[0, :, 5 * d:6 * d]
    fg = fg_ref[...]

    def issue(tt):
        r0 = pl.multiple_of(tt * SUBLANES, SUBLANES)
        for k in range(TOP_K):
            tile = buf.at[slot_in, k, pl.ds(r0, SUBLANES), :]
            for r in range(SUBLANES):
                dd = dest_ref[(tt * SUBLANES + r) * TOP_K + k]
                pltpu.make_async_copy(y_hbm.at[pl.ds(dd, 1), :], tile.at[pl.ds(r, 1), :],
                                      sem.at[slot_in]).start(priority=k % 2)

    def reduce_rows(r0, nr):
        gate = gate_ref[pl.ds(r0, nr), :]
        moe_a, moe_b = None, None
        for k in range(TOP_K):
            ya, yb = _unpack_halves(buf[slot_out, k, pl.ds(r0, nr), :])
            gk = gate[:, k:k + 1]
            moe_a = gk * ya if k == 0 else moe_a + gk * ya
            moe_b = gk * yb if k == 0 else moe_b + gk * yb
        moe = jnp.concatenate([moe_a, moe_b], axis=-1)
        hx = hx_ref[pl.ds(r0, nr), :] + g2 * moe
        o_ref[pl.ds(r0, nr), :] = _rms(hx) * fg

    def wait_prev():
        for k in range(TOP_K):
            pltpu.make_async_copy(y_hbm.at[pl.ds(0, tbc), :], buf.at[slot_out, k], sem.at[slot_out]).wait()

    @pl.when(s == 0)
    def _():
        def body(tt, c):
            issue(tt)
            return c
        lax.fori_loop(0, tbc // SUBLANES, body, 0)

    @pl.when((s > 0) & (s < nsteps))
    def _():
        wait_prev()

        def body(tt, c):
            issue(tt)
            reduce_rows(pl.multiple_of(tt * SUBLANES, SUBLANES), SUBLANES)
            return c
        lax.fori_loop(0, tbc // SUBLANES, body, 0, unroll=8)

    @pl.when(s == nsteps)
    def _():
        wait_prev()
        reduce_rows(0, tbc)


def _combine_call(dest_flat, gate, hx1, mod3, rows_per_batch, final_g, y, tbc):
    t, d = hx1.shape
    bpb = rows_per_batch // tbc
    nsteps = t // tbc
    kern = functools.partial(_combine_kernel, tbc=tbc, d=d, nsteps=nsteps)
    prev = lambda s: jnp.maximum(s - 1, 0)
    return pl.pallas_call(
        kern,
        out_shape=jax.ShapeDtypeStruct((t, d), F32),
        grid=(nsteps + 1,),
        in_specs=[pl.BlockSpec((tbc * TOP_K,), lambda s: (jnp.minimum(s, nsteps - 1),), memory_space=pltpu.SMEM),
                  pl.BlockSpec((tbc, TOP_K), lambda s: (prev(s), 0)),
                  pl.BlockSpec((tbc, d), lambda s: (prev(s), 0)),
                  pl.BlockSpec((1, 1, N_MOD * d), lambda s: (prev(s) // bpb, 0, 0)),
                  pl.BlockSpec((1, d), lambda s: (0, 0)),
                  pl.BlockSpec(memory_space=pl.ANY)],
        out_specs=pl.BlockSpec((tbc, d), lambda s: (prev(s), 0)),
        scratch_shapes=[pltpu.VMEM((2, TOP_K, tbc, d // 2), U32), pltpu.SemaphoreType.DMA((2,))],
        compiler_params=_cparams("arbitrary"),
        name="combine",
    )(dest_flat, gate, hx1, mod3, final_g.reshape(1, d), y)


def _tiles(t_tokens, seq):
    tok = 512 if seq % 512 == 0 else seq
    moe_tm = 1024 if t_tokens * TOP_K >= 32 * 1024 else 256
    return dict(tok=tok, route=tok, moe_tm=moe_tm,
                disp=tok, comb=tok)


def kernel(x, c, ctx, c_ctx, w_mod, b_mod, norm1_g, norm2_g, w_in, lru_conv_w, lru_conv_b, lru_wa, lru_ba,
           lru_wi, lru_bi, lru_lambda, hy_conv_w, hy_conv_b, hy_w1, hy_b1, hy_w2, hy_b2, hy_w3, hy_b3,
           hy_freq, hy_w4, hy_d, gn_lru, gn_hy, w_out, router_w, router_b, exp_w_gu, exp_b_gu,
           exp_w_down, exp_b_down, final_g):
    depth = w_mod.shape[0]
    assert depth == 1, "single-layer stack: the context stream only feeds the latent scan states"
    n_batch, seq, d = x.shape
    ctx_len = ctx.shape[1]
    w_lru = lru_conv_w.shape[-1]
    w_hy = hy_d.shape[-1]
    nb_l, nb_h = w_lru // LANES, w_hy // LANES
    assert lru_wa.shape[2] == nb_l and lru_wa.shape[3] == LANES, "gate blocks must be 128 wide"
    assert seq % (GRID_W * SUBLANES) == 0
    ne = router_w.shape[-1]
    t_tokens = n_batch * seq
    tl = _tiles(t_tokens, seq)
    l = 0

    n_rows = -(-(n_batch + 1) // SUBLANES) * SUBLANES
    cstack = jnp.concatenate([c, c_ctx[None, :], jnp.zeros((n_rows - n_batch - 1, d), F32)], axis=0)
    mod = _mod_call(cstack, w_mod[l], b_mod[l])
    mod3 = mod.reshape(n_rows, 1, N_MOD * d)

    w_in_b = w_in[l].astype(BF16)
    x2d = x.reshape(t_tokens, d)
    bpb = seq // tl["tok"]
    p_x = _inproj_call(x2d, mod3, lambda i: i // bpb, norm1_g[l], w_in_b, tl["tok"])
    ctx_tm = ctx_len if ctx_len <= 512 else 256
    p_c = _inproj_call(ctx.reshape(n_batch * ctx_len, d), mod3, lambda i: n_batch, norm1_g[l],
                       w_in_b[:, :2 * w_lru], ctx_tm)

    lru_args = (lru_conv_w[l], lru_conv_b[l], lru_wa[l], lru_ba[l], lru_wi[l], lru_bi[l], lru_lambda[l])
    _, h_ctx = _lru_call(p_c, nb_l, n_batch, ctx_len, *lru_args, jnp.zeros((n_batch, 2, w_lru), F32))
    y_lru, _ = _lru_call(p_x, nb_l, n_batch, seq, *lru_args, h_ctx)

    rows = seq // GRID_W
    tables = _dft_tables(rows)
    hfilt = _filt_call(seq, w_hy, hy_w1[l], hy_b1[l], hy_w2[l], hy_b2[l], hy_w3[l], hy_b3[l], hy_freq[l], hy_w4[l])
    kf = _spec_call(hfilt, seq, nb_h, tables[0], tables[1])
    cw, cb = hy_conv_w[l], hy_conv_b[l]
    s0 = 2 * nb_l
    z = _hyena_call(p_x, s0, p_x, s0 + nb_h, cw[:, :w_hy], cb[:w_hy], cw[:, w_hy:2 * w_hy], cb[w_hy:2 * w_hy],
                    hy_d[l, 0], kf, 0, tables, n_batch, seq, nb_h, True)
    hy = _hyena_call(z, 0, p_x, s0 + 2 * nb_h, cw[:, :w_hy], cb[:w_hy], cw[:, 2 * w_hy:], cb[2 * w_hy:],
                     hy_d[l, 1], kf, 1, tables, n_batch, seq, nb_h, False)

    hx1, m, logits = _outproj_call(y_lru, hy, x2d, mod3, seq, gn_lru[l], gn_hy[l], norm2_g[l],
                                   w_out[l].astype(BF16), router_w[l], router_b[l], tl["tok"])

    idx, gate, rank, counts = _route_call(logits, tl["route"])
    tm = tl["moe_tm"]
    counts = counts[0]
    pcnt = (counts + tm - 1) // tm * tm
    pend = jnp.cumsum(pcnt)
    pstart = pend - pcnt
    e_ids = jnp.arange(ne, dtype=I32)
    dest = (jnp.sum(jnp.where(idx[:, :, None] == e_ids, pstart.astype(I32), 0), axis=-1) + rank).reshape(-1)
    nblk = t_tokens * TOP_K // tm + ne
    cap = nblk * tm
    n_used = (pend[-1] // tm).astype(I32).reshape(1)
    blk_ids = jnp.arange(nblk, dtype=I32)
    block_expert = jnp.minimum(jnp.sum((pend[None, :] <= (blk_ids * tm)[:, None]).astype(I32), axis=1), ne - 1)
    block_expert = jnp.where(blk_ids < n_used[0], block_expert, block_expert[n_used[0] - 1])
    group_end = (pstart + counts).astype(I32)
    n_valid = jnp.where(blk_ids < n_used[0], jnp.clip(group_end[block_expert] - blk_ids * tm, 0, tm), 0).astype(I32)

    xs = _dispatch_call(dest, pend.astype(I32), pcnt.astype(I32), n_used, m, cap, tm, tl["disp"])
    d_ff = exp_w_down.shape[2]
    tf = next((c for c in (512, 256) if d_ff % c == 0), d_ff)
    tn = 512 if (d // 2) % 512 == 0 else d // 2
    y = _moe_call(block_expert, n_used, n_valid, xs, exp_w_gu[l], exp_b_gu[l], exp_w_down[l], exp_b_down[l],
                  tm, tf, tn)
    out = _combine_call(dest, gate, hx1, mod3, seq, final_g, y, tl["comb"])
    return out.reshape(n_batch, seq, d)
```

```python
import functools
import math

import jax
import jax.numpy as jnp
from jax import lax
from jax.experimental import pallas as pl
from jax.experimental.pallas import tpu as pltpu

F32 = jnp.float32
BF16 = jnp.bfloat16
I32 = jnp.int32
U32 = jnp.uint32
HI = lax.Precision.HIGHEST

GRID_W = 64
EPS = 1e-6
N_MOD = 6
LRU_CONV = 4
LRU_C = 8.0
HY_CONV = 3
HY_ORDER = 2
HY_BANDS = 16
HY_TARGET = 1e-2
HY_FAST = 0.3
HY_SLOW = 1.5
TOP_K = 4
SWIGLU_LIMIT = 7.0
SWIGLU_ALPHA = 1.702

LANES = 128
SUBLANES = 8
VMEM_LIMIT_BYTES = 56 * 1024 * 1024
VMEM_LIMIT_MOE_BYTES = 60 * 1024 * 1024

DFT_N1 = 2 * GRID_W
K1_USED = DFT_N1 // 2 + 1
K1_PAD = 72
assert K1_PAD % SUBLANES == 0 and K1_PAD >= K1_USED
LOOP_UNROLL = 32
K1_UNROLL = 16
assert (K1_USED - 1) % K1_UNROLL == 0


def _cparams(*sem, vmem=VMEM_LIMIT_BYTES):
    return pltpu.CompilerParams(dimension_semantics=sem, vmem_limit_bytes=vmem)


def _rms(x):
    return x * lax.rsqrt(jnp.mean(x * x, axis=-1, keepdims=True) + EPS)


def _pack_pair(a, b):
    wa = pltpu.bitcast(a.astype(BF16).astype(F32), U32)
    wb = pltpu.bitcast(b.astype(BF16).astype(F32), U32)
    return wa | (wb >> 16)


def _pack_halves(v):
    h = v.shape[-1] // 2
    return _pack_pair(v[:, :h], v[:, h:])


def _unpack_halves(w):
    return pltpu.bitcast(w & jnp.uint32(0xFFFF0000), F32), pltpu.bitcast(w << 16, F32)


def _mod_kernel(c_ref, w_ref, b_ref, o_ref):
    c = c_ref[...]
    s = c * jax.nn.sigmoid(c)
    o_ref[...] = jnp.dot(s, w_ref[...], preferred_element_type=F32, precision=HI) + b_ref[...]


def _mod_call(cstack, w_mod, b_mod):
    rows, d = cstack.shape
    n = w_mod.shape[1]
    tn = 1536 if n % 1536 == 0 else n
    return pl.pallas_call(
        _mod_kernel,
        out_shape=jax.ShapeDtypeStruct((rows, n), F32),
        grid=(n // tn,),
        in_specs=[pl.BlockSpec((rows, d), lambda j: (0, 0)),
                  pl.BlockSpec((d, tn), lambda j: (0, j)),
                  pl.BlockSpec((1, tn), lambda j: (0, j))],
        out_specs=pl.BlockSpec((rows, tn), lambda j: (0, j)),
        compiler_params=_cparams("arbitrary"),
        name="mod",
    )(cstack, w_mod, b_mod.reshape(1, n))


def _inproj_kernel(x_ref, mod_ref, g_ref, w_ref, o_ref, *, d, n_slabs, nchunk):
    x = x_ref[...]
    sh = mod_ref[0, :, 0 * d:1 * d]
    sc = mod_ref[0, :, 1 * d:2 * d]
    a = (_rms(x) * g_ref[...]) * (1.0 + sc) + sh
    ab = a.astype(BF16)
    spc = nchunk // LANES
    for j in range(n_slabs // spc):
        p = jnp.dot(ab, w_ref[:, j * nchunk:(j + 1) * nchunk], preferred_element_type=F32)
        for s in range(spc):
            o_ref[j * spc + s] = p[:, s * LANES:(s + 1) * LANES].astype(BF16)


def _inproj_call(x2d, mod3, mod_row_of_block, g, w_bf16, tm):
    t, d = x2d.shape
    n = w_bf16.shape[1]
    n_slabs = n // LANES
    nchunk = 512 if n % 512 == 0 else LANES
    kern = functools.partial(_inproj_kernel, d=d, n_slabs=n_slabs, nchunk=nchunk)
    return pl.pallas_call(
        kern,
        out_shape=jax.ShapeDtypeStruct((n_slabs, t, LANES), BF16),
        grid=(t // tm,),
        in_specs=[pl.BlockSpec((tm, d), lambda i: (i, 0)),
                  pl.BlockSpec((1, 1, N_MOD * d), lambda i: (mod_row_of_block(i), 0, 0)),
                  pl.BlockSpec((1, d), lambda i: (0, 0)),
                  pl.BlockSpec((d, n), lambda i: (0, 0), pipeline_mode=pl.Buffered(1))],
        out_specs=pl.BlockSpec((n_slabs, tm, LANES), lambda i: (0, i, 0)),
        compiler_params=_cparams("arbitrary"),
        name="inproj",
    )(x2d, mod3, g.reshape(1, d), w_bf16)


def _gelu_tanh(x):
    return 0.5 * x * (1.0 + jnp.tanh(math.sqrt(2.0 / math.pi) * (x + 0.044715 * (x * x * x))))


def _scan_tile(a, b, row, reverse):
    for s in (1, 2, 4):
        if reverse:
            m = row < SUBLANES - s
            sh = SUBLANES - s
        else:
            m = row >= s
            sh = s
        a_s = jnp.where(m, pltpu.roll(a, sh, 0), 1.0)
        b_s = jnp.where(m, pltpu.roll(b, sh, 0), 0.0)
        b = a * b_s + b
        a = a * a_s
    return a, b


def _lru_kernel(r_ref, g_ref, cw_ref, cb_ref, wa_ref, wi_ref, ba_ref, bi_ref, lam_ref, h0_ref,
                y_ref, hl_ref, xp_ref, hf_ref, ab_ref, bb_ref, *, t_len, tc):
    nc = t_len // tc
    ntile = tc // SUBLANES
    zero8 = jnp.zeros((SUBLANES, LANES), F32)
    xp_ref[0:SUBLANES, :] = zero8
    xp_ref[t_len + SUBLANES:t_len + 2 * SUBLANES, :] = zero8

    def copy_body(ci, c):
        t0 = pl.multiple_of(ci * tc, tc)
        xp_ref[pl.ds(t0 + SUBLANES, tc), :] = r_ref[0, pl.ds(t0, tc), :].astype(F32)
        return c
    lax.fori_loop(0, nc, copy_body, 0)

    row = lax.broadcasted_iota(I32, (SUBLANES, LANES), 0)
    cw = cw_ref[...]
    cb = cb_ref[...]
    lam = lam_ref[...]
    sp = jnp.log1p(jnp.exp(-lam))
    wab = [wa_ref[d, 0].astype(BF16) for d in range(2)]
    wib = [wi_ref[d, 0].astype(BF16) for d in range(2)]

    def coeffs(u, ub, d):
        ga = jnp.dot(ub, wab[d], preferred_element_type=F32) + ba_ref[d:d + 1, :]
        gi = jnp.dot(ub, wib[d], preferred_element_type=F32) + bi_ref[d:d + 1, :]
        rg = jax.nn.sigmoid(ga)
        ig = jax.nn.sigmoid(gi)
        a = jnp.exp((-LRU_C) * rg * sp[d:d + 1, :])
        b = jnp.sqrt(1.0 - a * a) * (ig * u)
        return a, b

    def fwd_body(ci, hcar):
        t0 = pl.multiple_of(ci * tc, tc)
        blk = xp_ref[pl.ds(t0, tc + 2 * SUBLANES), :]
        u = (cw[0:1, :] * blk[6:6 + tc] + cw[1:2, :] * blk[7:7 + tc]
             + cw[2:3, :] * blk[8:8 + tc] + cw[3:4, :] * blk[9:9 + tc]) + cb
        ub = u.astype(BF16)
        a0, b0 = coeffs(u, ub, 0)
        a1, b1 = coeffs(u, ub, 1)
        ab_ref[pl.ds(t0, tc), :] = a1
        bb_ref[pl.ds(t0, tc), :] = b1
        for j in range(ntile):
            at, bt = _scan_tile(a0[j * 8:(j + 1) * 8], b0[j * 8:(j + 1) * 8], row, False)
            hf_ref[pl.ds(t0 + j * 8, 8), :] = at * hcar + bt
            atot = jnp.broadcast_to(at[7:8, :], (SUBLANES, LANES))
            btot = jnp.broadcast_to(bt[7:8, :], (SUBLANES, LANES))
            hcar = atot * hcar + btot
        return hcar

    h0f = jnp.broadcast_to(h0_ref[0, 0:1, :], (SUBLANES, LANES))
    hfin = lax.fori_loop(0, nc, fwd_body, h0f, unroll=2 if nc % 2 == 0 else 1)
    hl_ref[0, 0:1, :] = hfin[0:1, :]

    def bwd_body(k, hcar):
        ci = nc - 1 - k
        t0 = pl.multiple_of(ci * tc, tc)
        a1 = ab_ref[pl.ds(t0, tc), :]
        b1 = bb_ref[pl.ds(t0, tc), :]
        gg = g_ref[0, pl.ds(t0, tc), :].astype(F32)
        gl = _gelu_tanh(gg)
        for j in reversed(range(ntile)):
            at, bt = _scan_tile(a1[j * 8:(j + 1) * 8], b1[j * 8:(j + 1) * 8], row, True)
            hb = at * hcar + bt
            hf = hf_ref[pl.ds(t0 + j * 8, 8), :]
            y_ref[0, pl.ds(t0 + j * 8, 8), :] = ((hf + hb) * gl[j * 8:(j + 1) * 8]).astype(y_ref.dtype)
            atot = jnp.broadcast_to(at[0:1, :], (SUBLANES, LANES))
            btot = jnp.broadcast_to(bt[0:1, :], (SUBLANES, LANES))
            hcar = atot * hcar + btot
        return hcar

    h0b = jnp.broadcast_to(h0_ref[0, 1:2, :], (SUBLANES, LANES))
    hfin_b = lax.fori_loop(0, nc, bwd_body, h0b)
    hl_ref[0, 1:2, :] = hfin_b[0:1, :]


def _lru_call(p_slabs, nb, n_batch, t_len, conv_w, conv_b, wa, ba, wi, bi, lam, h0):
    tc = 128 if t_len % 128 == 0 else t_len
    kern = functools.partial(_lru_kernel, t_len=t_len, tc=tc)
    w_lru = nb * LANES
    return pl.pallas_call(
        kern,
        out_shape=[jax.ShapeDtypeStruct((nb, n_batch * t_len, LANES), F32),
                   jax.ShapeDtypeStruct((n_batch, 2, w_lru), F32)],
        grid=(nb, n_batch),
        in_specs=[pl.BlockSpec((1, t_len, LANES), lambda n, b: (n, b, 0)),
                  pl.BlockSpec((1, t_len, LANES), lambda n, b: (nb + n, b, 0)),
                  pl.BlockSpec((LRU_CONV, LANES), lambda n, b: (0, n)),
                  pl.BlockSpec((1, LANES), lambda n, b: (0, n)),
                  pl.BlockSpec((2, 1, LANES, LANES), lambda n, b: (0, n, 0, 0)),
                  pl.BlockSpec((2, 1, LANES, LANES), lambda n, b: (0, n, 0, 0)),
                  pl.BlockSpec((2, LANES), lambda n, b: (0, n)),
                  pl.BlockSpec((2, LANES), lambda n, b: (0, n)),
                  pl.BlockSpec((2, LANES), lambda n, b: (0, n)),
                  pl.BlockSpec((1, 2, LANES), lambda n, b: (b, 0, n))],
        out_specs=[pl.BlockSpec((1, t_len, LANES), lambda n, b: (n, b, 0)),
                   pl.BlockSpec((1, 2, LANES), lambda n, b: (b, 0, n))],
        scratch_shapes=[pltpu.VMEM((t_len + 2 * SUBLANES, LANES), F32),
                        pltpu.VMEM((t_len, LANES), F32),
                        pltpu.VMEM((t_len, LANES), F32),
                        pltpu.VMEM((t_len, LANES), F32)],
        compiler_params=_cparams("arbitrary", "arbitrary"),
        name="lru",
    )(p_slabs, p_slabs, conv_w, conv_b.reshape(1, w_lru), wa, wi, ba, bi, lam, h0)


def _filt_kernel(w1_ref, b1_ref, w2_ref, b2_ref, w3_ref, b3_ref, fr_ref, w4_ref, o_ref,
                 *, l_len, rows, tl, w_hy):
    i = pl.program_id(0)
    rho = i * tl + lax.broadcasted_iota(I32, (tl, 1), 0)
    tidx = ((rho % GRID_W) * rows + rho // GRID_W).astype(F32)
    tt = tidx * (1.0 / (l_len - 1))
    ww = (2.0 * math.pi) * tidx / l_len
    band = lax.broadcasted_iota(I32, (1, HY_BANDS), 1).astype(F32)
    f = 1e-4 + band * ((HY_BANDS - 1 - 1e-4) / (HY_BANDS - 1))
    fw = f * ww
    w1 = w1_ref[...]
    pre = (tt * w1[0:1, :]
           + jnp.dot(jnp.cos(fw), w1[1:1 + HY_BANDS, :], preferred_element_type=F32, precision=HI)
           + jnp.dot(-jnp.sin(fw), w1[1 + HY_BANDS:1 + 2 * HY_BANDS, :], preferred_element_type=F32, precision=HI))
    fr = fr_ref[...]
    h = jnp.sin(fr * (pre + b1_ref[...]))
    h = jnp.sin(fr * (jnp.dot(h, w2_ref[...], preferred_element_type=F32, precision=HI) + b2_ref[...]))
    h = jnp.sin(fr * (jnp.dot(h, w3_ref[...], preferred_element_type=F32, precision=HI) + b3_ref[...]))
    max_decay = math.log(HY_TARGET) / HY_FAST
    min_decay = math.log(HY_TARGET) / HY_SLOW
    nblk = w_hy // LANES
    h_hi = h.astype(BF16)
    h_lo = (h - h_hi.astype(F32)).astype(BF16)
    for cb in range(nblk):
        ch = (cb * LANES + lax.broadcasted_iota(I32, (1, LANES), 1)).astype(F32)
        delta = jnp.abs(min_decay + ch * ((max_decay - min_decay) / (w_hy - 1)))
        decay = jnp.exp(-tt * delta)
        for od in range(HY_ORDER * 2):
            col = od * w_hy + cb * LANES
            w4 = w4_ref[:, col:col + LANES]
            w_hi = w4.astype(BF16)
            w_lo = (w4 - w_hi.astype(F32)).astype(BF16)
            v = (jnp.dot(h_hi, w_hi, preferred_element_type=F32) + jnp.dot(h_lo, w_hi, preferred_element_type=F32)
                 + jnp.dot(h_hi, w_lo, preferred_element_type=F32))
            o_ref[od * nblk + cb] = v * decay


def _filt_call(l_len, w_hy, w1, b1, w2, b2, w3, b3, freq, w4):
    rows = l_len // GRID_W
    tl = 512 if l_len % 512 == 0 else l_len
    fh = w2.shape[0]
    nslab = HY_ORDER * 2 * (w_hy // LANES)
    kern = functools.partial(_filt_kernel, l_len=l_len, rows=rows, tl=tl, w_hy=w_hy)
    full = lambda a: pl.BlockSpec(a.shape, lambda i: (0,) * a.ndim)
    args = (w1, b1.reshape(1, fh), w2, b2.reshape(1, fh), w3, b3.reshape(1, fh), freq.reshape(1, fh), w4)
    return pl.pallas_call(
        kern,
        out_shape=jax.ShapeDtypeStruct((nslab, l_len, LANES), F32),
        grid=(l_len // tl,),
        in_specs=[full(a) for a in args],
        out_specs=pl.BlockSpec((nslab, tl, LANES), lambda i: (0, i, 0)),
        compiler_params=_cparams("arbitrary"),
        name="filt",
    )(*args)


def _dft_tables(rows):
    n_len = DFT_N1 * rows
    k1 = jnp.arange(K1_PAD, dtype=I32)
    r = jnp.arange(rows, dtype=I32)
    w = jnp.arange(GRID_W, dtype=I32)
    idx = (k1[None, :, None] * (r[:, None, None] + rows * w[None, None, :])) % n_len
    ang = idx.astype(F32) * (2.0 * math.pi / n_len)
    valid = (k1 < K1_USED)[None, :, None]
    ca = jnp.where(valid, jnp.cos(ang), 0.0)
    sa = jnp.where(valid, jnp.sin(ang), 0.0)
    ta = jnp.stack([ca, -sa], axis=2).reshape(rows, 2 * K1_PAD, GRID_W)
    td = jnp.transpose(ta, (0, 2, 1))
    k2 = jnp.arange(rows, dtype=I32)
    th = ((k2[:, None] * r[None, :]) % rows).astype(F32) * (2.0 * math.pi / rows)
    c, s = jnp.cos(th), jnp.sin(th)
    tb = jnp.concatenate([jnp.stack([c, s], axis=2).reshape(rows, 2 * rows),
                          jnp.stack([-s, c], axis=2).reshape(rows, 2 * rows)], axis=0)
    tcm = jnp.stack([jnp.concatenate([c, -s], axis=1), jnp.concatenate([s, c], axis=1)], axis=1).reshape(2 * rows, 2 * rows)
    return ta.astype(BF16), tb.astype(BF16), tcm.astype(BF16), td.astype(BF16)


def _pitch_r(rows):
    return rows + SUBLANES


def _stage_a(tile_fn, ta_ref, s1_ref, rows):
    pitch = _pitch_r(rows)

    def body(r, c):
        out = jnp.dot(ta_ref[r], tile_fn(r), preferred_element_type=F32)
        words = pltpu.bitcast(out.astype(BF16), U32)
        s1_ref[pl.ds(r, K1_PAD, stride=pitch), :] = words
        return c
    lax.fori_loop(0, rows, body, 0, unroll=min(LOOP_UNROLL, rows))


def _loop_k1(body):
    lax.fori_loop(0, K1_USED - 1, body, 0, unroll=K1_UNROLL)
    body(K1_USED - 1, 0)


def _stage_b(k1, tb_ref, s1_ref, rows):
    pitch = _pitch_r(rows)
    t = s1_ref[pl.ds(pl.multiple_of(k1 * pitch, SUBLANES), rows), :]
    y = jnp.dot(tb_ref[...], pltpu.bitcast(t, BF16), preferred_element_type=F32)
    return y[:rows], y[rows:]


def _spec_kernel(hf_ref, hb_ref, ta_ref, tb_ref, kf_ref, s1_ref, *, rows):
    n_len = DFT_N1 * rows

    def scale(k1):
        return jnp.where((k1 == 0) | (k1 == K1_USED - 1), 1.0, 2.0) * (1.0 / n_len)

    _stage_a(lambda r: hf_ref[0, pl.ds(pl.multiple_of(r * GRID_W, GRID_W), GRID_W), :].astype(BF16),
             ta_ref, s1_ref, rows)

    def b_fwd(k1, c):
        yre, yim = _stage_b(k1, tb_ref, s1_ref, rows)
        sc = scale(k1)
        kf_ref[0, 0, k1, 0] = yre * sc
        kf_ref[0, 0, k1, 1] = yim * sc
        return c
    _loop_k1(b_fwd)

    rid = lax.broadcasted_iota(I32, (GRID_W, LANES), 0)

    def tile_b(r):
        x = hb_ref[0, pl.ds(pl.multiple_of(r * GRID_W, GRID_W), GRID_W), :]
        x = jnp.where((rid == 0) & (r == 0), 0.0, x)
        return x.astype(BF16)
    _stage_a(tile_b, ta_ref, s1_ref, rows)

    def b_bwd(k1, c):
        yre, yim = _stage_b(k1, tb_ref, s1_ref, rows)
        sc = scale(k1)
        kf_ref[0, 0, k1, 0] += yre * sc
        kf_ref[0, 0, k1, 1] -= yim * sc
        return c
    _loop_k1(b_bwd)


def _spec_call(hfilt, l_len, nblk, ta, tb):
    rows = l_len // GRID_W
    kern = functools.partial(_spec_kernel, rows=rows)
    return pl.pallas_call(
        kern,
        out_shape=jax.ShapeDtypeStruct((HY_ORDER, nblk, K1_USED, 2, rows, LANES), F32),
        grid=(HY_ORDER, nblk),
        in_specs=[pl.BlockSpec((1, l_len, LANES), lambda o, c: (o * 2 * nblk + c, 0, 0)),
                  pl.BlockSpec((1, l_len, LANES), lambda o, c: (o * 2 * nblk + nblk + c, 0, 0)),
                  pl.BlockSpec(ta.shape, lambda o, c: (0, 0, 0), pipeline_mode=pl.Buffered(1)),
                  pl.BlockSpec(tb.shape, lambda o, c: (0, 0), pipeline_mode=pl.Buffered(1))],
        out_specs=pl.BlockSpec((1, 1, K1_USED, 2, rows, LANES), lambda o, c: (o, c, 0, 0, 0, 0)),
        scratch_shapes=[pltpu.VMEM((K1_PAD * _pitch_r(rows), LANES), U32)],
        compiler_params=_cparams("arbitrary", "arbitrary"),
        name="spec",
    )(hfilt, hfilt, ta, tb)


def _conv_grid_order(src_ref, dst_ref, w_ref, b_ref, rows):
    w0, w1, w2, bb = w_ref[0:1, :], w_ref[1:2, :], w_ref[2:3, :], b_ref[...]
    g = GRID_W
    rid = lax.broadcasted_iota(I32, (g, LANES), 0)

    def grp(r0):
        return src_ref[0, pl.ds(r0, g), :].astype(F32)

    last = grp((rows - 1) * g)
    prev0 = jnp.where(rid >= 1, pltpu.roll(last, 1, 0), 0.0)
    dst_ref[pl.ds(0, g), :] = (w0 * prev0 + w1 * grp(0) + w2 * grp(g) + bb).astype(dst_ref.dtype)
    first = grp(0)
    nxt = jnp.where(rid < g - 1, pltpu.roll(first, g - 1, 0), 0.0)
    dst_ref[pl.ds((rows - 1) * g, g), :] = (w0 * grp((rows - 2) * g) + w1 * last + w2 * nxt + bb).astype(dst_ref.dtype)

    def body(r, c):
        rm = pl.multiple_of((r - 1) * g, g)
        r0 = pl.multiple_of(r * g, g)
        rp = pl.multiple_of((r + 1) * g, g)
        v = w0 * grp(rm) + w1 * grp(r0) + w2 * grp(rp) + bb
        dst_ref[pl.ds(r0, g), :] = v.astype(dst_ref.dtype)
        return c
    lax.fori_loop(1, rows - 1, body, 0, unroll=2)


def _hyena_kernel(sig_ref, gate_ref, cws_ref, cbs_ref, cwg_ref, cbg_ref, d_ref, kf_ref,
                  ta_ref, tb_ref, tc_ref, td_ref, o_ref, u_ref, gt_ref, s1_ref, s2_ref,
                  *, rows, conv_sig):
    g = GRID_W

    @pl.when((pl.program_id(0) == 0) & (pl.program_id(1) == 0))
    def _():
        s2_ref[...] = jnp.zeros(s2_ref.shape, U32)

    if conv_sig:
        _conv_grid_order(sig_ref, u_ref, cws_ref, cbs_ref, rows)
    else:
        def cp(r, c):
            r0 = pl.multiple_of(r * g, g)
            u_ref[pl.ds(r0, g), :] = sig_ref[0, pl.ds(r0, g), :]
            return c
        lax.fori_loop(0, rows, cp, 0, unroll=min(LOOP_UNROLL, rows))
    _conv_grid_order(gate_ref, gt_ref, cwg_ref, cbg_ref, rows)

    _stage_a(lambda r: u_ref[pl.ds(pl.multiple_of(r * g, g), g), :], ta_ref, s1_ref, rows)

    def bc_body(k1, c):
        yre, yim = _stage_b(k1, tb_ref, s1_ref, rows)
        kre = kf_ref[0, 0, k1, 0]
        kim = kf_ref[0, 0, k1, 1]
        zre = yre * kre - yim * kim
        zim = yre * kim + yim * kre
        z = jnp.concatenate([zre, zim], axis=0).astype(BF16)
        v = jnp.dot(tc_ref[...], z, preferred_element_type=F32)
        s2_ref[pl.ds(k1, rows, stride=K1_PAD), :] = pltpu.bitcast(v.astype(BF16), U32)
        return c
    _loop_k1(bc_body)

    dd = d_ref[...]

    def d_body(r, c):
        r0 = pl.multiple_of(r * g, g)
        t = s2_ref[pl.ds(pl.multiple_of(r * K1_PAD, SUBLANES), K1_PAD), :]
        y = jnp.dot(td_ref[r], pltpu.bitcast(t, BF16), preferred_element_type=F32)
        uu = u_ref[pl.ds(r0, g), :].astype(F32)
        o_ref[0, pl.ds(r0, g), :] = (gt_ref[pl.ds(r0, g), :] * (y + dd * uu)).astype(o_ref.dtype)
        return c
    lax.fori_loop(0, rows, d_body, 0, unroll=min(LOOP_UNROLL, rows))


def _hyena_call(sig_arr, sig_slab0, gate_arr, gate_slab0, conv_w_sig, conv_b_sig, conv_w_gate, conv_b_gate,
                d_vec, kf, order, tables, n_batch, l_len, nblk, conv_sig):
    rows = l_len // GRID_W
    ta, tb, tcm, td = tables
    kern = functools.partial(_hyena_kernel, rows=rows, conv_sig=conv_sig)
    w_hy = nblk * LANES
    const = lambda a: pl.BlockSpec(a.shape, lambda c, b: (0,) * a.ndim, pipeline_mode=pl.Buffered(1))
    return pl.pallas_call(
        kern,
        out_shape=jax.ShapeDtypeStruct((nblk, n_batch * l_len, LANES), BF16),
        grid=(nblk, n_batch),
        in_specs=[pl.BlockSpec((1, l_len, LANES), lambda c, b: (sig_slab0 + c, b, 0)),
                  pl.BlockSpec((1, l_len, LANES), lambda c, b: (gate_slab0 + c, b, 0)),
                  pl.BlockSpec((HY_CONV, LANES), lambda c, b: (0, c)),
                  pl.BlockSpec((1, LANES), lambda c, b: (0, c)),
                  pl.BlockSpec((HY_CONV, LANES), lambda c, b: (0, c)),
                  pl.BlockSpec((1, LANES), lambda c, b: (0, c)),
                  pl.BlockSpec((1, LANES), lambda c, b: (0, c)),
                  pl.BlockSpec((1, 1, K1_USED, 2, rows, LANES), lambda c, b: (order, c, 0, 0, 0, 0),
                               pipeline_mode=pl.Buffered(1)),
                  const(ta), const(tb), const(tcm), const(td)],
        out_specs=pl.BlockSpec((1, l_len, LANES), lambda c, b: (c, b, 0)),
        scratch_shapes=[pltpu.VMEM((l_len, LANES), BF16),
                        pltpu.VMEM((l_len, LANES), F32),
                        pltpu.VMEM((K1_PAD * _pitch_r(rows), LANES), U32),
                        pltpu.VMEM((rows * K1_PAD, LANES), U32)],
        compiler_params=_cparams("arbitrary", "arbitrary"),
        name="hyena%d" % order,
    )(sig_arr, gate_arr, conv_w_sig, conv_b_sig.reshape(1, w_hy), conv_w_gate, conv_b_gate.reshape(1, w_hy),
      d_vec.reshape(1, w_hy), kf, ta, tb, tcm, td)


def _outproj_kernel(yl_ref, hy_ref, x_ref, mod_ref, gl_ref, gh_ref, n2_ref, w_ref, rw2_ref, rb_ref,
                    hx_ref, m_ref, lg_ref, *, d, nb_l, nb_h, ne):
    yl = jnp.concatenate([yl_ref[j].astype(F32) for j in range(nb_l)], axis=-1)
    hy = jnp.concatenate([hy_ref[j].astype(F32) for j in range(nb_h)], axis=-1)
    cat = jnp.concatenate([_rms(yl) * gl_ref[...], _rms(hy) * gh_ref[...]], axis=-1).astype(BF16)
    out = jnp.dot(cat, w_ref[...], preferred_element_type=F32)
    g1 = mod_ref[0, :, 2 * d:3 * d]
    sh2 = mod_ref[0, :, 3 * d:4 * d]
    sc2 = mod_ref[0, :, 4 * d:5 * d]
    hx = x_ref[...] + g1 * out
    hx_ref[...] = hx
    m = (_rms(hx) * n2_ref[...]) * (1.0 + sc2) + sh2
    m_ref[...] = _pack_halves(m)
    m_hi = m.astype(BF16)
    m_lo = (m - m_hi.astype(F32)).astype(BF16)
    a = jnp.dot(m_hi, rw2_ref[...], preferred_element_type=F32)
    b = jnp.dot(m_lo, rw2_ref[:, :ne], preferred_element_type=F32)
    lg_ref[...] = (a[:, :ne] + b) + a[:, ne:] + rb_ref[...]


def _outproj_call(y_lru, hy, x2d, mod3, rows_per_batch, gn_lru, gn_hy, norm2_g, w_out_bf16, router_w, router_b, tm):
    t, d = x2d.shape
    nb_l, nb_h = y_lru.shape[0], hy.shape[0]
    ne = router_w.shape[1]
    bpb = rows_per_batch // tm
    rw_hi = router_w.astype(BF16)
    rw_lo = (router_w - rw_hi.astype(F32)).astype(BF16)
    rw2 = jnp.concatenate([rw_hi, rw_lo], axis=1)
    kern = functools.partial(_outproj_kernel, d=d, nb_l=nb_l, nb_h=nb_h, ne=ne)
    return pl.pallas_call(
        kern,
        out_shape=[jax.ShapeDtypeStruct((t, d), F32), jax.ShapeDtypeStruct((t, d // 2), U32),
                   jax.ShapeDtypeStruct((t, ne), F32)],
        grid=(t // tm,),
        in_specs=[pl.BlockSpec((nb_l, tm, LANES), lambda i: (0, i, 0)),
                  pl.BlockSpec((nb_h, tm, LANES), lambda i: (0, i, 0)),
                  pl.BlockSpec((tm, d), lambda i: (i, 0)),
                  pl.BlockSpec((1, 1, N_MOD * d), lambda i: (i // bpb, 0, 0)),
                  pl.BlockSpec((1, nb_l * LANES), lambda i: (0, 0)),
                  pl.BlockSpec((1, nb_h * LANES), lambda i: (0, 0)),
                  pl.BlockSpec((1, d), lambda i: (0, 0)),
                  pl.BlockSpec(w_out_bf16.shape, lambda i: (0, 0), pipeline_mode=pl.Buffered(1)),
                  pl.BlockSpec((d, 2 * ne), lambda i: (0, 0)),
                  pl.BlockSpec((1, ne), lambda i: (0, 0))],
        out_specs=[pl.BlockSpec((tm, d), lambda i: (i, 0)),
                   pl.BlockSpec((tm, d // 2), lambda i: (i, 0)),
                   pl.BlockSpec((tm, ne), lambda i: (i, 0))],
        compiler_params=_cparams("arbitrary"),
        name="outproj",
    )(y_lru, hy, x2d, mod3, gn_lru.reshape(1, -1), gn_hy.reshape(1, -1), norm2_g.reshape(1, d),
      w_out_bf16, rw2, router_b.reshape(1, ne))


def _route_kernel(lg_ref, idx_ref, gate_ref, rank_ref, cnt_ref, run_ref, *, tb, ne):
    @pl.when(pl.program_id(0) == 0)
    def _():
        run_ref[...] = jnp.zeros(run_ref.shape, F32)

    l = lg_ref[...]
    lane = lax.broadcasted_iota(I32, (tb, ne), 1)
    vals, idxs, ohs = [], [], []
    for _ in range(TOP_K):
        m = jnp.max(l, axis=-1, keepdims=True)
        ix = jnp.min(jnp.where(l == m, lane, ne), axis=-1, keepdims=True)
        sel = lane == ix
        vals.append(m)
        idxs.append(ix)
        ohs.append(sel.astype(F32))
        l = jnp.where(sel, -jnp.inf, l)
    es = [jnp.exp(v - vals[0]) for v in vals]
    den = es[0] + es[1] + es[2] + es[3]
    oh_all = ohs[0] + ohs[1] + ohs[2] + ohs[3]
    ri = lax.broadcasted_iota(I32, (tb, tb), 0)
    ci = lax.broadcasted_iota(I32, (tb, tb), 1)
    ltri = (ci < ri).astype(BF16)
    before = jnp.dot(ltri, oh_all.astype(BF16), preferred_element_type=F32) + run_ref[...]
    for k in range(TOP_K):
        idx_ref[:, k:k + 1] = idxs[k]
        gate_ref[:, k:k + 1] = es[k] / den
        rank_ref[:, k:k + 1] = jnp.sum(ohs[k] * before, axis=-1, keepdims=True).astype(I32)
    run_ref[...] += jnp.sum(oh_all, axis=0, keepdims=True)
    cnt_ref[...] = run_ref[...].astype(I32)


def _route_call(logits, tb):
    t, ne = logits.shape
    kern = functools.partial(_route_kernel, tb=tb, ne=ne)
    blk = lambda: pl.BlockSpec((tb, TOP_K), lambda i: (i, 0))
    return pl.pallas_call(
        kern,
        out_shape=[jax.ShapeDtypeStruct((t, TOP_K), I32), jax.ShapeDtypeStruct((t, TOP_K), F32),
                   jax.ShapeDtypeStruct((t, TOP_K), I32), jax.ShapeDtypeStruct((1, ne), I32)],
        grid=(t // tb,),
        in_specs=[pl.BlockSpec((tb, ne), lambda i: (i, 0))],
        out_specs=[blk(), blk(), blk(), pl.BlockSpec((1, ne), lambda i: (0, 0))],
        scratch_shapes=[pltpu.VMEM((1, ne), F32)],
        compiler_params=_cparams("arbitrary"),
        name="route",
    )(logits)


def _dispatch_kernel(dest_ref, pend_ref, pcnt_ref, nu_ref, m_ref, xs_hbm, zbuf, sem, zsem, *, tbd, tm, ne, nblk):
    i = pl.program_id(0)

    @pl.when(i == 0)
    def _():
        zbuf[...] = jnp.zeros(zbuf.shape, U32)

        def zero_block(row0):
            cp = pltpu.make_async_copy(zbuf, xs_hbm.at[pl.ds(pl.multiple_of(row0, tm), tm), :], zsem)
            cp.start()
            cp.wait()

        def zfill(e, c):
            @pl.when(pcnt_ref[e] > 0)
            def _():
                zero_block(pend_ref[e] - tm)
            return c
        lax.fori_loop(0, ne, zfill, 0)

        def ztail(bi, c):
            zero_block(bi * tm)
            return c
        lax.fori_loop(nu_ref[0], nblk, ztail, 0)

    def body(tt, c):
        tile = m_ref.at[pl.ds(pl.multiple_of(tt * SUBLANES, SUBLANES), SUBLANES), :]
        for s in range(SUBLANES):
            for k in range(TOP_K):
                d = dest_ref[(tt * SUBLANES + s) * TOP_K + k]
                pltpu.make_async_copy(tile.at[pl.ds(s, 1), :], xs_hbm.at[pl.ds(d, 1), :], sem).start(priority=k % 2)
        return c
    lax.fori_loop(0, tbd // SUBLANES, body, 0)
    for k in range(TOP_K):
        pltpu.make_async_copy(m_ref, xs_hbm.at[pl.ds(0, tbd), :], sem).wait()


def _dispatch_call(dest_flat, pend, pcnt, n_used, m, cap, tm, tbd):
    t, d = m.shape
    ne = pend.shape[0]
    kern = functools.partial(_dispatch_kernel, tbd=tbd, tm=tm, ne=ne, nblk=cap // tm)
    return pl.pallas_call(
        kern,
        out_shape=jax.ShapeDtypeStruct((cap, d), U32),
        grid=(t // tbd,),
        in_specs=[pl.BlockSpec((tbd * TOP_K,), lambda i: (i,), memory_space=pltpu.SMEM),
                  pl.BlockSpec(memory_space=pltpu.SMEM),
                  pl.BlockSpec(memory_space=pltpu.SMEM),
                  pl.BlockSpec(memory_space=pltpu.SMEM),
                  pl.BlockSpec((tbd, d), lambda i: (i, 0))],
        out_specs=pl.BlockSpec(memory_space=pl.ANY),
        scratch_shapes=[pltpu.VMEM((tm, d), U32), pltpu.SemaphoreType.DMA(()), pltpu.SemaphoreType.DMA(())],
        compiler_params=_cparams("arbitrary"),
        name="dispatch",
    )(dest_flat, pend, pcnt, n_used, m)


MOE_SUB = 4


def _row_blocks(nv, tm, compute, fill_zero):
    @pl.when(nv == tm)
    def _():
        compute(0, tm)

    tq = tm // MOE_SUB
    for q in range(MOE_SUB):
        @pl.when((nv < tm) & (nv > q * tq))
        def _():
            compute(q * tq, tq)

        @pl.when(nv <= q * tq)
        def _():
            fill_zero(q * tq, tq)


def _moe_up_kernel(be_ref, nu_ref, nv_ref, x_ref, wg_ref, wl_ref, bg_ref, bl_ref, a_ref, xb_ref):
    i = pl.program_id(0)
    j = pl.program_id(1)
    half = x_ref.shape[-1]
    tm = x_ref.shape[0]
    nv = nv_ref[i]

    @pl.when((nv > 0) & (j == 0))
    def _():
        hi, lo = _unpack_halves(x_ref[...])
        xb_ref[:, :half] = hi.astype(BF16)
        xb_ref[:, half:] = lo.astype(BF16)

    def compute(r0, nr):
        xb = xb_ref[r0:r0 + nr, :]
        glu = jnp.dot(xb, wg_ref[0].astype(BF16), preferred_element_type=F32) + bg_ref[0]
        lin = jnp.dot(xb, wl_ref[0].astype(BF16), preferred_element_type=F32) + bl_ref[0]
        glu = jnp.minimum(glu, SWIGLU_LIMIT)
        lin = jnp.clip(lin, -SWIGLU_LIMIT, SWIGLU_LIMIT)
        a_ref[r0:r0 + nr, :] = (glu * jax.nn.sigmoid(SWIGLU_ALPHA * glu) * (lin + 1.0)).astype(BF16)

    def fill_zero(r0, nr):
        a_ref[r0:r0 + nr, :] = jnp.zeros((nr, a_ref.shape[1]), BF16)

    _row_blocks(nv, tm, compute, fill_zero)


def _moe_down_kernel(be_ref, nu_ref, nv_ref, a_ref, wa_ref, wb_ref, ba_ref, bb_ref, y_ref):
    i = pl.program_id(1)
    tm = a_ref.shape[0]

    def compute(r0, nr):
        a = a_ref[r0:r0 + nr, :]
        ya = jnp.dot(a, wa_ref[0].astype(BF16), preferred_element_type=F32) + ba_ref[0]
        yb = jnp.dot(a, wb_ref[0].astype(BF16), preferred_element_type=F32) + bb_ref[0]
        y_ref[r0:r0 + nr, :] = _pack_pair(ya, yb)

    def fill_zero(r0, nr):
        y_ref[r0:r0 + nr, :] = jnp.zeros((nr, y_ref.shape[1]), U32)

    _row_blocks(nv_ref[i], tm, compute, fill_zero)


def _moe_call(block_expert, n_used, n_valid, xs, w_gu, b_gu, w_down, b_down, tm, tf, tn):
    cap, half = xs.shape
    d = 2 * half
    ne, _, two_ff = w_gu.shape
    d_ff = two_ff // 2
    nf = d_ff // tf
    nn = half // tn
    nblk = cap // tm
    assert tm % (MOE_SUB * 2 * SUBLANES) == 0

    def keep(i, j, nu, last):
        return jnp.where(i < nu[0], j, last)

    up_spec = pltpu.PrefetchScalarGridSpec(
        num_scalar_prefetch=3,
        grid=(nblk, nf),
        in_specs=[pl.BlockSpec((tm, half), lambda i, j, be, nu, nv: (i, 0)),
                  pl.BlockSpec((1, d, tf), lambda i, j, be, nu, nv: (be[i], 0, keep(i, j, nu, nf - 1))),
                  pl.BlockSpec((1, d, tf), lambda i, j, be, nu, nv: (be[i], 0, nf + keep(i, j, nu, nf - 1))),
                  pl.BlockSpec((1, 1, tf), lambda i, j, be, nu, nv: (be[i], 0, keep(i, j, nu, nf - 1))),
                  pl.BlockSpec((1, 1, tf), lambda i, j, be, nu, nv: (be[i], 0, nf + keep(i, j, nu, nf - 1)))],
        out_specs=pl.BlockSpec((tm, tf), lambda i, j, be, nu, nv: (i, j)),
        scratch_shapes=[pltpu.VMEM((tm, d), BF16)],
    )
    b_gu3 = b_gu.reshape(ne, 1, two_ff)
    act = pl.pallas_call(
        _moe_up_kernel,
        out_shape=jax.ShapeDtypeStruct((cap, d_ff), BF16),
        grid_spec=up_spec,
        compiler_params=_cparams("arbitrary", "arbitrary", vmem=VMEM_LIMIT_MOE_BYTES),
        name="moe_up",
    )(block_expert, n_used, n_valid, xs, w_gu, w_gu, b_gu3, b_gu3)

    down_spec = pltpu.PrefetchScalarGridSpec(
        num_scalar_prefetch=3,
        grid=(nn, nblk),
        in_specs=[pl.BlockSpec((tm, d_ff), lambda n, i, be, nu, nv: (i, 0)),
                  pl.BlockSpec((1, d_ff, tn), lambda n, i, be, nu, nv: (be[i], 0, n)),
                  pl.BlockSpec((1, d_ff, tn), lambda n, i, be, nu, nv: (be[i], 0, nn + n)),
                  pl.BlockSpec((1, 1, tn), lambda n, i, be, nu, nv: (be[i], 0, n)),
                  pl.BlockSpec((1, 1, tn), lambda n, i, be, nu, nv: (be[i], 0, nn + n))],
        out_specs=pl.BlockSpec((tm, tn), lambda n, i, be, nu, nv: (i, n)),
    )
    b_d3 = b_down.reshape(ne, 1, d)
    return pl.pallas_call(
        _moe_down_kernel,
        out_shape=jax.ShapeDtypeStruct((cap, half), U32),
        grid_spec=down_spec,
        compiler_params=_cparams("arbitrary", "arbitrary", vmem=VMEM_LIMIT_MOE_BYTES),
        name="moe_down",
    )(block_expert, n_used, n_valid, act, w_down, w_down, b_d3, b_d3)


def _combine_kernel(dest_ref, gate_ref, hx_ref, mod_ref, fg_ref, y_hbm, o_ref, buf, sem, *, tbc, d, nsteps):
    s = pl.program_id(0)
    slot_in = s % 2
    slot_out = (s + 1) % 2
    g2 = mod_ref[0, :, 5 * d:6 * d]
    fg = fg_ref[...]

    def issue(tt):
        r0 = pl.multiple_of(tt * SUBLANES, SUBLANES)
        for k in range(TOP_K):
            tile = buf.at[slot_in, k, pl.ds(r0, SUBLANES), :]
            for r in range(SUBLANES):
                dd = dest_ref[(tt * SUBLANES + r) * TOP_K + k]
                pltpu.make_async_copy(y_hbm.at[pl.ds(dd, 1), :], tile.at[pl.ds(r, 1), :],
                                      sem.at[slot_in]).start(priority=k % 2)

    def reduce_rows(r0, nr):
        gate = gate_ref[pl.ds(r0, nr), :]
        moe_a, moe_b = None, None
        for k in range(TOP_K):
            ya, yb = _unpack_halves(buf[slot_out, k, pl.ds(r0, nr), :])
            gk = gate[:, k:k + 1]
            moe_a = gk * ya if k == 0 else moe_a + gk * ya
            moe_b = gk * yb if k == 0 else moe_b + gk * yb
        moe = jnp.concatenate([moe_a, moe_b], axis=-1)
        hx = hx_ref[pl.ds(r0, nr), :] + g2 * moe
        o_ref[pl.ds(r0, nr), :] = _rms(hx) * fg

    def wait_prev():
        for k in range(TOP_K):
            pltpu.make_async_copy(y_hbm.at[pl.ds(0, tbc), :], buf.at[slot_out, k], sem.at[slot_out]).wait()

    @pl.when(s == 0)
    def _():
        def body(tt, c):
            issue(tt)
            return c
        lax.fori_loop(0, tbc // SUBLANES, body, 0)

    @pl.when((s > 0) & (s < nsteps))
    def _():
        wait_prev()

        def body(tt, c):
            issue(tt)
            reduce_rows(pl.multiple_of(tt * SUBLANES, SUBLANES), SUBLANES)
            return c
        lax.fori_loop(0, tbc // SUBLANES, body, 0, unroll=8)

    @pl.when(s == nsteps)
    def _():
        wait_prev()
        reduce_rows(0, tbc)


def _combine_call(dest_flat, gate, hx1, mod3, rows_per_batch, final_g, y, tbc):
    t, d = hx1.shape
    bpb = rows_per_batch // tbc
    nsteps = t // tbc
    kern = functools.partial(_combine_kernel, tbc=tbc, d=d, nsteps=nsteps)
    prev = lambda s: jnp.maximum(s - 1, 0)
    return pl.pallas_call(
        kern,
        out_shape=jax.ShapeDtypeStruct((t, d), F32),
        grid=(nsteps + 1,),
        in_specs=[pl.BlockSpec((tbc * TOP_K,), lambda s: (jnp.minimum(s, nsteps - 1),), memory_space=pltpu.SMEM),
                  pl.BlockSpec((tbc, TOP_K), lambda s: (prev(s), 0)),
                  pl.BlockSpec((tbc, d), lambda s: (prev(s), 0)),
                  pl.BlockSpec((1, 1, N_MOD * d), lambda s: (prev(s) // bpb, 0, 0)),
                  pl.BlockSpec((1, d), lambda s: (0, 0)),
                  pl.BlockSpec(memory_space=pl.ANY)],
        out_specs=pl.BlockSpec((tbc, d), lambda s: (prev(s), 0)),
        scratch_shapes=[pltpu.VMEM((2, TOP_K, tbc, d // 2), U32), pltpu.SemaphoreType.DMA((2,))],
        compiler_params=_cparams("arbitrary"),
        name="combine",
    )(dest_flat, gate, hx1, mod3, final_g.reshape(1, d), y)


def _tiles(t_tokens, seq):
    tok = 512 if seq % 512 == 0 else seq
    moe_tm = 1024 if t_tokens * TOP_K >= 32 * 1024 else 256
    return dict(tok=tok, route=tok, moe_tm=moe_tm,
                disp=tok, comb=tok)


def kernel(x, c, ctx, c_ctx, w_mod, b_mod, norm1_g, norm2_g, w_in, lru_conv_w, lru_conv_b, lru_wa, lru_ba,
           lru_wi, lru_bi, lru_lambda, hy_conv_w, hy_conv_b, hy_w1, hy_b1, hy_w2, hy_b2, hy_w3, hy_b3,
           hy_freq, hy_w4, hy_d, gn_lru, gn_hy, w_out, router_w, router_b, exp_w_gu, exp_b_gu,
           exp_w_down, exp_b_down, final_g):
    depth = w_mod.shape[0]
    assert depth == 1, "single-layer stack: the context stream only feeds the latent scan states"
    n_batch, seq, d = x.shape
    ctx_len = ctx.shape[1]
    w_lru = lru_conv_w.shape[-1]
    w_hy = hy_d.shape[-1]
    nb_l, nb_h = w_lru // LANES, w_hy // LANES
    assert lru_wa.shape[2] == nb_l and lru_wa.shape[3] == LANES, "gate blocks must be 128 wide"
    assert seq % (GRID_W * SUBLANES) == 0
    ne = router_w.shape[-1]
    t_tokens = n_batch * seq
    tl = _tiles(t_tokens, seq)
    l = 0

    n_rows = -(-(n_batch + 1) // SUBLANES) * SUBLANES
    cstack = jnp.concatenate([c, c_ctx[None, :], jnp.zeros((n_rows - n_batch - 1, d), F32)], axis=0)
    mod = _mod_call(cstack, w_mod[l], b_mod[l])
    mod3 = mod.reshape(n_rows, 1, N_MOD * d)

    w_in_b = w_in[l].astype(BF16)
    x2d = x.reshape(t_tokens, d)
    bpb = seq // tl["tok"]
    p_x = _inproj_call(x2d, mod3, lambda i: i // bpb, norm1_g[l], w_in_b, tl["tok"])
    ctx_tm = ctx_len if ctx_len <= 512 else 256
    p_c = _inproj_call(ctx.reshape(n_batch * ctx_len, d), mod3, lambda i: n_batch, norm1_g[l],
                       w_in_b[:, :2 * w_lru], ctx_tm)

    lru_args = (lru_conv_w[l], lru_conv_b[l], lru_wa[l], lru_ba[l], lru_wi[l], lru_bi[l], lru_lambda[l])
    _, h_ctx = _lru_call(p_c, nb_l, n_batch, ctx_len, *lru_args, jnp.zeros((n_batch, 2, w_lru), F32))
    y_lru, _ = _lru_call(p_x, nb_l, n_batch, seq, *lru_args, h_ctx)

    rows = seq // GRID_W
    tables = _dft_tables(rows)
    hfilt = _filt_call(seq, w_hy, hy_w1[l], hy_b1[l], hy_w2[l], hy_b2[l], hy_w3[l], hy_b3[l], hy_freq[l], hy_w4[l])
    kf = _spec_call(hfilt, seq, nb_h, tables[0], tables[1])
    cw, cb = hy_conv_w[l], hy_conv_b[l]
    s0 = 2 * nb_l
    z = _hyena_call(p_x, s0, p_x, s0 + nb_h, cw[:, :w_hy], cb[:w_hy], cw[:, w_hy:2 * w_hy], cb[w_hy:2 * w_hy],
                    hy_d[l, 0], kf, 0, tables, n_batch, seq, nb_h, True)
    hy = _hyena_call(z, 0, p_x, s0 + 2 * nb_h, cw[:, :w_hy], cb[:w_hy], cw[:, 2 * w_hy:], cb[2 * w_hy:],
                     hy_d[l, 1], kf, 1, tables, n_batch, seq, nb_h, False)

    hx1, m, logits = _outproj_call(y_lru, hy, x2d, mod3, seq, gn_lru[l], gn_hy[l], norm2_g[l],
                                   w_out[l].astype(BF16), router_w[l], router_b[l], tl["tok"])

    idx, gate, rank, counts = _route_call(logits, tl["route"])
    tm = tl["moe_tm"]
    counts = counts[0]
    pcnt = (counts + tm - 1) // tm * tm
    pend = jnp.cumsum(pcnt)
    pstart = pend - pcnt
    e_ids = jnp.arange(ne, dtype=I32)
    dest = (jnp.sum(jnp.where(idx[:, :, None] == e_ids, pstart.astype(I32), 0), axis=-1) + rank).reshape(-1)
    nblk = t_tokens * TOP_K // tm + ne
    cap = nblk * tm
    n_used = (pend[-1] // tm).astype(I32).reshape(1)
    blk_ids = jnp.arange(nblk, dtype=I32)
    block_expert = jnp.minimum(jnp.sum((pend[None, :] <= (blk_ids * tm)[:, None]).astype(I32), axis=1), ne - 1)
    block_expert = jnp.where(blk_ids < n_used[0], block_expert, block_expert[n_used[0] - 1])
    group_end = (pstart + counts).astype(I32)
    n_valid = jnp.where(blk_ids < n_used[0], jnp.clip(group_end[block_expert] - blk_ids * tm, 0, tm), 0).astype(I32)

    xs = _dispatch_call(dest, pend.astype(I32), pcnt.astype(I32), n_used, m, cap, tm, tl["disp"])
    d_ff = exp_w_down.shape[2]
    tf = next((c for c in (1024, 512, 256) if d_ff % c == 0), d_ff)
    tn = next((c for c in (1024, 512) if (d // 2) % c == 0), d // 2)
    y = _moe_call(block_expert, n_used, n_valid, xs, exp_w_gu[l], exp_b_gu[l], exp_w_down[l], exp_b_down[l],
                  tm, tf, tn)
    out = _combine_call(dest, gate, hx1, mod3, seq, final_g, y, tl["comb"])
    return out.reshape(n_batch, seq, d)
```

```python
import functools
import math

import jax
import jax.numpy as jnp
from jax import lax
from jax.experimental import pallas as pl
from jax.experimental.pallas import tpu as pltpu

F32 = jnp.float32
BF16 = jnp.bfloat16
I32 = jnp.int32
U32 = jnp.uint32
HI = lax.Precision.HIGHEST

GRID_W = 64
EPS = 1e-6
N_MOD = 6
LRU_CONV = 4
LRU_C = 8.0
HY_CONV = 3
HY_ORDER = 2
HY_BANDS = 16
HY_TARGET = 1e-2
HY_FAST = 0.3
HY_SLOW = 1.5
TOP_K = 4
SWIGLU_LIMIT = 7.0
SWIGLU_ALPHA = 1.702

LANES = 128
SUBLANES = 8
VMEM_LIMIT_BYTES = 56 * 1024 * 1024
VMEM_LIMIT_MOE_BYTES = 60 * 1024 * 1024

DFT_N1 = 2 * GRID_W
K1_USED = DFT_N1 // 2 + 1
K1_PAD = 72
assert K1_PAD % SUBLANES == 0 and K1_PAD >= K1_USED
LOOP_UNROLL = 32
K1_UNROLL = 16
assert (K1_USED - 1) % K1_UNROLL == 0


def _cparams(*sem, vmem=VMEM_LIMIT_BYTES):
    return pltpu.CompilerParams(dimension_semantics=sem, vmem_limit_bytes=vmem)


def _rms(x):
    return x * lax.rsqrt(jnp.mean(x * x, axis=-1, keepdims=True) + EPS)


def _pack_pair(a, b):
    wa = pltpu.bitcast(a.astype(BF16).astype(F32), U32)
    wb = pltpu.bitcast(b.astype(BF16).astype(F32), U32)
    return wa | (wb >> 16)


def _pack_halves(v):
    h = v.shape[-1] // 2
    return _pack_pair(v[:, :h], v[:, h:])


def _unpack_halves(w):
    return pltpu.bitcast(w & jnp.uint32(0xFFFF0000), F32), pltpu.bitcast(w << 16, F32)


def _mod_kernel(c_ref, w_ref, b_ref, o_ref):
    c = c_ref[...]
    s = c * jax.nn.sigmoid(c)
    o_ref[...] = jnp.dot(s, w_ref[...], preferred_element_type=F32, precision=HI) + b_ref[...]


def _mod_call(cstack, w_mod, b_mod):
    rows, d = cstack.shape
    n = w_mod.shape[1]
    tn = 1536 if n % 1536 == 0 else n
    return pl.pallas_call(
        _mod_kernel,
        out_shape=jax.ShapeDtypeStruct((rows, n), F32),
        grid=(n // tn,),
        in_specs=[pl.BlockSpec((rows, d), lambda j: (0, 0)),
                  pl.BlockSpec((d, tn), lambda j: (0, j)),
                  pl.BlockSpec((1, tn), lambda j: (0, j))],
        out_specs=pl.BlockSpec((rows, tn), lambda j: (0, j)),
        compiler_params=_cparams("arbitrary"),
        name="mod",
    )(cstack, w_mod, b_mod.reshape(1, n))


def _inproj_kernel(x_ref, mod_ref, g_ref, w_ref, o_ref, *, d, n_slabs, nchunk):
    x = x_ref[...]
    sh = mod_ref[0, :, 0 * d:1 * d]
    sc = mod_ref[0, :, 1 * d:2 * d]
    a = (_rms(x) * g_ref[...]) * (1.0 + sc) + sh
    ab = a.astype(BF16)
    spc = nchunk // LANES
    for j in range(n_slabs // spc):
        p = jnp.dot(ab, w_ref[:, j * nchunk:(j + 1) * nchunk], preferred_element_type=F32)
        for s in range(spc):
            o_ref[j * spc + s] = p[:, s * LANES:(s + 1) * LANES].astype(BF16)


def _inproj_call(x2d, mod3, mod_row_of_block, g, w_bf16, tm):
    t, d = x2d.shape
    n = w_bf16.shape[1]
    n_slabs = n // LANES
    nchunk = 512 if n % 512 == 0 else LANES
    kern = functools.partial(_inproj_kernel, d=d, n_slabs=n_slabs, nchunk=nchunk)
    return pl.pallas_call(
        kern,
        out_shape=jax.ShapeDtypeStruct((n_slabs, t, LANES), BF16),
        grid=(t // tm,),
        in_specs=[pl.BlockSpec((tm, d), lambda i: (i, 0)),
                  pl.BlockSpec((1, 1, N_MOD * d), lambda i: (mod_row_of_block(i), 0, 0)),
                  pl.BlockSpec((1, d), lambda i: (0, 0)),
                  pl.BlockSpec((d, n), lambda i: (0, 0), pipeline_mode=pl.Buffered(1))],
        out_specs=pl.BlockSpec((n_slabs, tm, LANES), lambda i: (0, i, 0)),
        compiler_params=_cparams("arbitrary"),
        name="inproj",
    )(x2d, mod3, g.reshape(1, d), w_bf16)


def _gelu_tanh(x):
    return 0.5 * x * (1.0 + jnp.tanh(math.sqrt(2.0 / math.pi) * (x + 0.044715 * (x * x * x))))


def _scan_tile(a, b, row, reverse):
    for s in (1, 2, 4):
        if reverse:
            m = row < SUBLANES - s
            sh = SUBLANES - s
        else:
            m = row >= s
            sh = s
        a_s = jnp.where(m, pltpu.roll(a, sh, 0), 1.0)
        b_s = jnp.where(m, pltpu.roll(b, sh, 0), 0.0)
        b = a * b_s + b
        a = a * a_s
    return a, b


def _lru_kernel(r_ref, g_ref, cw_ref, cb_ref, wa_ref, wi_ref, ba_ref, bi_ref, lam_ref, h0_ref,
                y_ref, hl_ref, xp_ref, hf_ref, ab_ref, bb_ref, *, t_len, tc):
    nc = t_len // tc
    ntile = tc // SUBLANES
    zero8 = jnp.zeros((SUBLANES, LANES), F32)
    xp_ref[0:SUBLANES, :] = zero8
    xp_ref[t_len + SUBLANES:t_len + 2 * SUBLANES, :] = zero8

    def copy_body(ci, c):
        t0 = pl.multiple_of(ci * tc, tc)
        xp_ref[pl.ds(t0 + SUBLANES, tc), :] = r_ref[0, pl.ds(t0, tc), :].astype(F32)
        return c
    lax.fori_loop(0, nc, copy_body, 0)

    row = lax.broadcasted_iota(I32, (SUBLANES, LANES), 0)
    cw = cw_ref[...]
    cb = cb_ref[...]
    lam = lam_ref[...]
    sp = jnp.log1p(jnp.exp(-lam))
    wab = [wa_ref[d, 0].astype(BF16) for d in range(2)]
    wib = [wi_ref[d, 0].astype(BF16) for d in range(2)]

    def coeffs(u, ub, d):
        ga = jnp.dot(ub, wab[d], preferred_element_type=F32) + ba_ref[d:d + 1, :]
        gi = jnp.dot(ub, wib[d], preferred_element_type=F32) + bi_ref[d:d + 1, :]
        rg = jax.nn.sigmoid(ga)
        ig = jax.nn.sigmoid(gi)
        a = jnp.exp((-LRU_C) * rg * sp[d:d + 1, :])
        b = jnp.sqrt(1.0 - a * a) * (ig * u)
        return a, b

    def fwd_body(ci, hcar):
        t0 = pl.multiple_of(ci * tc, tc)
        blk = xp_ref[pl.ds(t0, tc + 2 * SUBLANES), :]
        u = (cw[0:1, :] * blk[6:6 + tc] + cw[1:2, :] * blk[7:7 + tc]
             + cw[2:3, :] * blk[8:8 + tc] + cw[3:4, :] * blk[9:9 + tc]) + cb
        ub = u.astype(BF16)
        a0, b0 = coeffs(u, ub, 0)
        a1, b1 = coeffs(u, ub, 1)
        ab_ref[pl.ds(t0, tc), :] = a1
        bb_ref[pl.ds(t0, tc), :] = b1
        for j in range(ntile):
            at, bt = _scan_tile(a0[j * 8:(j + 1) * 8], b0[j * 8:(j + 1) * 8], row, False)
            hf_ref[pl.ds(t0 + j * 8, 8), :] = at * hcar + bt
            atot = jnp.broadcast_to(at[7:8, :], (SUBLANES, LANES))
            btot = jnp.broadcast_to(bt[7:8, :], (SUBLANES, LANES))
            hcar = atot * hcar + btot
        return hcar

    h0f = jnp.broadcast_to(h0_ref[0, 0:1, :], (SUBLANES, LANES))
    hfin = lax.fori_loop(0, nc, fwd_body, h0f, unroll=2 if nc % 2 == 0 else 1)
    hl_ref[0, 0:1, :] = hfin[0:1, :]

    def bwd_body(k, hcar):
        ci = nc - 1 - k
        t0 = pl.multiple_of(ci * tc, tc)
        a1 = ab_ref[pl.ds(t0, tc), :]
        b1 = bb_ref[pl.ds(t0, tc), :]
        gg = g_ref[0, pl.ds(t0, tc), :].astype(F32)
        gl = _gelu_tanh(gg)
        for j in reversed(range(ntile)):
            at, bt = _scan_tile(a1[j * 8:(j + 1) * 8], b1[j * 8:(j + 1) * 8], row, True)
            hb = at * hcar + bt
            hf = hf_ref[pl.ds(t0 + j * 8, 8), :]
            y_ref[0, pl.ds(t0 + j * 8, 8), :] = ((hf + hb) * gl[j * 8:(j + 1) * 8]).astype(y_ref.dtype)
            atot = jnp.broadcast_to(at[0:1, :], (SUBLANES, LANES))
            btot = jnp.broadcast_to(bt[0:1, :], (SUBLANES, LANES))
            hcar = atot * hcar + btot
        return hcar

    h0b = jnp.broadcast_to(h0_ref[0, 1:2, :], (SUBLANES, LANES))
    hfin_b = lax.fori_loop(0, nc, bwd_body, h0b, unroll=2 if nc % 2 == 0 else 1)
    hl_ref[0, 1:2, :] = hfin_b[0:1, :]


def _lru_call(p_slabs, nb, n_batch, t_len, conv_w, conv_b, wa, ba, wi, bi, lam, h0):
    tc = 128 if t_len % 128 == 0 else t_len
    kern = functools.partial(_lru_kernel, t_len=t_len, tc=tc)
    w_lru = nb * LANES
    return pl.pallas_call(
        kern,
        out_shape=[jax.ShapeDtypeStruct((nb, n_batch * t_len, LANES), F32),
                   jax.ShapeDtypeStruct((n_batch, 2, w_lru), F32)],
        grid=(nb, n_batch),
        in_specs=[pl.BlockSpec((1, t_len, LANES), lambda n, b: (n, b, 0)),
                  pl.BlockSpec((1, t_len, LANES), lambda n, b: (nb + n, b, 0)),
                  pl.BlockSpec((LRU_CONV, LANES), lambda n, b: (0, n)),
                  pl.BlockSpec((1, LANES), lambda n, b: (0, n)),
                  pl.BlockSpec((2, 1, LANES, LANES), lambda n, b: (0, n, 0, 0)),
                  pl.BlockSpec((2, 1, LANES, LANES), lambda n, b: (0, n, 0, 0)),
                  pl.BlockSpec((2, LANES), lambda n, b: (0, n)),
                  pl.BlockSpec((2, LANES), lambda n, b: (0, n)),
                  pl.BlockSpec((2, LANES), lambda n, b: (0, n)),
                  pl.BlockSpec((1, 2, LANES), lambda n, b: (b, 0, n))],
        out_specs=[pl.BlockSpec((1, t_len, LANES), lambda n, b: (n, b, 0)),
                   pl.BlockSpec((1, 2, LANES), lambda n, b: (b, 0, n))],
        scratch_shapes=[pltpu.VMEM((t_len + 2 * SUBLANES, LANES), F32),
                        pltpu.VMEM((t_len, LANES), F32),
                        pltpu.VMEM((t_len, LANES), F32),
                        pltpu.VMEM((t_len, LANES), F32)],
        compiler_params=_cparams("arbitrary", "arbitrary"),
        name="lru",
    )(p_slabs, p_slabs, conv_w, conv_b.reshape(1, w_lru), wa, wi, ba, bi, lam, h0)


def _filt_kernel(w1_ref, b1_ref, w2_ref, b2_ref, w3_ref, b3_ref, fr_ref, w4_ref, o_ref,
                 *, l_len, rows, tl, w_hy):
    i = pl.program_id(0)
    rho = i * tl + lax.broadcasted_iota(I32, (tl, 1), 0)
    tidx = ((rho % GRID_W) * rows + rho // GRID_W).astype(F32)
    tt = tidx * (1.0 / (l_len - 1))
    ww = (2.0 * math.pi) * tidx / l_len
    band = lax.broadcasted_iota(I32, (1, HY_BANDS), 1).astype(F32)
    f = 1e-4 + band * ((HY_BANDS - 1 - 1e-4) / (HY_BANDS - 1))
    fw = f * ww
    w1 = w1_ref[...]
    pre = (tt * w1[0:1, :]
           + jnp.dot(jnp.cos(fw), w1[1:1 + HY_BANDS, :], preferred_element_type=F32, precision=HI)
           + jnp.dot(-jnp.sin(fw), w1[1 + HY_BANDS:1 + 2 * HY_BANDS, :], preferred_element_type=F32, precision=HI))
    fr = fr_ref[...]
    h = jnp.sin(fr * (pre + b1_ref[...]))
    h = jnp.sin(fr * (jnp.dot(h, w2_ref[...], preferred_element_type=F32, precision=HI) + b2_ref[...]))
    h = jnp.sin(fr * (jnp.dot(h, w3_ref[...], preferred_element_type=F32, precision=HI) + b3_ref[...]))
    max_decay = math.log(HY_TARGET) / HY_FAST
    min_decay = math.log(HY_TARGET) / HY_SLOW
    nblk = w_hy // LANES
    h_hi = h.astype(BF16)
    h_lo = (h - h_hi.astype(F32)).astype(BF16)
    for cb in range(nblk):
        ch = (cb * LANES + lax.broadcasted_iota(I32, (1, LANES), 1)).astype(F32)
        delta = jnp.abs(min_decay + ch * ((max_decay - min_decay) / (w_hy - 1)))
        decay = jnp.exp(-tt * delta)
        for od in range(HY_ORDER * 2):
            col = od * w_hy + cb * LANES
            w4 = w4_ref[:, col:col + LANES]
            w_hi = w4.astype(BF16)
            w_lo = (w4 - w_hi.astype(F32)).astype(BF16)
            v = (jnp.dot(h_hi, w_hi, preferred_element_type=F32) + jnp.dot(h_lo, w_hi, preferred_element_type=F32)
                 + jnp.dot(h_hi, w_lo, preferred_element_type=F32))
            o_ref[od * nblk + cb] = v * decay


def _filt_call(l_len, w_hy, w1, b1, w2, b2, w3, b3, freq, w4):
    rows = l_len // GRID_W
    tl = 512 if l_len % 512 == 0 else l_len
    fh = w2.shape[0]
    nslab = HY_ORDER * 2 * (w_hy // LANES)
    kern = functools.partial(_filt_kernel, l_len=l_len, rows=rows, tl=tl, w_hy=w_hy)
    full = lambda a: pl.BlockSpec(a.shape, lambda i: (0,) * a.ndim)
    args = (w1, b1.reshape(1, fh), w2, b2.reshape(1, fh), w3, b3.reshape(1, fh), freq.reshape(1, fh), w4)
    return pl.pallas_call(
        kern,
        out_shape=jax.ShapeDtypeStruct((nslab, l_len, LANES), F32),
        grid=(l_len // tl,),
        in_specs=[full(a) for a in args],
        out_specs=pl.BlockSpec((nslab, tl, LANES), lambda i: (0, i, 0)),
        compiler_params=_cparams("arbitrary"),
        name="filt",
    )(*args)


def _dft_tables(rows):
    n_len = DFT_N1 * rows
    k1 = jnp.arange(K1_PAD, dtype=I32)
    r = jnp.arange(rows, dtype=I32)
    w = jnp.arange(GRID_W, dtype=I32)
    idx = (k1[None, :, None] * (r[:, None, None] + rows * w[None, None, :])) % n_len
    ang = idx.astype(F32) * (2.0 * math.pi / n_len)
    valid = (k1 < K1_USED)[None, :, None]
    ca = jnp.where(valid, jnp.cos(ang), 0.0)
    sa = jnp.where(valid, jnp.sin(ang), 0.0)
    ta = jnp.stack([ca, -sa], axis=2).reshape(rows, 2 * K1_PAD, GRID_W)
    td = jnp.transpose(ta, (0, 2, 1))
    k2 = jnp.arange(rows, dtype=I32)
    th = ((k2[:, None] * r[None, :]) % rows).astype(F32) * (2.0 * math.pi / rows)
    c, s = jnp.cos(th), jnp.sin(th)
    tb = jnp.concatenate([jnp.stack([c, s], axis=2).reshape(rows, 2 * rows),
                          jnp.stack([-s, c], axis=2).reshape(rows, 2 * rows)], axis=0)
    tcm = jnp.stack([jnp.concatenate([c, -s], axis=1), jnp.concatenate([s, c], axis=1)], axis=1).reshape(2 * rows, 2 * rows)
    return ta.astype(BF16), tb.astype(BF16), tcm.astype(BF16), td.astype(BF16)


def _pitch_r(rows):
    return rows + SUBLANES


def _stage_a(tile_fn, ta_ref, s1_ref, rows):
    pitch = _pitch_r(rows)

    def body(r, c):
        out = jnp.dot(ta_ref[r], tile_fn(r), preferred_element_type=F32)
        words = pltpu.bitcast(out.astype(BF16), U32)
        s1_ref[pl.ds(r, K1_PAD, stride=pitch), :] = words
        return c
    lax.fori_loop(0, rows, body, 0, unroll=min(LOOP_UNROLL, rows))


def _loop_k1(body):
    lax.fori_loop(0, K1_USED - 1, body, 0, unroll=K1_UNROLL)
    body(K1_USED - 1, 0)


def _stage_b(k1, tb_ref, s1_ref, rows):
    pitch = _pitch_r(rows)
    t = s1_ref[pl.ds(pl.multiple_of(k1 * pitch, SUBLANES), rows), :]
    y = jnp.dot(tb_ref[...], pltpu.bitcast(t, BF16), preferred_element_type=F32)
    return y[:rows], y[rows:]


def _spec_kernel(hf_ref, hb_ref, ta_ref, tb_ref, kf_ref, s1_ref, *, rows):
    n_len = DFT_N1 * rows

    def scale(k1):
        return jnp.where((k1 == 0) | (k1 == K1_USED - 1), 1.0, 2.0) * (1.0 / n_len)

    _stage_a(lambda r: hf_ref[0, pl.ds(pl.multiple_of(r * GRID_W, GRID_W), GRID_W), :].astype(BF16),
             ta_ref, s1_ref, rows)

    def b_fwd(k1, c):
        yre, yim = _stage_b(k1, tb_ref, s1_ref, rows)
        sc = scale(k1)
        kf_ref[0, 0, k1, 0] = yre * sc
        kf_ref[0, 0, k1, 1] = yim * sc
        return c
    _loop_k1(b_fwd)

    rid = lax.broadcasted_iota(I32, (GRID_W, LANES), 0)

    def tile_b(r):
        x = hb_ref[0, pl.ds(pl.multiple_of(r * GRID_W, GRID_W), GRID_W), :]
        x = jnp.where((rid == 0) & (r == 0), 0.0, x)
        return x.astype(BF16)
    _stage_a(tile_b, ta_ref, s1_ref, rows)

    def b_bwd(k1, c):
        yre, yim = _stage_b(k1, tb_ref, s1_ref, rows)
        sc = scale(k1)
        kf_ref[0, 0, k1, 0] += yre * sc
        kf_ref[0, 0, k1, 1] -= yim * sc
        return c
    _loop_k1(b_bwd)


def _spec_call(hfilt, l_len, nblk, ta, tb):
    rows = l_len // GRID_W
    kern = functools.partial(_spec_kernel, rows=rows)
    return pl.pallas_call(
        kern,
        out_shape=jax.ShapeDtypeStruct((HY_ORDER, nblk, K1_USED, 2, rows, LANES), F32),
        grid=(HY_ORDER, nblk),
        in_specs=[pl.BlockSpec((1, l_len, LANES), lambda o, c: (o * 2 * nblk + c, 0, 0)),
                  pl.BlockSpec((1, l_len, LANES), lambda o, c: (o * 2 * nblk + nblk + c, 0, 0)),
                  pl.BlockSpec(ta.shape, lambda o, c: (0, 0, 0), pipeline_mode=pl.Buffered(1)),
                  pl.BlockSpec(tb.shape, lambda o, c: (0, 0), pipeline_mode=pl.Buffered(1))],
        out_specs=pl.BlockSpec((1, 1, K1_USED, 2, rows, LANES), lambda o, c: (o, c, 0, 0, 0, 0)),
        scratch_shapes=[pltpu.VMEM((K1_PAD * _pitch_r(rows), LANES), U32)],
        compiler_params=_cparams("arbitrary", "arbitrary"),
        name="spec",
    )(hfilt, hfilt, ta, tb)


def _conv_grid_order(src_ref, dst_ref, w_ref, b_ref, rows):
    w0, w1, w2, bb = w_ref[0:1, :], w_ref[1:2, :], w_ref[2:3, :], b_ref[...]
    g = GRID_W
    rid = lax.broadcasted_iota(I32, (g, LANES), 0)

    def grp(r0):
        return src_ref[0, pl.ds(r0, g), :].astype(F32)

    last = grp((rows - 1) * g)
    prev0 = jnp.where(rid >= 1, pltpu.roll(last, 1, 0), 0.0)
    dst_ref[pl.ds(0, g), :] = (w0 * prev0 + w1 * grp(0) + w2 * grp(g) + bb).astype(dst_ref.dtype)
    first = grp(0)
    nxt = jnp.where(rid < g - 1, pltpu.roll(first, g - 1, 0), 0.0)
    dst_ref[pl.ds((rows - 1) * g, g), :] = (w0 * grp((rows - 2) * g) + w1 * last + w2 * nxt + bb).astype(dst_ref.dtype)

    def body(r, c):
        rm = pl.multiple_of((r - 1) * g, g)
        r0 = pl.multiple_of(r * g, g)
        rp = pl.multiple_of((r + 1) * g, g)
        v = w0 * grp(rm) + w1 * grp(r0) + w2 * grp(rp) + bb
        dst_ref[pl.ds(r0, g), :] = v.astype(dst_ref.dtype)
        return c
    lax.fori_loop(1, rows - 1, body, 0, unroll=2)


def _hyena_kernel(sig_ref, gate_ref, cws_ref, cbs_ref, cwg_ref, cbg_ref, d_ref, kf_ref,
                  ta_ref, tb_ref, tc_ref, td_ref, o_ref, u_ref, gt_ref, s1_ref, s2_ref,
                  *, rows, conv_sig):
    g = GRID_W

    @pl.when((pl.program_id(0) == 0) & (pl.program_id(1) == 0))
    def _():
        s2_ref[...] = jnp.zeros(s2_ref.shape, U32)

    if conv_sig:
        _conv_grid_order(sig_ref, u_ref, cws_ref, cbs_ref, rows)
    else:
        def cp(r, c):
            r0 = pl.multiple_of(r * g, g)
            u_ref[pl.ds(r0, g), :] = sig_ref[0, pl.ds(r0, g), :]
            return c
        lax.fori_loop(0, rows, cp, 0, unroll=min(LOOP_UNROLL, rows))
    _conv_grid_order(gate_ref, gt_ref, cwg_ref, cbg_ref, rows)

    _stage_a(lambda r: u_ref[pl.ds(pl.multiple_of(r * g, g), g), :], ta_ref, s1_ref, rows)

    def bc_body(k1, c):
        yre, yim = _stage_b(k1, tb_ref, s1_ref, rows)
        kre = kf_ref[0, 0, k1, 0]
        kim = kf_ref[0, 0, k1, 1]
        zre = yre * kre - yim * kim
        zim = yre * kim + yim * kre
        z = jnp.concatenate([zre, zim], axis=0).astype(BF16)
        v = jnp.dot(tc_ref[...], z, preferred_element_type=F32)
        s2_ref[pl.ds(k1, rows, stride=K1_PAD), :] = pltpu.bitcast(v.astype(BF16), U32)
        return c
    _loop_k1(bc_body)

    dd = d_ref[...]

    def d_body(r, c):
        r0 = pl.multiple_of(r * g, g)
        t = s2_ref[pl.ds(pl.multiple_of(r * K1_PAD, SUBLANES), K1_PAD), :]
        y = jnp.dot(td_ref[r], pltpu.bitcast(t, BF16), preferred_element_type=F32)
        uu = u_ref[pl.ds(r0, g), :].astype(F32)
        o_ref[0, pl.ds(r0, g), :] = (gt_ref[pl.ds(r0, g), :] * (y + dd * uu)).astype(o_ref.dtype)
        return c
    lax.fori_loop(0, rows, d_body, 0, unroll=min(LOOP_UNROLL, rows))


def _hyena_call(sig_arr, sig_slab0, gate_arr, gate_slab0, conv_w_sig, conv_b_sig, conv_w_gate, conv_b_gate,
                d_vec, kf, order, tables, n_batch, l_len, nblk, conv_sig):
    rows = l_len // GRID_W
    ta, tb, tcm, td = tables
    kern = functools.partial(_hyena_kernel, rows=rows, conv_sig=conv_sig)
    w_hy = nblk * LANES
    const = lambda a: pl.BlockSpec(a.shape, lambda c, b: (0,) * a.ndim, pipeline_mode=pl.Buffered(1))
    return pl.pallas_call(
        kern,
        out_shape=jax.ShapeDtypeStruct((nblk, n_batch * l_len, LANES), BF16),
        grid=(nblk, n_batch),
        in_specs=[pl.BlockSpec((1, l_len, LANES), lambda c, b: (sig_slab0 + c, b, 0)),
                  pl.BlockSpec((1, l_len, LANES), lambda c, b: (gate_slab0 + c, b, 0)),
                  pl.BlockSpec((HY_CONV, LANES), lambda c, b: (0, c)),
                  pl.BlockSpec((1, LANES), lambda c, b: (0, c)),
                  pl.BlockSpec((HY_CONV, LANES), lambda c, b: (0, c)),
                  pl.BlockSpec((1, LANES), lambda c, b: (0, c)),
                  pl.BlockSpec((1, LANES), lambda c, b: (0, c)),
                  pl.BlockSpec((1, 1, K1_USED, 2, rows, LANES), lambda c, b: (order, c, 0, 0, 0, 0),
                               pipeline_mode=pl.Buffered(1)),
                  const(ta), const(tb), const(tcm), const(td)],
        out_specs=pl.BlockSpec((1, l_len, LANES), lambda c, b: (c, b, 0)),
        scratch_shapes=[pltpu.VMEM((l_len, LANES), BF16),
                        pltpu.VMEM((l_len, LANES), F32),
                        pltpu.VMEM((K1_PAD * _pitch_r(rows), LANES), U32),
                        pltpu.VMEM((rows * K1_PAD, LANES), U32)],
        compiler_params=_cparams("arbitrary", "arbitrary"),
        name="hyena%d" % order,
    )(sig_arr, gate_arr, conv_w_sig, conv_b_sig.reshape(1, w_hy), conv_w_gate, conv_b_gate.reshape(1, w_hy),
      d_vec.reshape(1, w_hy), kf, ta, tb, tcm, td)


def _outproj_kernel(yl_ref, hy_ref, x_ref, mod_ref, gl_ref, gh_ref, n2_ref, w_ref, rw2_ref, rb_ref,
                    hx_ref, m_ref, lg_ref, *, d, nb_l, nb_h, ne):
    yl = jnp.concatenate([yl_ref[j].astype(F32) for j in range(nb_l)], axis=-1)
    hy = jnp.concatenate([hy_ref[j].astype(F32) for j in range(nb_h)], axis=-1)
    cat = jnp.concatenate([_rms(yl) * gl_ref[...], _rms(hy) * gh_ref[...]], axis=-1).astype(BF16)
    out = jnp.dot(cat, w_ref[...], preferred_element_type=F32)
    g1 = mod_ref[0, :, 2 * d:3 * d]
    sh2 = mod_ref[0, :, 3 * d:4 * d]
    sc2 = mod_ref[0, :, 4 * d:5 * d]
    hx = x_ref[...] + g1 * out
    hx_ref[...] = hx
    m = (_rms(hx) * n2_ref[...]) * (1.0 + sc2) + sh2
    m_ref[...] = _pack_halves(m)
    m_hi = m.astype(BF16)
    m_lo = (m - m_hi.astype(F32)).astype(BF16)
    a = jnp.dot(m_hi, rw2_ref[...], preferred_element_type=F32)
    b = jnp.dot(m_lo, rw2_ref[:, :ne], preferred_element_type=F32)
    lg_ref[...] = (a[:, :ne] + b) + a[:, ne:] + rb_ref[...]


def _outproj_call(y_lru, hy, x2d, mod3, rows_per_batch, gn_lru, gn_hy, norm2_g, w_out_bf16, router_w, router_b, tm):
    t, d = x2d.shape
    nb_l, nb_h = y_lru.shape[0], hy.shape[0]
    ne = router_w.shape[1]
    bpb = rows_per_batch // tm
    rw_hi = router_w.astype(BF16)
    rw_lo = (router_w - rw_hi.astype(F32)).astype(BF16)
    rw2 = jnp.concatenate([rw_hi, rw_lo], axis=1)
    kern = functools.partial(_outproj_kernel, d=d, nb_l=nb_l, nb_h=nb_h, ne=ne)
    return pl.pallas_call(
        kern,
        out_shape=[jax.ShapeDtypeStruct((t, d), F32), jax.ShapeDtypeStruct((t, d // 2), U32),
                   jax.ShapeDtypeStruct((t, ne), F32)],
        grid=(t // tm,),
        in_specs=[pl.BlockSpec((nb_l, tm, LANES), lambda i: (0, i, 0)),
                  pl.BlockSpec((nb_h, tm, LANES), lambda i: (0, i, 0)),
                  pl.BlockSpec((tm, d), lambda i: (i, 0)),
                  pl.BlockSpec((1, 1, N_MOD * d), lambda i: (i // bpb, 0, 0)),
                  pl.BlockSpec((1, nb_l * LANES), lambda i: (0, 0)),
                  pl.BlockSpec((1, nb_h * LANES), lambda i: (0, 0)),
                  pl.BlockSpec((1, d), lambda i: (0, 0)),
                  pl.BlockSpec(w_out_bf16.shape, lambda i: (0, 0), pipeline_mode=pl.Buffered(1)),
                  pl.BlockSpec((d, 2 * ne), lambda i: (0, 0)),
                  pl.BlockSpec((1, ne), lambda i: (0, 0))],
        out_specs=[pl.BlockSpec((tm, d), lambda i: (i, 0)),
                   pl.BlockSpec((tm, d // 2), lambda i: (i, 0)),
                   pl.BlockSpec((tm, ne), lambda i: (i, 0))],
        compiler_params=_cparams("arbitrary"),
        name="outproj",
    )(y_lru, hy, x2d, mod3, gn_lru.reshape(1, -1), gn_hy.reshape(1, -1), norm2_g.reshape(1, d),
      w_out_bf16, rw2, router_b.reshape(1, ne))


def _route_kernel(lg_ref, idx_ref, gate_ref, rank_ref, cnt_ref, run_ref, *, tb, ne):
    @pl.when(pl.program_id(0) == 0)
    def _():
        run_ref[...] = jnp.zeros(run_ref.shape, F32)

    l = lg_ref[...]
    lane = lax.broadcasted_iota(I32, (tb, ne), 1)
    vals, idxs, ohs = [], [], []
    for _ in range(TOP_K):
        m = jnp.max(l, axis=-1, keepdims=True)
        ix = jnp.min(jnp.where(l == m, lane, ne), axis=-1, keepdims=True)
        sel = lane == ix
        vals.append(m)
        idxs.append(ix)
        ohs.append(sel.astype(F32))
        l = jnp.where(sel, -jnp.inf, l)
    es = [jnp.exp(v - vals[0]) for v in vals]
    den = es[0] + es[1] + es[2] + es[3]
    oh_all = ohs[0] + ohs[1] + ohs[2] + ohs[3]
    ri = lax.broadcasted_iota(I32, (tb, tb), 0)
    ci = lax.broadcasted_iota(I32, (tb, tb), 1)
    ltri = (ci < ri).astype(BF16)
    before = jnp.dot(ltri, oh_all.astype(BF16), preferred_element_type=F32) + run_ref[...]
    for k in range(TOP_K):
        idx_ref[:, k:k + 1] = idxs[k]
        gate_ref[:, k:k + 1] = es[k] / den
        rank_ref[:, k:k + 1] = jnp.sum(ohs[k] * before, axis=-1, keepdims=True).astype(I32)
    run_ref[...] += jnp.sum(oh_all, axis=0, keepdims=True)
    cnt_ref[...] = run_ref[...].astype(I32)


def _route_call(logits, tb):
    t, ne = logits.shape
    kern = functools.partial(_route_kernel, tb=tb, ne=ne)
    blk = lambda: pl.BlockSpec((tb, TOP_K), lambda i: (i, 0))
    return pl.pallas_call(
        kern,
        out_shape=[jax.ShapeDtypeStruct((t, TOP_K), I32), jax.ShapeDtypeStruct((t, TOP_K), F32),
                   jax.ShapeDtypeStruct((t, TOP_K), I32), jax.ShapeDtypeStruct((1, ne), I32)],
        grid=(t // tb,),
        in_specs=[pl.BlockSpec((tb, ne), lambda i: (i, 0))],
        out_specs=[blk(), blk(), blk(), pl.BlockSpec((1, ne), lambda i: (0, 0))],
        scratch_shapes=[pltpu.VMEM((1, ne), F32)],
        compiler_params=_cparams("arbitrary"),
        name="route",
    )(logits)


def _dispatch_kernel(dest_ref, pend_ref, pcnt_ref, nu_ref, m_ref, xs_hbm, zbuf, sem, zsem, *, tbd, tm, ne, nblk):
    i = pl.program_id(0)

    @pl.when(i == 0)
    def _():
        zbuf[...] = jnp.zeros(zbuf.shape, U32)

        def zero_block(row0):
            cp = pltpu.make_async_copy(zbuf, xs_hbm.at[pl.ds(pl.multiple_of(row0, tm), tm), :], zsem)
            cp.start()
            cp.wait()

        def zfill(e, c):
            @pl.when(pcnt_ref[e] > 0)
            def _():
                zero_block(pend_ref[e] - tm)
            return c
        lax.fori_loop(0, ne, zfill, 0)

        def ztail(bi, c):
            zero_block(bi * tm)
            return c
        lax.fori_loop(nu_ref[0], nblk, ztail, 0)

    def body(tt, c):
        tile = m_ref.at[pl.ds(pl.multiple_of(tt * SUBLANES, SUBLANES), SUBLANES), :]
        for s in range(SUBLANES):
            for k in range(TOP_K):
                d = dest_ref[(tt * SUBLANES + s) * TOP_K + k]
                pltpu.make_async_copy(tile.at[pl.ds(s, 1), :], xs_hbm.at[pl.ds(d, 1), :], sem).start(priority=k % 2)
        return c
    lax.fori_loop(0, tbd // SUBLANES, body, 0)
    for k in range(TOP_K):
        pltpu.make_async_copy(m_ref, xs_hbm.at[pl.ds(0, tbd), :], sem).wait()


def _dispatch_call(dest_flat, pend, pcnt, n_used, m, cap, tm, tbd):
    t, d = m.shape
    ne = pend.shape[0]
    kern = functools.partial(_dispatch_kernel, tbd=tbd, tm=tm, ne=ne, nblk=cap // tm)
    return pl.pallas_call(
        kern,
        out_shape=jax.ShapeDtypeStruct((cap, d), U32),
        grid=(t // tbd,),
        in_specs=[pl.BlockSpec((tbd * TOP_K,), lambda i: (i,), memory_space=pltpu.SMEM),
                  pl.BlockSpec(memory_space=pltpu.SMEM),
                  pl.BlockSpec(memory_space=pltpu.SMEM),
                  pl.BlockSpec(memory_space=pltpu.SMEM),
                  pl.BlockSpec((tbd, d), lambda i: (i, 0))],
        out_specs=pl.BlockSpec(memory_space=pl.ANY),
        scratch_shapes=[pltpu.VMEM((tm, d), U32), pltpu.SemaphoreType.DMA(()), pltpu.SemaphoreType.DMA(())],
        compiler_params=_cparams("arbitrary"),
        name="dispatch",
    )(dest_flat, pend, pcnt, n_used, m)


MOE_SUB = 4


def _row_blocks(nv, tm, compute, fill_zero):
    @pl.when(nv == tm)
    def _():
        compute(0, tm)

    tq = tm // MOE_SUB
    for q in range(MOE_SUB):
        @pl.when((nv < tm) & (nv > q * tq))
        def _():
            compute(q * tq, tq)

        @pl.when(nv <= q * tq)
        def _():
            fill_zero(q * tq, tq)


def _moe_up_kernel(be_ref, nu_ref, nv_ref, x_ref, wg_ref, wl_ref, bg_ref, bl_ref, a_ref, xb_ref):
    i = pl.program_id(0)
    j = pl.program_id(1)
    half = x_ref.shape[-1]
    tm = x_ref.shape[0]
    nv = nv_ref[i]

    @pl.when((nv > 0) & (j == 0))
    def _():
        hi, lo = _unpack_halves(x_ref[...])
        xb_ref[:, :half] = hi.astype(BF16)
        xb_ref[:, half:] = lo.astype(BF16)

    def compute(r0, nr):
        xb = xb_ref[r0:r0 + nr, :]
        glu = jnp.dot(xb, wg_ref[0].astype(BF16), preferred_element_type=F32) + bg_ref[0]
        lin = jnp.dot(xb, wl_ref[0].astype(BF16), preferred_element_type=F32) + bl_ref[0]
        glu = jnp.minimum(glu, SWIGLU_LIMIT)
        lin = jnp.clip(lin, -SWIGLU_LIMIT, SWIGLU_LIMIT)
        a_ref[r0:r0 + nr, :] = (glu * jax.nn.sigmoid(SWIGLU_ALPHA * glu) * (lin + 1.0)).astype(BF16)

    def fill_zero(r0, nr):
        a_ref[r0:r0 + nr, :] = jnp.zeros((nr, a_ref.shape[1]), BF16)

    _row_blocks(nv, tm, compute, fill_zero)


def _moe_down_kernel(be_ref, nu_ref, nv_ref, a_ref, wa_ref, wb_ref, ba_ref, bb_ref, y_ref):
    i = pl.program_id(1)
    tm = a_ref.shape[0]

    def compute(r0, nr):
        a = a_ref[r0:r0 + nr, :]
        ya = jnp.dot(a, wa_ref[0].astype(BF16), preferred_element_type=F32) + ba_ref[0]
        yb = jnp.dot(a, wb_ref[0].astype(BF16), preferred_element_type=F32) + bb_ref[0]
        y_ref[r0:r0 + nr, :] = _pack_pair(ya, yb)

    def fill_zero(r0, nr):
        y_ref[r0:r0 + nr, :] = jnp.zeros((nr, y_ref.shape[1]), U32)

    _row_blocks(nv_ref[i], tm, compute, fill_zero)


def _moe_call(block_expert, n_used, n_valid, xs, w_gu, b_gu, w_down, b_down, tm, tf, tn):
    cap, half = xs.shape
    d = 2 * half
    ne, _, two_ff = w_gu.shape
    d_ff = two_ff // 2
    nf = d_ff // tf
    nn = half // tn
    nblk = cap // tm
    assert tm % (MOE_SUB * 2 * SUBLANES) == 0

    def keep(i, j, nu, last):
        return jnp.where(i < nu[0], j, last)

    up_spec = pltpu.PrefetchScalarGridSpec(
        num_scalar_prefetch=3,
        grid=(nblk, nf),
        in_specs=[pl.BlockSpec((tm, half), lambda i, j, be, nu, nv: (i, 0)),
                  pl.BlockSpec((1, d, tf), lambda i, j, be, nu, nv: (be[i], 0, keep(i, j, nu, nf - 1))),
                  pl.BlockSpec((1, d, tf), lambda i, j, be, nu, nv: (be[i], 0, nf + keep(i, j, nu, nf - 1))),
                  pl.BlockSpec((1, 1, tf), lambda i, j, be, nu, nv: (be[i], 0, keep(i, j, nu, nf - 1))),
                  pl.BlockSpec((1, 1, tf), lambda i, j, be, nu, nv: (be[i], 0, nf + keep(i, j, nu, nf - 1)))],
        out_specs=pl.BlockSpec((tm, tf), lambda i, j, be, nu, nv: (i, j)),
        scratch_shapes=[pltpu.VMEM((tm, d), BF16)],
    )
    b_gu3 = b_gu.reshape(ne, 1, two_ff)
    act = pl.pallas_call(
        _moe_up_kernel,
        out_shape=jax.ShapeDtypeStruct((cap, d_ff), BF16),
        grid_spec=up_spec,
        compiler_params=_cparams("arbitrary", "arbitrary", vmem=VMEM_LIMIT_MOE_BYTES),
        name="moe_up",
    )(block_expert, n_used, n_valid, xs, w_gu, w_gu, b_gu3, b_gu3)

    down_spec = pltpu.PrefetchScalarGridSpec(
        num_scalar_prefetch=3,
        grid=(nn, nblk),
        in_specs=[pl.BlockSpec((tm, d_ff), lambda n, i, be, nu, nv: (i, 0)),
                  pl.BlockSpec((1, d_ff, tn), lambda n, i, be, nu, nv: (be[i], 0, n)),
                  pl.BlockSpec((1, d_ff, tn), lambda n, i, be, nu, nv: (be[i], 0, nn + n)),
                  pl.BlockSpec((1, 1, tn), lambda n, i, be, nu, nv: (be[i], 0, n)),
                  pl.BlockSpec((1, 1, tn), lambda n, i, be, nu, nv: (be[i], 0, nn + n))],
        out_specs=pl.BlockSpec((tm, tn), lambda n, i, be, nu, nv: (i, n)),
    )
    b_d3 = b_down.reshape(ne, 1, d)
    return pl.pallas_call(
        _moe_down_kernel,
        out_shape=jax.ShapeDtypeStruct((cap, half), U32),
        grid_spec=down_spec,
        compiler_params=_cparams("arbitrary", "arbitrary", vmem=VMEM_LIMIT_MOE_BYTES),
        name="moe_down",
    )(block_expert, n_used, n_valid, act, w_down, w_down, b_d3, b_d3)


def _combine_kernel(dest_ref, gate_ref, hx_ref, mod_ref, fg_ref, y_hbm, o_ref, buf, sem, *, tbc, d, nsteps):
    s = pl.program_id(0)
    slot_in = s % 2
    slot_out = (s + 1) % 2
    g2 = mod_ref[0, :, 5 * d:6 * d]
    fg = fg_ref[...]

    def issue(tt):
        r0 = pl.multiple_of(tt * SUBLANES, SUBLANES)
        for k in range(TOP_K):
            tile = buf.at[slot_in, k, pl.ds(r0, SUBLANES), :]
            for r in range(SUBLANES):
                dd = dest_ref[(tt * SUBLANES + r) * TOP_K + k]
                pltpu.make_async_copy(y_hbm.at[pl.ds(dd, 1), :], tile.at[pl.ds(r, 1), :],
                                      sem.at[slot_in]).start(priority=k % 2)

    def reduce_rows(r0, nr):
        gate = gate_ref[pl.ds(r0, nr), :]
        moe_a, moe_b = None, None
        for k in range(TOP_K):
            ya, yb = _unpack_halves(buf[slot_out, k, pl.ds(r0, nr), :])
            gk = gate[:, k:k + 1]
            moe_a = gk * ya if k == 0 else moe_a + gk * ya
            moe_b = gk * yb if k == 0 else moe_b + gk * yb
        moe = jnp.concatenate([moe_a, moe_b], axis=-1)
        hx = hx_ref[pl.ds(r0, nr), :] + g2 * moe
        o_ref[pl.ds(r0, nr), :] = _rms(hx) * fg

    def wait_prev():
        for k in range(TOP_K):
            pltpu.make_async_copy(y_hbm.at[pl.ds(0, tbc), :], buf.at[slot_out, k], sem.at[slot_out]).wait()

    @pl.when(s == 0)
    def _():
        def body(tt, c):
            issue(tt)
            return c
        lax.fori_loop(0, tbc // SUBLANES, body, 0)

    @pl.when((s > 0) & (s < nsteps))
    def _():
        wait_prev()

        def body(tt, c):
            issue(tt)
            reduce_rows(pl.multiple_of(tt * SUBLANES, SUBLANES), SUBLANES)
            return c
        lax.fori_loop(0, tbc // SUBLANES, body, 0, unroll=8)

    @pl.when(s == nsteps)
    def _():
        wait_prev()
        reduce_rows(0, tbc)


def _combine_call(dest_flat, gate, hx1, mod3, rows_per_batch, final_g, y, tbc):
    t, d = hx1.shape
    bpb = rows_per_batch // tbc
    nsteps = t // tbc
    kern = functools.partial(_combine_kernel, tbc=tbc, d=d, nsteps=nsteps)
    prev = lambda s: jnp.maximum(s - 1, 0)
    return pl.pallas_call(
        kern,
        out_shape=jax.ShapeDtypeStruct((t, d), F32),
        grid=(nsteps + 1,),
        in_specs=[pl.BlockSpec((tbc * TOP_K,), lambda s: (jnp.minimum(s, nsteps - 1),), memory_space=pltpu.SMEM),
                  pl.BlockSpec((tbc, TOP_K), lambda s: (prev(s), 0)),
                  pl.BlockSpec((tbc, d), lambda s: (prev(s), 0)),
                  pl.BlockSpec((1, 1, N_MOD * d), lambda s: (prev(s) // bpb, 0, 0)),
                  pl.BlockSpec((1, d), lambda s: (0, 0)),
                  pl.BlockSpec(memory_space=pl.ANY)],
        out_specs=pl.BlockSpec((tbc, d), lambda s: (prev(s), 0)),
        scratch_shapes=[pltpu.VMEM((2, TOP_K, tbc, d // 2), U32), pltpu.SemaphoreType.DMA((2,))],
        compiler_params=_cparams("arbitrary"),
        name="combine",
    )(dest_flat, gate, hx1, mod3, final_g.reshape(1, d), y)


def _tiles(t_tokens, seq):
    tok = 512 if seq % 512 == 0 else seq
    moe_tm = 1024 if t_tokens * TOP_K >= 32 * 1024 else 256
    big = 1024 if seq % 1024 == 0 else tok
    return dict(tok=tok, route=big, moe_tm=moe_tm, disp=big, comb=tok)


def kernel(x, c, ctx, c_ctx, w_mod, b_mod, norm1_g, norm2_g, w_in, lru_conv_w, lru_conv_b, lru_wa, lru_ba,
           lru_wi, lru_bi, lru_lambda, hy_conv_w, hy_conv_b, hy_w1, hy_b1, hy_w2, hy_b2, hy_w3, hy_b3,
           hy_freq, hy_w4, hy_d, gn_lru, gn_hy, w_out, router_w, router_b, exp_w_gu, exp_b_gu,
           exp_w_down, exp_b_down, final_g):
    depth = w_mod.shape[0]
    assert depth == 1, "single-layer stack: the context stream only feeds the latent scan states"
    n_batch, seq, d = x.shape
    ctx_len = ctx.shape[1]
    w_lru = lru_conv_w.shape[-1]
    w_hy = hy_d.shape[-1]
    nb_l, nb_h = w_lru // LANES, w_hy // LANES
    assert lru_wa.shape[2] == nb_l and lru_wa.shape[3] == LANES, "gate blocks must be 128 wide"
    assert seq % (GRID_W * SUBLANES) == 0
    ne = router_w.shape[-1]
    t_tokens = n_batch * seq
    tl = _tiles(t_tokens, seq)
    l = 0

    n_rows = -(-(n_batch + 1) // SUBLANES) * SUBLANES
    cstack = jnp.concatenate([c, c_ctx[None, :], jnp.zeros((n_rows - n_batch - 1, d), F32)], axis=0)
    mod = _mod_call(cstack, w_mod[l], b_mod[l])
    mod3 = mod.reshape(n_rows, 1, N_MOD * d)

    w_in_b = w_in[l].astype(BF16)
    x2d = x.reshape(t_tokens, d)
    bpb = seq // tl["tok"]
    p_x = _inproj_call(x2d, mod3, lambda i: i // bpb, norm1_g[l], w_in_b, tl["tok"])
    ctx_tm = ctx_len if ctx_len <= 512 else 256
    p_c = _inproj_call(ctx.reshape(n_batch * ctx_len, d), mod3, lambda i: n_batch, norm1_g[l],
                       w_in_b[:, :2 * w_lru], ctx_tm)

    lru_args = (lru_conv_w[l], lru_conv_b[l], lru_wa[l], lru_ba[l], lru_wi[l], lru_bi[l], lru_lambda[l])
    _, h_ctx = _lru_call(p_c, nb_l, n_batch, ctx_len, *lru_args, jnp.zeros((n_batch, 2, w_lru), F32))
    y_lru, _ = _lru_call(p_x, nb_l, n_batch, seq, *lru_args, h_ctx)

    rows = seq // GRID_W
    tables = _dft_tables(rows)
    hfilt = _filt_call(seq, w_hy, hy_w1[l], hy_b1[l], hy_w2[l], hy_b2[l], hy_w3[l], hy_b3[l], hy_freq[l], hy_w4[l])
    kf = _spec_call(hfilt, seq, nb_h, tables[0], tables[1])
    cw, cb = hy_conv_w[l], hy_conv_b[l]
    s0 = 2 * nb_l
    z = _hyena_call(p_x, s0, p_x, s0 + nb_h, cw[:, :w_hy], cb[:w_hy], cw[:, w_hy:2 * w_hy], cb[w_hy:2 * w_hy],
                    hy_d[l, 0], kf, 0, tables, n_batch, seq, nb_h, True)
    hy = _hyena_call(z, 0, p_x, s0 + 2 * nb_h, cw[:, :w_hy], cb[:w_hy], cw[:, 2 * w_hy:], cb[2 * w_hy:],
                     hy_d[l, 1], kf, 1, tables, n_batch, seq, nb_h, False)

    hx1, m, logits = _outproj_call(y_lru, hy, x2d, mod3, seq, gn_lru[l], gn_hy[l], norm2_g[l],
                                   w_out[l].astype(BF16), router_w[l], router_b[l], tl["tok"])

    idx, gate, rank, counts = _route_call(logits, tl["route"])
    tm = tl["moe_tm"]
    counts = counts[0]
    pcnt = (counts + tm - 1) // tm * tm
    pend = jnp.cumsum(pcnt)
    pstart = pend - pcnt
    e_ids = jnp.arange(ne, dtype=I32)
    dest = (jnp.sum(jnp.where(idx[:, :, None] == e_ids, pstart.astype(I32), 0), axis=-1) + rank).reshape(-1)
    nblk = t_tokens * TOP_K // tm + ne
    cap = nblk * tm
    n_used = (pend[-1] // tm).astype(I32).reshape(1)
    blk_ids = jnp.arange(nblk, dtype=I32)
    block_expert = jnp.minimum(jnp.sum((pend[None, :] <= (blk_ids * tm)[:, None]).astype(I32), axis=1), ne - 1)
    block_expert = jnp.where(blk_ids < n_used[0], block_expert, block_expert[n_used[0] - 1])
    group_end = (pstart + counts).astype(I32)
    n_valid = jnp.where(blk_ids < n_used[0], jnp.clip(group_end[block_expert] - blk_ids * tm, 0, tm), 0).astype(I32)

    xs = _dispatch_call(dest, pend.astype(I32), pcnt.astype(I32), n_used, m, cap, tm, tl["disp"])
    d_ff = exp_w_down.shape[2]
    tf = next((c for c in (1024, 512, 256) if d_ff % c == 0), d_ff)
    tn = next((c for c in (1024, 512) if (d // 2) % c == 0), d // 2)
    y = _moe_call(block_expert, n_used, n_valid, xs, exp_w_gu[l], exp_b_gu[l], exp_w_down[l], exp_b_down[l],
                  tm, tf, tn)
    out = _combine_call(dest, gate, hx1, mod3, seq, final_g, y, tl["comb"])
    return out.reshape(n_batch, seq, d)
```

```python
import functools
import math

import jax
import jax.numpy as jnp
from jax import lax
from jax.experimental import pallas as pl
from jax.experimental.pallas import tpu as pltpu

F32 = jnp.float32
BF16 = jnp.bfloat16
I32 = jnp.int32
U32 = jnp.uint32
HI = lax.Precision.HIGHEST

GRID_W = 64
EPS = 1e-6
N_MOD = 6
LRU_CONV = 4
LRU_C = 8.0
HY_CONV = 3
HY_ORDER = 2
HY_BANDS = 16
HY_TARGET = 1e-2
HY_FAST = 0.3
HY_SLOW = 1.5
TOP_K = 4
SWIGLU_LIMIT = 7.0
SWIGLU_ALPHA = 1.702

LANES = 128
SUBLANES = 8
VMEM_LIMIT_BYTES = 56 * 1024 * 1024
VMEM_LIMIT_MOE_BYTES = 60 * 1024 * 1024

DFT_N1 = 2 * GRID_W
K1_USED = DFT_N1 // 2 + 1
K1_PAD = 72
assert K1_PAD % SUBLANES == 0 and K1_PAD >= K1_USED
LOOP_UNROLL = 64
K1_UNROLL = 32
assert (K1_USED - 1) % K1_UNROLL == 0


def _cparams(*sem, vmem=VMEM_LIMIT_BYTES):
    return pltpu.CompilerParams(dimension_semantics=sem, vmem_limit_bytes=vmem)


def _rms(x):
    return x * lax.rsqrt(jnp.mean(x * x, axis=-1, keepdims=True) + EPS)


def _pack_pair(a, b):
    wa = pltpu.bitcast(a.astype(BF16).astype(F32), U32)
    wb = pltpu.bitcast(b.astype(BF16).astype(F32), U32)
    return wa | (wb >> 16)


def _pack_halves(v):
    h = v.shape[-1] // 2
    return _pack_pair(v[:, :h], v[:, h:])


def _unpack_halves(w):
    return pltpu.bitcast(w & jnp.uint32(0xFFFF0000), F32), pltpu.bitcast(w << 16, F32)


def _mod_kernel(c_ref, w_ref, b_ref, o_ref):
    c = c_ref[...]
    s = c * jax.nn.sigmoid(c)
    o_ref[...] = jnp.dot(s, w_ref[...], preferred_element_type=F32, precision=HI) + b_ref[...]


def _mod_call(cstack, w_mod, b_mod):
    rows, d = cstack.shape
    n = w_mod.shape[1]
    tn = 1536 if n % 1536 == 0 else n
    return pl.pallas_call(
        _mod_kernel,
        out_shape=jax.ShapeDtypeStruct((rows, n), F32),
        grid=(n // tn,),
        in_specs=[pl.BlockSpec((rows, d), lambda j: (0, 0)),
                  pl.BlockSpec((d, tn), lambda j: (0, j)),
                  pl.BlockSpec((1, tn), lambda j: (0, j))],
        out_specs=pl.BlockSpec((rows, tn), lambda j: (0, j)),
        compiler_params=_cparams("arbitrary"),
        name="mod",
    )(cstack, w_mod, b_mod.reshape(1, n))


def _inproj_kernel(x_ref, mod_ref, g_ref, w_ref, o_ref, *, d, n_slabs, nchunk):
    x = x_ref[...]
    sh = mod_ref[0, :, 0 * d:1 * d]
    sc = mod_ref[0, :, 1 * d:2 * d]
    a = (_rms(x) * g_ref[...]) * (1.0 + sc) + sh
    ab = a.astype(BF16)
    spc = nchunk // LANES
    for j in range(n_slabs // spc):
        p = jnp.dot(ab, w_ref[:, j * nchunk:(j + 1) * nchunk], preferred_element_type=F32)
        for s in range(spc):
            o_ref[j * spc + s] = p[:, s * LANES:(s + 1) * LANES].astype(BF16)


def _inproj_call(x2d, mod3, mod_row_of_block, g, w_bf16, tm):
    t, d = x2d.shape
    n = w_bf16.shape[1]
    n_slabs = n // LANES
    nchunk = 512 if n % 512 == 0 else LANES
    kern = functools.partial(_inproj_kernel, d=d, n_slabs=n_slabs, nchunk=nchunk)
    return pl.pallas_call(
        kern,
        out_shape=jax.ShapeDtypeStruct((n_slabs, t, LANES), BF16),
        grid=(t // tm,),
        in_specs=[pl.BlockSpec((tm, d), lambda i: (i, 0)),
                  pl.BlockSpec((1, 1, N_MOD * d), lambda i: (mod_row_of_block(i), 0, 0)),
                  pl.BlockSpec((1, d), lambda i: (0, 0)),
                  pl.BlockSpec((d, n), lambda i: (0, 0), pipeline_mode=pl.Buffered(1))],
        out_specs=pl.BlockSpec((n_slabs, tm, LANES), lambda i: (0, i, 0)),
        compiler_params=_cparams("arbitrary"),
        name="inproj",
    )(x2d, mod3, g.reshape(1, d), w_bf16)


def _gelu_tanh(x):
    return 0.5 * x * (1.0 + jnp.tanh(math.sqrt(2.0 / math.pi) * (x + 0.044715 * (x * x * x))))


def _scan_tile(a, b, row, reverse):
    for s in (1, 2, 4):
        if reverse:
            m = row < SUBLANES - s
            sh = SUBLANES - s
        else:
            m = row >= s
            sh = s
        a_s = jnp.where(m, pltpu.roll(a, sh, 0), 1.0)
        b_s = jnp.where(m, pltpu.roll(b, sh, 0), 0.0)
        b = a * b_s + b
        a = a * a_s
    return a, b


def _lru_kernel(r_ref, g_ref, cw_ref, cb_ref, wa_ref, wi_ref, ba_ref, bi_ref, lam_ref, h0_ref,
                y_ref, hl_ref, xp_ref, hf_ref, ab_ref, bb_ref, *, t_len, tc):
    nc = t_len // tc
    ntile = tc // SUBLANES
    zero8 = jnp.zeros((SUBLANES, LANES), F32)
    xp_ref[0:SUBLANES, :] = zero8
    xp_ref[t_len + SUBLANES:t_len + 2 * SUBLANES, :] = zero8

    def copy_body(ci, c):
        t0 = pl.multiple_of(ci * tc, tc)
        xp_ref[pl.ds(t0 + SUBLANES, tc), :] = r_ref[0, pl.ds(t0, tc), :].astype(F32)
        return c
    lax.fori_loop(0, nc, copy_body, 0)

    row = lax.broadcasted_iota(I32, (SUBLANES, LANES), 0)
    cw = cw_ref[...]
    cb = cb_ref[...]
    lam = lam_ref[...]
    sp = jnp.log1p(jnp.exp(-lam))
    wab = [wa_ref[d, 0].astype(BF16) for d in range(2)]
    wib = [wi_ref[d, 0].astype(BF16) for d in range(2)]

    def coeffs(u, ub, d):
        ga = jnp.dot(ub, wab[d], preferred_element_type=F32) + ba_ref[d:d + 1, :]
        gi = jnp.dot(ub, wib[d], preferred_element_type=F32) + bi_ref[d:d + 1, :]
        rg = jax.nn.sigmoid(ga)
        ig = jax.nn.sigmoid(gi)
        a = jnp.exp((-LRU_C) * rg * sp[d:d + 1, :])
        b = jnp.sqrt(1.0 - a * a) * (ig * u)
        return a, b

    def fwd_body(ci, hcar):
        t0 = pl.multiple_of(ci * tc, tc)
        u = (cw[0:1, :] * xp_ref[pl.ds(t0 + SUBLANES - 2, tc), :] + cw[1:2, :] * xp_ref[pl.ds(t0 + SUBLANES - 1, tc), :]
             + cw[2:3, :] * xp_ref[pl.ds(t0 + SUBLANES, tc), :] + cw[3:4, :] * xp_ref[pl.ds(t0 + SUBLANES + 1, tc), :]) + cb
        ub = u.astype(BF16)
        a0, b0 = coeffs(u, ub, 0)
        a1, b1 = coeffs(u, ub, 1)
        ab_ref[pl.ds(t0, tc), :] = a1
        bb_ref[pl.ds(t0, tc), :] = b1
        for j in range(ntile):
            at, bt = _scan_tile(a0[j * 8:(j + 1) * 8], b0[j * 8:(j + 1) * 8], row, False)
            hf_ref[pl.ds(t0 + j * 8, 8), :] = at * hcar + bt
            atot = jnp.broadcast_to(at[7:8, :], (SUBLANES, LANES))
            btot = jnp.broadcast_to(bt[7:8, :], (SUBLANES, LANES))
            hcar = atot * hcar + btot
        return hcar

    h0f = jnp.broadcast_to(h0_ref[0, 0:1, :], (SUBLANES, LANES))
    hfin = lax.fori_loop(0, nc, fwd_body, h0f, unroll=4 if nc % 4 == 0 else 1)
    hl_ref[0, 0:1, :] = hfin[0:1, :]

    def bwd_body(k, hcar):
        ci = nc - 1 - k
        t0 = pl.multiple_of(ci * tc, tc)
        a1 = ab_ref[pl.ds(t0, tc), :]
        b1 = bb_ref[pl.ds(t0, tc), :]
        gg = g_ref[0, pl.ds(t0, tc), :].astype(F32)
        gl = _gelu_tanh(gg)
        for j in reversed(range(ntile)):
            at, bt = _scan_tile(a1[j * 8:(j + 1) * 8], b1[j * 8:(j + 1) * 8], row, True)
            hb = at * hcar + bt
            hf = hf_ref[pl.ds(t0 + j * 8, 8), :]
            y_ref[0, pl.ds(t0 + j * 8, 8), :] = ((hf + hb) * gl[j * 8:(j + 1) * 8]).astype(y_ref.dtype)
            atot = jnp.broadcast_to(at[0:1, :], (SUBLANES, LANES))
            btot = jnp.broadcast_to(bt[0:1, :], (SUBLANES, LANES))
            hcar = atot * hcar + btot
        return hcar

    h0b = jnp.broadcast_to(h0_ref[0, 1:2, :], (SUBLANES, LANES))
    hfin_b = lax.fori_loop(0, nc, bwd_body, h0b, unroll=2 if nc % 2 == 0 else 1)
    hl_ref[0, 1:2, :] = hfin_b[0:1, :]


def _lru_call(p_slabs, nb, n_batch, t_len, conv_w, conv_b, wa, ba, wi, bi, lam, h0):
    tc = 128 if t_len % 128 == 0 else t_len
    kern = functools.partial(_lru_kernel, t_len=t_len, tc=tc)
    w_lru = nb * LANES
    return pl.pallas_call(
        kern,
        out_shape=[jax.ShapeDtypeStruct((nb, n_batch * t_len, LANES), F32),
                   jax.ShapeDtypeStruct((n_batch, 2, w_lru), F32)],
        grid=(nb, n_batch),
        in_specs=[pl.BlockSpec((1, t_len, LANES), lambda n, b: (n, b, 0)),
                  pl.BlockSpec((1, t_len, LANES), lambda n, b: (nb + n, b, 0)),
                  pl.BlockSpec((LRU_CONV, LANES), lambda n, b: (0, n)),
                  pl.BlockSpec((1, LANES), lambda n, b: (0, n)),
                  pl.BlockSpec((2, 1, LANES, LANES), lambda n, b: (0, n, 0, 0)),
                  pl.BlockSpec((2, 1, LANES, LANES), lambda n, b: (0, n, 0, 0)),
                  pl.BlockSpec((2, LANES), lambda n, b: (0, n)),
                  pl.BlockSpec((2, LANES), lambda n, b: (0, n)),
                  pl.BlockSpec((2, LANES), lambda n, b: (0, n)),
                  pl.BlockSpec((1, 2, LANES), lambda n, b: (b, 0, n))],
        out_specs=[pl.BlockSpec((1, t_len, LANES), lambda n, b: (n, b, 0)),
                   pl.BlockSpec((1, 2, LANES), lambda n, b: (b, 0, n))],
        scratch_shapes=[pltpu.VMEM((t_len + 2 * SUBLANES, LANES), F32),
                        pltpu.VMEM((t_len, LANES), F32),
                        pltpu.VMEM((t_len, LANES), F32),
                        pltpu.VMEM((t_len, LANES), F32)],
        compiler_params=_cparams("arbitrary", "arbitrary"),
        name="lru",
    )(p_slabs, p_slabs, conv_w, conv_b.reshape(1, w_lru), wa, wi, ba, bi, lam, h0)


def _filt_kernel(w1_ref, b1_ref, w2_ref, b2_ref, w3_ref, b3_ref, fr_ref, w4_ref, o_ref,
                 *, l_len, rows, tl, w_hy):
    i = pl.program_id(0)
    rho = i * tl + lax.broadcasted_iota(I32, (tl, 1), 0)
    tidx = ((rho % GRID_W) * rows + rho // GRID_W).astype(F32)
    tt = tidx * (1.0 / (l_len - 1))
    ww = (2.0 * math.pi) * tidx / l_len
    band = lax.broadcasted_iota(I32, (1, HY_BANDS), 1).astype(F32)
    f = 1e-4 + band * ((HY_BANDS - 1 - 1e-4) / (HY_BANDS - 1))
    fw = f * ww
    w1 = w1_ref[...]
    pre = (tt * w1[0:1, :]
           + jnp.dot(jnp.cos(fw), w1[1:1 + HY_BANDS, :], preferred_element_type=F32, precision=HI)
           + jnp.dot(-jnp.sin(fw), w1[1 + HY_BANDS:1 + 2 * HY_BANDS, :], preferred_element_type=F32, precision=HI))
    fr = fr_ref[...]
    h = jnp.sin(fr * (pre + b1_ref[...]))
    h = jnp.sin(fr * (jnp.dot(h, w2_ref[...], preferred_element_type=F32, precision=HI) + b2_ref[...]))
    h = jnp.sin(fr * (jnp.dot(h, w3_ref[...], preferred_element_type=F32, precision=HI) + b3_ref[...]))
    max_decay = math.log(HY_TARGET) / HY_FAST
    min_decay = math.log(HY_TARGET) / HY_SLOW
    nblk = w_hy // LANES
    h_hi = h.astype(BF16)
    h_lo = (h - h_hi.astype(F32)).astype(BF16)
    for cb in range(nblk):
        ch = (cb * LANES + lax.broadcasted_iota(I32, (1, LANES), 1)).astype(F32)
        delta = jnp.abs(min_decay + ch * ((max_decay - min_decay) / (w_hy - 1)))
        decay = jnp.exp(-tt * delta)
        for od in range(HY_ORDER * 2):
            col = od * w_hy + cb * LANES
            w4 = w4_ref[:, col:col + LANES]
            w_hi = w4.astype(BF16)
            w_lo = (w4 - w_hi.astype(F32)).astype(BF16)
            v = (jnp.dot(h_hi, w_hi, preferred_element_type=F32) + jnp.dot(h_lo, w_hi, preferred_element_type=F32)
                 + jnp.dot(h_hi, w_lo, preferred_element_type=F32))
            o_ref[od * nblk + cb] = v * decay


def _filt_call(l_len, w_hy, w1, b1, w2, b2, w3, b3, freq, w4):
    rows = l_len // GRID_W
    tl = 512 if l_len % 512 == 0 else l_len
    fh = w2.shape[0]
    nslab = HY_ORDER * 2 * (w_hy // LANES)
    kern = functools.partial(_filt_kernel, l_len=l_len, rows=rows, tl=tl, w_hy=w_hy)
    full = lambda a: pl.BlockSpec(a.shape, lambda i: (0,) * a.ndim)
    args = (w1, b1.reshape(1, fh), w2, b2.reshape(1, fh), w3, b3.reshape(1, fh), freq.reshape(1, fh), w4)
    return pl.pallas_call(
        kern,
        out_shape=jax.ShapeDtypeStruct((nslab, l_len, LANES), F32),
        grid=(l_len // tl,),
        in_specs=[full(a) for a in args],
        out_specs=pl.BlockSpec((nslab, tl, LANES), lambda i: (0, i, 0)),
        compiler_params=_cparams("arbitrary"),
        name="filt",
    )(*args)


def _dft_tables(rows):
    n_len = DFT_N1 * rows
    k1 = jnp.arange(K1_PAD, dtype=I32)
    r = jnp.arange(rows, dtype=I32)
    w = jnp.arange(GRID_W, dtype=I32)
    idx = (k1[None, :, None] * (r[:, None, None] + rows * w[None, None, :])) % n_len
    ang = idx.astype(F32) * (2.0 * math.pi / n_len)
    valid = (k1 < K1_USED)[None, :, None]
    ca = jnp.where(valid, jnp.cos(ang), 0.0)
    sa = jnp.where(valid, jnp.sin(ang), 0.0)
    ta = jnp.stack([ca, -sa], axis=2).reshape(rows, 2 * K1_PAD, GRID_W)
    td = jnp.transpose(ta, (0, 2, 1))
    k2 = jnp.arange(rows, dtype=I32)
    th = ((k2[:, None] * r[None, :]) % rows).astype(F32) * (2.0 * math.pi / rows)
    c, s = jnp.cos(th), jnp.sin(th)
    tb = jnp.concatenate([jnp.stack([c, s], axis=2).reshape(rows, 2 * rows),
                          jnp.stack([-s, c], axis=2).reshape(rows, 2 * rows)], axis=0)
    tcm = jnp.stack([jnp.concatenate([c, -s], axis=1), jnp.concatenate([s, c], axis=1)], axis=1).reshape(2 * rows, 2 * rows)
    return ta.astype(BF16), tb.astype(BF16), tcm.astype(BF16), td.astype(BF16)


def _pitch_r(rows):
    return rows + SUBLANES


def _stage_a(tile_fn, ta_ref, s1_ref, rows):
    pitch = _pitch_r(rows)

    def body(r, c):
        out = jnp.dot(ta_ref[r], tile_fn(r), preferred_element_type=F32)
        words = pltpu.bitcast(out.astype(BF16), U32)
        s1_ref[pl.ds(r, K1_PAD, stride=pitch), :] = words
        return c
    lax.fori_loop(0, rows, body, 0, unroll=min(LOOP_UNROLL, rows))


def _loop_k1(body):
    lax.fori_loop(0, K1_USED - 1, body, 0, unroll=K1_UNROLL)
    body(K1_USED - 1, 0)


def _stage_b(k1, tb_ref, s1_ref, rows):
    pitch = _pitch_r(rows)
    t = s1_ref[pl.ds(pl.multiple_of(k1 * pitch, SUBLANES), rows), :]
    y = jnp.dot(tb_ref[...], pltpu.bitcast(t, BF16), preferred_element_type=F32)
    return y[:rows], y[rows:]


def _spec_kernel(hf_ref, hb_ref, ta_ref, tb_ref, kf_ref, s1_ref, *, rows):
    n_len = DFT_N1 * rows

    def scale(k1):
        return jnp.where((k1 == 0) | (k1 == K1_USED - 1), 1.0, 2.0) * (1.0 / n_len)

    _stage_a(lambda r: hf_ref[0, pl.ds(pl.multiple_of(r * GRID_W, GRID_W), GRID_W), :].astype(BF16),
             ta_ref, s1_ref, rows)

    def b_fwd(k1, c):
        yre, yim = _stage_b(k1, tb_ref, s1_ref, rows)
        sc = scale(k1)
        kf_ref[0, 0, k1, 0] = yre * sc
        kf_ref[0, 0, k1, 1] = yim * sc
        return c
    _loop_k1(b_fwd)

    rid = lax.broadcasted_iota(I32, (GRID_W, LANES), 0)

    def tile_b(r):
        x = hb_ref[0, pl.ds(pl.multiple_of(r * GRID_W, GRID_W), GRID_W), :]
        x = jnp.where((rid == 0) & (r == 0), 0.0, x)
        return x.astype(BF16)
    _stage_a(tile_b, ta_ref, s1_ref, rows)

    def b_bwd(k1, c):
        yre, yim = _stage_b(k1, tb_ref, s1_ref, rows)
        sc = scale(k1)
        kf_ref[0, 0, k1, 0] += yre * sc
        kf_ref[0, 0, k1, 1] -= yim * sc
        return c
    _loop_k1(b_bwd)


def _spec_call(hfilt, l_len, nblk, ta, tb):
    rows = l_len // GRID_W
    kern = functools.partial(_spec_kernel, rows=rows)
    return pl.pallas_call(
        kern,
        out_shape=jax.ShapeDtypeStruct((HY_ORDER, nblk, K1_USED, 2, rows, LANES), F32),
        grid=(HY_ORDER, nblk),
        in_specs=[pl.BlockSpec((1, l_len, LANES), lambda o, c: (o * 2 * nblk + c, 0, 0)),
                  pl.BlockSpec((1, l_len, LANES), lambda o, c: (o * 2 * nblk + nblk + c, 0, 0)),
                  pl.BlockSpec(ta.shape, lambda o, c: (0, 0, 0), pipeline_mode=pl.Buffered(1)),
                  pl.BlockSpec(tb.shape, lambda o, c: (0, 0), pipeline_mode=pl.Buffered(1))],
        out_specs=pl.BlockSpec((1, 1, K1_USED, 2, rows, LANES), lambda o, c: (o, c, 0, 0, 0, 0)),
        scratch_shapes=[pltpu.VMEM((K1_PAD * _pitch_r(rows), LANES), U32)],
        compiler_params=_cparams("arbitrary", "arbitrary"),
        name="spec",
    )(hfilt, hfilt, ta, tb)


def _conv_grid_order(src_ref, dst_ref, w_ref, b_ref, rows):
    w0, w1, w2, bb = w_ref[0:1, :], w_ref[1:2, :], w_ref[2:3, :], b_ref[...]
    g = GRID_W
    rid = lax.broadcasted_iota(I32, (g, LANES), 0)

    def grp(r0):
        return src_ref[0, pl.ds(r0, g), :].astype(F32)

    last = grp((rows - 1) * g)
    prev0 = jnp.where(rid >= 1, pltpu.roll(last, 1, 0), 0.0)
    dst_ref[pl.ds(0, g), :] = (w0 * prev0 + w1 * grp(0) + w2 * grp(g) + bb).astype(dst_ref.dtype)
    first = grp(0)
    nxt = jnp.where(rid < g - 1, pltpu.roll(first, g - 1, 0), 0.0)
    dst_ref[pl.ds((rows - 1) * g, g), :] = (w0 * grp((rows - 2) * g) + w1 * last + w2 * nxt + bb).astype(dst_ref.dtype)

    def body(r, c):
        rm = pl.multiple_of((r - 1) * g, g)
        r0 = pl.multiple_of(r * g, g)
        rp = pl.multiple_of((r + 1) * g, g)
        v = w0 * grp(rm) + w1 * grp(r0) + w2 * grp(rp) + bb
        dst_ref[pl.ds(r0, g), :] = v.astype(dst_ref.dtype)
        return c
    n_mid = rows - 2
    lax.fori_loop(1, rows - 1, body, 0, unroll=next(u for u in (6, 4, 2, 1) if n_mid % u == 0))


def _hyena_kernel(sig_ref, gate_ref, cws_ref, cbs_ref, cwg_ref, cbg_ref, d_ref, kf_ref,
                  ta_ref, tb_ref, tc_ref, td_ref, o_ref, u_ref, gt_ref, s1_ref, s2_ref,
                  *, rows, conv_sig):
    g = GRID_W

    @pl.when((pl.program_id(0) == 0) & (pl.program_id(1) == 0))
    def _():
        s2_ref[...] = jnp.zeros(s2_ref.shape, U32)

    if conv_sig:
        _conv_grid_order(sig_ref, u_ref, cws_ref, cbs_ref, rows)
    else:
        def cp(r, c):
            r0 = pl.multiple_of(r * g, g)
            u_ref[pl.ds(r0, g), :] = sig_ref[0, pl.ds(r0, g), :]
            return c
        lax.fori_loop(0, rows, cp, 0, unroll=min(LOOP_UNROLL, rows))
    _conv_grid_order(gate_ref, gt_ref, cwg_ref, cbg_ref, rows)

    _stage_a(lambda r: u_ref[pl.ds(pl.multiple_of(r * g, g), g), :], ta_ref, s1_ref, rows)

    def bc_body(k1, c):
        yre, yim = _stage_b(k1, tb_ref, s1_ref, rows)
        kre = kf_ref[0, 0, k1, 0]
        kim = kf_ref[0, 0, k1, 1]
        zre = yre * kre - yim * kim
        zim = yre * kim + yim * kre
        z = jnp.concatenate([zre, zim], axis=0).astype(BF16)
        v = jnp.dot(tc_ref[...], z, preferred_element_type=F32)
        s2_ref[pl.ds(k1, rows, stride=K1_PAD), :] = pltpu.bitcast(v.astype(BF16), U32)
        return c
    _loop_k1(bc_body)

    dd = d_ref[...]

    def d_body(r, c):
        r0 = pl.multiple_of(r * g, g)
        t = s2_ref[pl.ds(pl.multiple_of(r * K1_PAD, SUBLANES), K1_PAD), :]
        y = jnp.dot(td_ref[r], pltpu.bitcast(t, BF16), preferred_element_type=F32)
        uu = u_ref[pl.ds(r0, g), :].astype(F32)
        o_ref[0, pl.ds(r0, g), :] = (gt_ref[pl.ds(r0, g), :] * (y + dd * uu)).astype(o_ref.dtype)
        return c
    lax.fori_loop(0, rows, d_body, 0, unroll=min(LOOP_UNROLL, rows))


def _hyena_call(sig_arr, sig_slab0, gate_arr, gate_slab0, conv_w_sig, conv_b_sig, conv_w_gate, conv_b_gate,
                d_vec, kf, order, tables, n_batch, l_len, nblk, conv_sig):
    rows = l_len // GRID_W
    ta, tb, tcm, td = tables
    kern = functools.partial(_hyena_kernel, rows=rows, conv_sig=conv_sig)
    w_hy = nblk * LANES
    const = lambda a: pl.BlockSpec(a.shape, lambda c, b: (0,) * a.ndim, pipeline_mode=pl.Buffered(1))
    return pl.pallas_call(
        kern,
        out_shape=jax.ShapeDtypeStruct((nblk, n_batch * l_len, LANES), BF16),
        grid=(nblk, n_batch),
        in_specs=[pl.BlockSpec((1, l_len, LANES), lambda c, b: (sig_slab0 + c, b, 0)),
                  pl.BlockSpec((1, l_len, LANES), lambda c, b: (gate_slab0 + c, b, 0)),
                  pl.BlockSpec((HY_CONV, LANES), lambda c, b: (0, c)),
                  pl.BlockSpec((1, LANES), lambda c, b: (0, c)),
                  pl.BlockSpec((HY_CONV, LANES), lambda c, b: (0, c)),
                  pl.BlockSpec((1, LANES), lambda c, b: (0, c)),
                  pl.BlockSpec((1, LANES), lambda c, b: (0, c)),
                  pl.BlockSpec((1, 1, K1_USED, 2, rows, LANES), lambda c, b: (order, c, 0, 0, 0, 0),
                               pipeline_mode=pl.Buffered(1)),
                  const(ta), const(tb), const(tcm), const(td)],
        out_specs=pl.BlockSpec((1, l_len, LANES), lambda c, b: (c, b, 0)),
        scratch_shapes=[pltpu.VMEM((l_len, LANES), BF16),
                        pltpu.VMEM((l_len, LANES), F32),
                        pltpu.VMEM((K1_PAD * _pitch_r(rows), LANES), U32),
                        pltpu.VMEM((rows * K1_PAD, LANES), U32)],
        compiler_params=_cparams("arbitrary", "arbitrary"),
        name="hyena%d" % order,
    )(sig_arr, gate_arr, conv_w_sig, conv_b_sig.reshape(1, w_hy), conv_w_gate, conv_b_gate.reshape(1, w_hy),
      d_vec.reshape(1, w_hy), kf, ta, tb, tcm, td)


def _outproj_kernel(yl_ref, hy_ref, x_ref, mod_ref, gl_ref, gh_ref, n2_ref, w_ref, rw2_ref, rb_ref,
                    hx_ref, m_ref, lg_ref, *, d, nb_l, nb_h, ne):
    yl = jnp.concatenate([yl_ref[j].astype(F32) for j in range(nb_l)], axis=-1)
    hy = jnp.concatenate([hy_ref[j].astype(F32) for j in range(nb_h)], axis=-1)
    cat = jnp.concatenate([_rms(yl) * gl_ref[...], _rms(hy) * gh_ref[...]], axis=-1).astype(BF16)
    out = jnp.dot(cat, w_ref[...], preferred_element_type=F32)
    g1 = mod_ref[0, :, 2 * d:3 * d]
    sh2 = mod_ref[0, :, 3 * d:4 * d]
    sc2 = mod_ref[0, :, 4 * d:5 * d]
    hx = x_ref[...] + g1 * out
    hx_ref[...] = hx
    m = (_rms(hx) * n2_ref[...]) * (1.0 + sc2) + sh2
    m_ref[...] = _pack_halves(m)
    m_hi = m.astype(BF16)
    m_lo = (m - m_hi.astype(F32)).astype(BF16)
    a = jnp.dot(m_hi, rw2_ref[...], preferred_element_type=F32)
    b = jnp.dot(m_lo, rw2_ref[:, :ne], preferred_element_type=F32)
    lg_ref[...] = (a[:, :ne] + b) + a[:, ne:] + rb_ref[...]


def _outproj_call(y_lru, hy, x2d, mod3, rows_per_batch, gn_lru, gn_hy, norm2_g, w_out_bf16, router_w, router_b, tm):
    t, d = x2d.shape
    nb_l, nb_h = y_lru.shape[0], hy.shape[0]
    ne = router_w.shape[1]
    bpb = rows_per_batch // tm
    rw_hi = router_w.astype(BF16)
    rw_lo = (router_w - rw_hi.astype(F32)).astype(BF16)
    rw2 = jnp.concatenate([rw_hi, rw_lo], axis=1)
    kern = functools.partial(_outproj_kernel, d=d, nb_l=nb_l, nb_h=nb_h, ne=ne)
    return pl.pallas_call(
        kern,
        out_shape=[jax.ShapeDtypeStruct((t, d), F32), jax.ShapeDtypeStruct((t, d // 2), U32),
                   jax.ShapeDtypeStruct((t, ne), F32)],
        grid=(t // tm,),
        in_specs=[pl.BlockSpec((nb_l, tm, LANES), lambda i: (0, i, 0)),
                  pl.BlockSpec((nb_h, tm, LANES), lambda i: (0, i, 0)),
                  pl.BlockSpec((tm, d), lambda i: (i, 0)),
                  pl.BlockSpec((1, 1, N_MOD * d), lambda i: (i // bpb, 0, 0)),
                  pl.BlockSpec((1, nb_l * LANES), lambda i: (0, 0)),
                  pl.BlockSpec((1, nb_h * LANES), lambda i: (0, 0)),
                  pl.BlockSpec((1, d), lambda i: (0, 0)),
                  pl.BlockSpec(w_out_bf16.shape, lambda i: (0, 0), pipeline_mode=pl.Buffered(1)),
                  pl.BlockSpec((d, 2 * ne), lambda i: (0, 0)),
                  pl.BlockSpec((1, ne), lambda i: (0, 0))],
        out_specs=[pl.BlockSpec((tm, d), lambda i: (i, 0)),
                   pl.BlockSpec((tm, d // 2), lambda i: (i, 0)),
                   pl.BlockSpec((tm, ne), lambda i: (i, 0))],
        compiler_params=_cparams("arbitrary"),
        name="outproj",
    )(y_lru, hy, x2d, mod3, gn_lru.reshape(1, -1), gn_hy.reshape(1, -1), norm2_g.reshape(1, d),
      w_out_bf16, rw2, router_b.reshape(1, ne))


def _route_kernel(lg_ref, idx_ref, gate_ref, rank_ref, cnt_ref, run_ref, *, tb, ne):
    @pl.when(pl.program_id(0) == 0)
    def _():
        run_ref[...] = jnp.zeros(run_ref.shape, F32)

    l = lg_ref[...]
    lane = lax.broadcasted_iota(I32, (tb, ne), 1)
    vals, idxs, ohs = [], [], []
    for _ in range(TOP_K):
        m = jnp.max(l, axis=-1, keepdims=True)
        ix = jnp.min(jnp.where(l == m, lane, ne), axis=-1, keepdims=True)
        sel = lane == ix
        vals.append(m)
        idxs.append(ix)
        ohs.append(sel.astype(F32))
        l = jnp.where(sel, -jnp.inf, l)
    es = [jnp.exp(v - vals[0]) for v in vals]
    den = es[0] + es[1] + es[2] + es[3]
    oh_all = ohs[0] + ohs[1] + ohs[2] + ohs[3]
    ri = lax.broadcasted_iota(I32, (tb, tb), 0)
    ci = lax.broadcasted_iota(I32, (tb, tb), 1)
    ltri = (ci < ri).astype(BF16)
    before = jnp.dot(ltri, oh_all.astype(BF16), preferred_element_type=F32) + run_ref[...]
    for k in range(TOP_K):
        idx_ref[:, k:k + 1] = idxs[k]
        gate_ref[:, k:k + 1] = es[k] / den
        rank_ref[:, k:k + 1] = jnp.sum(ohs[k] * before, axis=-1, keepdims=True).astype(I32)
    run_ref[...] += jnp.sum(oh_all, axis=0, keepdims=True)
    cnt_ref[...] = run_ref[...].astype(I32)


def _route_call(logits, tb):
    t, ne = logits.shape
    kern = functools.partial(_route_kernel, tb=tb, ne=ne)
    blk = lambda: pl.BlockSpec((tb, TOP_K), lambda i: (i, 0))
    return pl.pallas_call(
        kern,
        out_shape=[jax.ShapeDtypeStruct((t, TOP_K), I32), jax.ShapeDtypeStruct((t, TOP_K), F32),
                   jax.ShapeDtypeStruct((t, TOP_K), I32), jax.ShapeDtypeStruct((1, ne), I32)],
        grid=(t // tb,),
        in_specs=[pl.BlockSpec((tb, ne), lambda i: (i, 0))],
        out_specs=[blk(), blk(), blk(), pl.BlockSpec((1, ne), lambda i: (0, 0))],
        scratch_shapes=[pltpu.VMEM((1, ne), F32)],
        compiler_params=_cparams("arbitrary"),
        name="route",
    )(logits)


def _dispatch_kernel(dest_ref, pend_ref, pcnt_ref, nu_ref, m_ref, xs_hbm, zbuf, sem, zsem, *, tbd, tm, ne, nblk):
    i = pl.program_id(0)

    @pl.when(i == 0)
    def _():
        zbuf[...] = jnp.zeros(zbuf.shape, U32)

        def zero_block(row0):
            cp = pltpu.make_async_copy(zbuf, xs_hbm.at[pl.ds(pl.multiple_of(row0, tm), tm), :], zsem)
            cp.start()
            cp.wait()

        def zfill(e, c):
            @pl.when(pcnt_ref[e] > 0)
            def _():
                zero_block(pend_ref[e] - tm)
            return c
        lax.fori_loop(0, ne, zfill, 0)

        def ztail(bi, c):
            zero_block(bi * tm)
            return c
        lax.fori_loop(nu_ref[0], nblk, ztail, 0)

    def body(tt, c):
        tile = m_ref.at[pl.ds(pl.multiple_of(tt * SUBLANES, SUBLANES), SUBLANES), :]
        for s in range(SUBLANES):
            for k in range(TOP_K):
                d = dest_ref[(tt * SUBLANES + s) * TOP_K + k]
                pltpu.make_async_copy(tile.at[pl.ds(s, 1), :], xs_hbm.at[pl.ds(d, 1), :], sem).start(priority=k % 2)
        return c
    lax.fori_loop(0, tbd // SUBLANES, body, 0)
    for k in range(TOP_K):
        pltpu.make_async_copy(m_ref, xs_hbm.at[pl.ds(0, tbd), :], sem).wait()


def _dispatch_call(dest_flat, pend, pcnt, n_used, m, cap, tm, tbd):
    t, d = m.shape
    ne = pend.shape[0]
    kern = functools.partial(_dispatch_kernel, tbd=tbd, tm=tm, ne=ne, nblk=cap // tm)
    return pl.pallas_call(
        kern,
        out_shape=jax.ShapeDtypeStruct((cap, d), U32),
        grid=(t // tbd,),
        in_specs=[pl.BlockSpec((tbd * TOP_K,), lambda i: (i,), memory_space=pltpu.SMEM),
                  pl.BlockSpec(memory_space=pltpu.SMEM),
                  pl.BlockSpec(memory_space=pltpu.SMEM),
                  pl.BlockSpec(memory_space=pltpu.SMEM),
                  pl.BlockSpec((tbd, d), lambda i: (i, 0))],
        out_specs=pl.BlockSpec(memory_space=pl.ANY),
        scratch_shapes=[pltpu.VMEM((tm, d), U32), pltpu.SemaphoreType.DMA(()), pltpu.SemaphoreType.DMA(())],
        compiler_params=_cparams("arbitrary"),
        name="dispatch",
    )(dest_flat, pend, pcnt, n_used, m)


MOE_SUB = 4


def _row_blocks(nv, tm, compute, fill_zero):
    @pl.when(nv == tm)
    def _():
        compute(0, tm)

    tq = tm // MOE_SUB
    for q in range(MOE_SUB):
        @pl.when((nv < tm) & (nv > q * tq))
        def _():
            compute(q * tq, tq)

        @pl.when(nv <= q * tq)
        def _():
            fill_zero(q * tq, tq)


MOE_WSPLIT = 1


def _moe_up_kernel(be_ref, nu_ref, nv_ref, x_ref, *refs):
    wg_refs = refs[:MOE_WSPLIT]
    wl_refs = refs[MOE_WSPLIT:2 * MOE_WSPLIT]
    bg_ref, bl_ref, a_ref, xb_ref = refs[2 * MOE_WSPLIT:]
    i = pl.program_id(0)
    j = pl.program_id(1)
    half = x_ref.shape[-1]
    tm = x_ref.shape[0]
    th = wg_refs[0].shape[-1]
    nv = nv_ref[i]

    @pl.when((nv > 0) & (j == 0))
    def _():
        hi, lo = _unpack_halves(x_ref[...])
        xb_ref[:, :half] = hi.astype(BF16)
        xb_ref[:, half:] = lo.astype(BF16)

    def compute(r0, nr):
        xb = xb_ref[r0:r0 + nr, :]
        for p in range(MOE_WSPLIT):
            cols = slice(p * th, (p + 1) * th)
            glu = jnp.dot(xb, wg_refs[p][0].astype(BF16), preferred_element_type=F32) + bg_ref[0, :, cols]
            lin = jnp.dot(xb, wl_refs[p][0].astype(BF16), preferred_element_type=F32) + bl_ref[0, :, cols]
            glu = jnp.minimum(glu, SWIGLU_LIMIT)
            lin = jnp.clip(lin, -SWIGLU_LIMIT, SWIGLU_LIMIT)
            a_ref[r0:r0 + nr, cols] = (glu * jax.nn.sigmoid(SWIGLU_ALPHA * glu) * (lin + 1.0)).astype(BF16)

    def fill_zero(r0, nr):
        a_ref[r0:r0 + nr, :] = jnp.zeros((nr, a_ref.shape[1]), BF16)

    _row_blocks(nv, tm, compute, fill_zero)


def _moe_down_kernel(be_ref, nu_ref, nv_ref, a_ref, wa_ref, wb_ref, ba_ref, bb_ref, y_ref):
    i = pl.program_id(1)
    tm = a_ref.shape[0]

    def compute(r0, nr):
        a = a_ref[r0:r0 + nr, :]
        ya = jnp.dot(a, wa_ref[0].astype(BF16), preferred_element_type=F32) + ba_ref[0]
        yb = jnp.dot(a, wb_ref[0].astype(BF16), preferred_element_type=F32) + bb_ref[0]
        y_ref[r0:r0 + nr, :] = _pack_pair(ya, yb)

    def fill_zero(r0, nr):
        y_ref[r0:r0 + nr, :] = jnp.zeros((nr, y_ref.shape[1]), U32)

    _row_blocks(nv_ref[i], tm, compute, fill_zero)


def _moe_call(block_expert, n_used, n_valid, xs, w_gu, b_gu, w_down, b_down, tm, tf, tn):
    cap, half = xs.shape
    d = 2 * half
    ne, _, two_ff = w_gu.shape
    d_ff = two_ff // 2
    nf = d_ff // tf
    nn = half // tn
    nblk = cap // tm
    assert tm % (MOE_SUB * 2 * SUBLANES) == 0

    def keep(i, j, nu, last):
        return jnp.where(i < nu[0], j, last)

    th = tf // MOE_WSPLIT

    def w_piece(chunk0, p):
        return pl.BlockSpec((1, d, th), lambda i, j, be, nu, nv: (
            be[i], 0, (chunk0 + keep(i, j, nu, nf - 1)) * MOE_WSPLIT + p))

    up_spec = pltpu.PrefetchScalarGridSpec(
        num_scalar_prefetch=3,
        grid=(nblk, nf),
        in_specs=([pl.BlockSpec((tm, half), lambda i, j, be, nu, nv: (i, 0))]
                  + [w_piece(0, p) for p in range(MOE_WSPLIT)]
                  + [w_piece(nf, p) for p in range(MOE_WSPLIT)]
                  + [pl.BlockSpec((1, 1, tf), lambda i, j, be, nu, nv: (be[i], 0, keep(i, j, nu, nf - 1))),
                     pl.BlockSpec((1, 1, tf), lambda i, j, be, nu, nv: (be[i], 0, nf + keep(i, j, nu, nf - 1)))]),
        out_specs=pl.BlockSpec((tm, tf), lambda i, j, be, nu, nv: (i, j)),
        scratch_shapes=[pltpu.VMEM((tm, d), BF16)],
    )
    b_gu3 = b_gu.reshape(ne, 1, two_ff)
    act = pl.pallas_call(
        _moe_up_kernel,
        out_shape=jax.ShapeDtypeStruct((cap, d_ff), BF16),
        grid_spec=up_spec,
        compiler_params=_cparams("arbitrary", "arbitrary", vmem=VMEM_LIMIT_MOE_BYTES),
        name="moe_up",
    )(block_expert, n_used, n_valid, xs, *([w_gu] * (2 * MOE_WSPLIT)), b_gu3, b_gu3)

    down_spec = pltpu.PrefetchScalarGridSpec(
        num_scalar_prefetch=3,
        grid=(nn, nblk),
        in_specs=[pl.BlockSpec((tm, d_ff), lambda n, i, be, nu, nv: (i, 0)),
                  pl.BlockSpec((1, d_ff, tn), lambda n, i, be, nu, nv: (be[i], 0, n)),
                  pl.BlockSpec((1, d_ff, tn), lambda n, i, be, nu, nv: (be[i], 0, nn + n)),
                  pl.BlockSpec((1, 1, tn), lambda n, i, be, nu, nv: (be[i], 0, n)),
                  pl.BlockSpec((1, 1, tn), lambda n, i, be, nu, nv: (be[i], 0, nn + n))],
        out_specs=pl.BlockSpec((tm, tn), lambda n, i, be, nu, nv: (i, n)),
    )
    b_d3 = b_down.reshape(ne, 1, d)
    return pl.pallas_call(
        _moe_down_kernel,
        out_shape=jax.ShapeDtypeStruct((cap, half), U32),
        grid_spec=down_spec,
        compiler_params=_cparams("arbitrary", "arbitrary", vmem=VMEM_LIMIT_MOE_BYTES),
        name="moe_down",
    )(block_expert, n_used, n_valid, act, w_down, w_down, b_d3, b_d3)


def _combine_kernel(dest_ref, gate_ref, hx_ref, mod_ref, fg_ref, y_hbm, o_ref, buf, sem, *, tbc, d, nsteps):
    s = pl.program_id(0)
    slot_in = s % 2
    slot_out = (s + 1) % 2
    g2 = mod_ref[0, :, 5 * d:6 * d]
    fg = fg_ref[...]

    def issue(tt):
        r0 = pl.multiple_of(tt * SUBLANES, SUBLANES)
        for k in range(TOP_K):
            tile = buf.at[slot_in, k, pl.ds(r0, SUBLANES), :]
            for r in range(SUBLANES):
                dd = dest_ref[(tt * SUBLANES + r) * TOP_K + k]
                pltpu.make_async_copy(y_hbm.at[pl.ds(dd, 1), :], tile.at[pl.ds(r, 1), :],
                                      sem.at[slot_in]).start(priority=k % 2)

    def reduce_rows(r0, nr):
        gate = gate_ref[pl.ds(r0, nr), :]
        moe_a, moe_b = None, None
        for k in range(TOP_K):
            ya, yb = _unpack_halves(buf[slot_out, k, pl.ds(r0, nr), :])
            gk = gate[:, k:k + 1]
            moe_a = gk * ya if k == 0 else moe_a + gk * ya
            moe_b = gk * yb if k == 0 else moe_b + gk * yb
        moe = jnp.concatenate([moe_a, moe_b], axis=-1)
        hx = hx_ref[pl.ds(r0, nr), :] + g2 * moe
        o_ref[pl.ds(r0, nr), :] = _rms(hx) * fg

    def wait_prev():
        for k in range(TOP_K):
            pltpu.make_async_copy(y_hbm.at[pl.ds(0, tbc), :], buf.at[slot_out, k], sem.at[slot_out]).wait()

    @pl.when(s == 0)
    def _():
        def body(tt, c):
            issue(tt)
            return c
        lax.fori_loop(0, tbc // SUBLANES, body, 0)

    @pl.when((s > 0) & (s < nsteps))
    def _():
        wait_prev()

        def body(tt, c):
            issue(tt)
            reduce_rows(pl.multiple_of(tt * SUBLANES, SUBLANES), SUBLANES)
            return c
        lax.fori_loop(0, tbc // SUBLANES, body, 0, unroll=16 if tbc % (16 * SUBLANES) == 0 else 1)

    @pl.when(s == nsteps)
    def _():
        wait_prev()
        reduce_rows(0, tbc)


def _combine_call(dest_flat, gate, hx1, mod3, rows_per_batch, final_g, y, tbc):
    t, d = hx1.shape
    bpb = rows_per_batch // tbc
    nsteps = t // tbc
    kern = functools.partial(_combine_kernel, tbc=tbc, d=d, nsteps=nsteps)
    prev = lambda s: jnp.maximum(s - 1, 0)
    return pl.pallas_call(
        kern,
        out_shape=jax.ShapeDtypeStruct((t, d), F32),
        grid=(nsteps + 1,),
        in_specs=[pl.BlockSpec((tbc * TOP_K,), lambda s: (jnp.minimum(s, nsteps - 1),), memory_space=pltpu.SMEM),
                  pl.BlockSpec((tbc, TOP_K), lambda s: (prev(s), 0)),
                  pl.BlockSpec((tbc, d), lambda s: (prev(s), 0)),
                  pl.BlockSpec((1, 1, N_MOD * d), lambda s: (prev(s) // bpb, 0, 0)),
                  pl.BlockSpec((1, d), lambda s: (0, 0)),
                  pl.BlockSpec(memory_space=pl.ANY)],
        out_specs=pl.BlockSpec((tbc, d), lambda s: (prev(s), 0)),
        scratch_shapes=[pltpu.VMEM((2, TOP_K, tbc, d // 2), U32), pltpu.SemaphoreType.DMA((2,))],
        compiler_params=_cparams("arbitrary"),
        name="combine",
    )(dest_flat, gate, hx1, mod3, final_g.reshape(1, d), y)


def _tiles(t_tokens, seq):
    tok = 512 if seq % 512 == 0 else seq
    moe_tm = 1024 if t_tokens * TOP_K >= 32 * 1024 else 256
    big = 1024 if seq % 1024 == 0 else tok
    return dict(tok=tok, route=big, moe_tm=moe_tm, disp=big, comb=tok)


def kernel(x, c, ctx, c_ctx, w_mod, b_mod, norm1_g, norm2_g, w_in, lru_conv_w, lru_conv_b, lru_wa, lru_ba,
           lru_wi, lru_bi, lru_lambda, hy_conv_w, hy_conv_b, hy_w1, hy_b1, hy_w2, hy_b2, hy_w3, hy_b3,
           hy_freq, hy_w4, hy_d, gn_lru, gn_hy, w_out, router_w, router_b, exp_w_gu, exp_b_gu,
           exp_w_down, exp_b_down, final_g):
    depth = w_mod.shape[0]
    assert depth == 1, "single-layer stack: the context stream only feeds the latent scan states"
    n_batch, seq, d = x.shape
    ctx_len = ctx.shape[1]
    w_lru = lru_conv_w.shape[-1]
    w_hy = hy_d.shape[-1]
    nb_l, nb_h = w_lru // LANES, w_hy // LANES
    assert lru_wa.shape[2] == nb_l and lru_wa.shape[3] == LANES, "gate blocks must be 128 wide"
    assert seq % (GRID_W * SUBLANES) == 0
    ne = router_w.shape[-1]
    t_tokens = n_batch * seq
    tl = _tiles(t_tokens, seq)
    l = 0

    n_rows = -(-(n_batch + 1) // SUBLANES) * SUBLANES
    cstack = jnp.concatenate([c, c_ctx[None, :], jnp.zeros((n_rows - n_batch - 1, d), F32)], axis=0)
    mod = _mod_call(cstack, w_mod[l], b_mod[l])
    mod3 = mod.reshape(n_rows, 1, N_MOD * d)

    w_in_b = w_in[l].astype(BF16)
    x2d = x.reshape(t_tokens, d)
    bpb = seq // tl["tok"]
    p_x = _inproj_call(x2d, mod3, lambda i: i // bpb, norm1_g[l], w_in_b, tl["tok"])
    ctx_tm = ctx_len if ctx_len <= 512 else 256
    p_c = _inproj_call(ctx.reshape(n_batch * ctx_len, d), mod3, lambda i: n_batch, norm1_g[l],
                       w_in_b[:, :2 * w_lru], ctx_tm)

    lru_args = (lru_conv_w[l], lru_conv_b[l], lru_wa[l], lru_ba[l], lru_wi[l], lru_bi[l], lru_lambda[l])
    _, h_ctx = _lru_call(p_c, nb_l, n_batch, ctx_len, *lru_args, jnp.zeros((n_batch, 2, w_lru), F32))
    y_lru, _ = _lru_call(p_x, nb_l, n_batch, seq, *lru_args, h_ctx)

    rows = seq // GRID_W
    tables = _dft_tables(rows)
    hfilt = _filt_call(seq, w_hy, hy_w1[l], hy_b1[l], hy_w2[l], hy_b2[l], hy_w3[l], hy_b3[l], hy_freq[l], hy_w4[l])
    kf = _spec_call(hfilt, seq, nb_h, tables[0], tables[1])
    cw, cb = hy_conv_w[l], hy_conv_b[l]
    s0 = 2 * nb_l
    z = _hyena_call(p_x, s0, p_x, s0 + nb_h, cw[:, :w_hy], cb[:w_hy], cw[:, w_hy:2 * w_hy], cb[w_hy:2 * w_hy],
                    hy_d[l, 0], kf, 0, tables, n_batch, seq, nb_h, True)
    hy = _hyena_call(z, 0, p_x, s0 + 2 * nb_h, cw[:, :w_hy], cb[:w_hy], cw[:, 2 * w_hy:], cb[2 * w_hy:],
                     hy_d[l, 1], kf, 1, tables, n_batch, seq, nb_h, False)

    hx1, m, logits = _outproj_call(y_lru, hy, x2d, mod3, seq, gn_lru[l], gn_hy[l], norm2_g[l],
                                   w_out[l].astype(BF16), router_w[l], router_b[l], tl["tok"])

    idx, gate, rank, counts = _route_call(logits, tl["route"])
    tm = tl["moe_tm"]
    counts = counts[0]
    pcnt = (counts + tm - 1) // tm * tm
    pend = jnp.cumsum(pcnt)
    pstart = pend - pcnt
    e_ids = jnp.arange(ne, dtype=I32)
    dest = (jnp.sum(jnp.where(idx[:, :, None] == e_ids, pstart.astype(I32), 0), axis=-1) + rank).reshape(-1)
    nblk = t_tokens * TOP_K // tm + ne
    cap = nblk * tm
    n_used = (pend[-1] // tm).astype(I32).reshape(1)
    blk_ids = jnp.arange(nblk, dtype=I32)
    block_expert = jnp.minimum(jnp.sum((pend[None, :] <= (blk_ids * tm)[:, None]).astype(I32), axis=1), ne - 1)
    block_expert = jnp.where(blk_ids < n_used[0], block_expert, block_expert[n_used[0] - 1])
    group_end = (pstart + counts).astype(I32)
    n_valid = jnp.where(blk_ids < n_used[0], jnp.clip(group_end[block_expert] - blk_ids * tm, 0, tm), 0).astype(I32)

    xs = _dispatch_call(dest, pend.astype(I32), pcnt.astype(I32), n_used, m, cap, tm, tl["disp"])
    d_ff = exp_w_down.shape[2]
    tf = next((c for c in (1024, 512, 256) if d_ff % c == 0), d_ff)
    tn = next((c for c in (1024, 512) if (d // 2) % c == 0), d // 2)
    y = _moe_call(block_expert, n_used, n_valid, xs, exp_w_gu[l], exp_b_gu[l], exp_w_down[l], exp_b_down[l],
                  tm, tf, tn)
    out = _combine_call(dest, gate, hx1, mod3, seq, final_g, y, tl["comb"])
    return out.reshape(n_batch, seq, d)
```

```python
import functools
import math

import jax
import jax.numpy as jnp
from jax import lax
from jax.experimental import pallas as pl
from jax.experimental.pallas import tpu as pltpu

F32 = jnp.float32
BF16 = jnp.bfloat16
I32 = jnp.int32
U32 = jnp.uint32
HI = lax.Precision.HIGHEST

GRID_W = 64
EPS = 1e-6
N_MOD = 6
LRU_CONV = 4
LRU_C = 8.0
HY_CONV = 3
HY_ORDER = 2
HY_BANDS = 16
HY_TARGET = 1e-2
HY_FAST = 0.3
HY_SLOW = 1.5
TOP_K = 4
SWIGLU_LIMIT = 7.0
SWIGLU_ALPHA = 1.702

LANES = 128
SUBLANES = 8
VMEM_LIMIT_BYTES = 56 * 1024 * 1024
VMEM_LIMIT_MOE_BYTES = 60 * 1024 * 1024

DFT_N1 = 2 * GRID_W
K1_USED = DFT_N1 // 2 + 1
K1_PAD = 72
assert K1_PAD % SUBLANES == 0 and K1_PAD >= K1_USED
LOOP_UNROLL = 64
K1_UNROLL = 32
assert (K1_USED - 1) % K1_UNROLL == 0


def _cparams(*sem, vmem=VMEM_LIMIT_BYTES):
    return pltpu.CompilerParams(dimension_semantics=sem, vmem_limit_bytes=vmem)


def _rms(x):
    return x * lax.rsqrt(jnp.mean(x * x, axis=-1, keepdims=True) + EPS)


def _pack_pair(a, b):
    wa = pltpu.bitcast(a.astype(BF16).astype(F32), U32)
    wb = pltpu.bitcast(b.astype(BF16).astype(F32), U32)
    return wa | (wb >> 16)


def _pack_halves(v):
    h = v.shape[-1] // 2
    return _pack_pair(v[:, :h], v[:, h:])


def _unpack_halves(w):
    return pltpu.bitcast(w & jnp.uint32(0xFFFF0000), F32), pltpu.bitcast(w << 16, F32)


def _mod_kernel(c_ref, w_ref, b_ref, o_ref):
    c = c_ref[...]
    s = c * jax.nn.sigmoid(c)
    o_ref[...] = jnp.dot(s, w_ref[...], preferred_element_type=F32, precision=HI) + b_ref[...]


def _mod_call(cstack, w_mod, b_mod):
    rows, d = cstack.shape
    n = w_mod.shape[1]
    tn = 1536 if n % 1536 == 0 else n
    return pl.pallas_call(
        _mod_kernel,
        out_shape=jax.ShapeDtypeStruct((rows, n), F32),
        grid=(n // tn,),
        in_specs=[pl.BlockSpec((rows, d), lambda j: (0, 0)),
                  pl.BlockSpec((d, tn), lambda j: (0, j)),
                  pl.BlockSpec((1, tn), lambda j: (0, j))],
        out_specs=pl.BlockSpec((rows, tn), lambda j: (0, j)),
        compiler_params=_cparams("arbitrary"),
        name="mod",
    )(cstack, w_mod, b_mod.reshape(1, n))


def _inproj_kernel(x_ref, mod_ref, g_ref, w_ref, o_ref, *, d, n_slabs, nchunk):
    x = x_ref[...]
    sh = mod_ref[0, :, 0 * d:1 * d]
    sc = mod_ref[0, :, 1 * d:2 * d]
    a = (_rms(x) * g_ref[...]) * (1.0 + sc) + sh
    ab = a.astype(BF16)
    spc = nchunk // LANES
    for j in range(n_slabs // spc):
        p = jnp.dot(ab, w_ref[:, j * nchunk:(j + 1) * nchunk], preferred_element_type=F32)
        for s in range(spc):
            o_ref[j * spc + s] = p[:, s * LANES:(s + 1) * LANES].astype(BF16)


def _inproj_call(x2d, mod3, mod_row_of_block, g, w_bf16, tm):
    t, d = x2d.shape
    n = w_bf16.shape[1]
    n_slabs = n // LANES
    nchunk = 512 if n % 512 == 0 else LANES
    kern = functools.partial(_inproj_kernel, d=d, n_slabs=n_slabs, nchunk=nchunk)
    return pl.pallas_call(
        kern,
        out_shape=jax.ShapeDtypeStruct((n_slabs, t, LANES), BF16),
        grid=(t // tm,),
        in_specs=[pl.BlockSpec((tm, d), lambda i: (i, 0)),
                  pl.BlockSpec((1, 1, N_MOD * d), lambda i: (mod_row_of_block(i), 0, 0)),
                  pl.BlockSpec((1, d), lambda i: (0, 0)),
                  pl.BlockSpec((d, n), lambda i: (0, 0), pipeline_mode=pl.Buffered(1))],
        out_specs=pl.BlockSpec((n_slabs, tm, LANES), lambda i: (0, i, 0)),
        compiler_params=_cparams("arbitrary"),
        name="inproj",
    )(x2d, mod3, g.reshape(1, d), w_bf16)


def _gelu_tanh(x):
    return 0.5 * x * (1.0 + jnp.tanh(math.sqrt(2.0 / math.pi) * (x + 0.044715 * (x * x * x))))


def _scan_tile(a, b, row, reverse):
    for s in (1, 2, 4):
        if reverse:
            m = row < SUBLANES - s
            sh = SUBLANES - s
        else:
            m = row >= s
            sh = s
        a_s = jnp.where(m, pltpu.roll(a, sh, 0), 1.0)
        b_s = jnp.where(m, pltpu.roll(b, sh, 0), 0.0)
        b = a * b_s + b
        a = a * a_s
    return a, b


def _lru_kernel(r_ref, g_ref, cw_ref, cb_ref, wa_ref, wi_ref, ba_ref, bi_ref, lam_ref, h0_ref,
                y_ref, hl_ref, xp_ref, hf_ref, ab_ref, bb_ref, *, t_len, tc):
    nc = t_len // tc
    ntile = tc // SUBLANES
    zero8 = jnp.zeros((SUBLANES, LANES), F32)
    xp_ref[0:SUBLANES, :] = zero8
    xp_ref[t_len + SUBLANES:t_len + 2 * SUBLANES, :] = zero8

    def copy_body(ci, c):
        t0 = pl.multiple_of(ci * tc, tc)
        xp_ref[pl.ds(t0 + SUBLANES, tc), :] = r_ref[0, pl.ds(t0, tc), :].astype(F32)
        return c
    lax.fori_loop(0, nc, copy_body, 0)

    row = lax.broadcasted_iota(I32, (SUBLANES, LANES), 0)
    cw = cw_ref[...]
    cb = cb_ref[...]
    lam = lam_ref[...]
    sp = jnp.log1p(jnp.exp(-lam))
    wab = [wa_ref[d, 0].astype(BF16) for d in range(2)]
    wib = [wi_ref[d, 0].astype(BF16) for d in range(2)]

    def coeffs(u, ub, d):
        ga = jnp.dot(ub, wab[d], preferred_element_type=F32) + ba_ref[d:d + 1, :]
        gi = jnp.dot(ub, wib[d], preferred_element_type=F32) + bi_ref[d:d + 1, :]
        rg = jax.nn.sigmoid(ga)
        ig = jax.nn.sigmoid(gi)
        a = jnp.exp((-LRU_C) * rg * sp[d:d + 1, :])
        b = jnp.sqrt(1.0 - a * a) * (ig * u)
        return a, b

    def fwd_body(ci, hcar):
        t0 = pl.multiple_of(ci * tc, tc)
        u = (cw[0:1, :] * xp_ref[pl.ds(t0 + SUBLANES - 2, tc), :] + cw[1:2, :] * xp_ref[pl.ds(t0 + SUBLANES - 1, tc), :]
             + cw[2:3, :] * xp_ref[pl.ds(t0 + SUBLANES, tc), :] + cw[3:4, :] * xp_ref[pl.ds(t0 + SUBLANES + 1, tc), :]) + cb
        ub = u.astype(BF16)
        a0, b0 = coeffs(u, ub, 0)
        a1, b1 = coeffs(u, ub, 1)
        ab_ref[pl.ds(t0, tc), :] = a1
        bb_ref[pl.ds(t0, tc), :] = b1
        for j in range(ntile):
            at, bt = _scan_tile(a0[j * 8:(j + 1) * 8], b0[j * 8:(j + 1) * 8], row, False)
            hf_ref[pl.ds(t0 + j * 8, 8), :] = at * hcar + bt
            atot = jnp.broadcast_to(at[7:8, :], (SUBLANES, LANES))
            btot = jnp.broadcast_to(bt[7:8, :], (SUBLANES, LANES))
            hcar = atot * hcar + btot
        return hcar

    h0f = jnp.broadcast_to(h0_ref[0, 0:1, :], (SUBLANES, LANES))
    hfin = lax.fori_loop(0, nc, fwd_body, h0f, unroll=4 if nc % 4 == 0 else 1)
    hl_ref[0, 0:1, :] = hfin[0:1, :]

    def bwd_body(k, hcar):
        ci = nc - 1 - k
        t0 = pl.multiple_of(ci * tc, tc)
        a1 = ab_ref[pl.ds(t0, tc), :]
        b1 = bb_ref[pl.ds(t0, tc), :]
        gg = g_ref[0, pl.ds(t0, tc), :].astype(F32)
        gl = _gelu_tanh(gg)
        for j in reversed(range(ntile)):
            at, bt = _scan_tile(a1[j * 8:(j + 1) * 8], b1[j * 8:(j + 1) * 8], row, True)
            hb = at * hcar + bt
            hf = hf_ref[pl.ds(t0 + j * 8, 8), :]
            y_ref[0, pl.ds(t0 + j * 8, 8), :] = ((hf + hb) * gl[j * 8:(j + 1) * 8]).astype(y_ref.dtype)
            atot = jnp.broadcast_to(at[0:1, :], (SUBLANES, LANES))
            btot = jnp.broadcast_to(bt[0:1, :], (SUBLANES, LANES))
            hcar = atot * hcar + btot
        return hcar

    h0b = jnp.broadcast_to(h0_ref[0, 1:2, :], (SUBLANES, LANES))
    hfin_b = lax.fori_loop(0, nc, bwd_body, h0b, unroll=2 if nc % 2 == 0 else 1)
    hl_ref[0, 1:2, :] = hfin_b[0:1, :]


def _lru_call(p_slabs, nb, n_batch, t_len, conv_w, conv_b, wa, ba, wi, bi, lam, h0):
    tc = 128 if t_len % 128 == 0 else t_len
    kern = functools.partial(_lru_kernel, t_len=t_len, tc=tc)
    w_lru = nb * LANES
    return pl.pallas_call(
        kern,
        out_shape=[jax.ShapeDtypeStruct((nb, n_batch * t_len, LANES), F32),
                   jax.ShapeDtypeStruct((n_batch, 2, w_lru), F32)],
        grid=(nb, n_batch),
        in_specs=[pl.BlockSpec((1, t_len, LANES), lambda n, b: (n, b, 0)),
                  pl.BlockSpec((1, t_len, LANES), lambda n, b: (nb + n, b, 0)),
                  pl.BlockSpec((LRU_CONV, LANES), lambda n, b: (0, n)),
                  pl.BlockSpec((1, LANES), lambda n, b: (0, n)),
                  pl.BlockSpec((2, 1, LANES, LANES), lambda n, b: (0, n, 0, 0)),
                  pl.BlockSpec((2, 1, LANES, LANES), lambda n, b: (0, n, 0, 0)),
                  pl.BlockSpec((2, LANES), lambda n, b: (0, n)),
                  pl.BlockSpec((2, LANES), lambda n, b: (0, n)),
                  pl.BlockSpec((2, LANES), lambda n, b: (0, n)),
                  pl.BlockSpec((1, 2, LANES), lambda n, b: (b, 0, n))],
        out_specs=[pl.BlockSpec((1, t_len, LANES), lambda n, b: (n, b, 0)),
                   pl.BlockSpec((1, 2, LANES), lambda n, b: (b, 0, n))],
        scratch_shapes=[pltpu.VMEM((t_len + 2 * SUBLANES, LANES), F32),
                        pltpu.VMEM((t_len, LANES), F32),
                        pltpu.VMEM((t_len, LANES), F32),
                        pltpu.VMEM((t_len, LANES), F32)],
        compiler_params=_cparams("arbitrary", "arbitrary"),
        name="lru",
    )(p_slabs, p_slabs, conv_w, conv_b.reshape(1, w_lru), wa, wi, ba, bi, lam, h0)


def _filt_kernel(w1_ref, b1_ref, w2_ref, b2_ref, w3_ref, b3_ref, fr_ref, w4_ref, o_ref,
                 *, l_len, rows, tl, w_hy):
    i = pl.program_id(0)
    rho = i * tl + lax.broadcasted_iota(I32, (tl, 1), 0)
    tidx = ((rho % GRID_W) * rows + rho // GRID_W).astype(F32)
    tt = tidx * (1.0 / (l_len - 1))
    ww = (2.0 * math.pi) * tidx / l_len
    band = lax.broadcasted_iota(I32, (1, HY_BANDS), 1).astype(F32)
    f = 1e-4 + band * ((HY_BANDS - 1 - 1e-4) / (HY_BANDS - 1))
    fw = f * ww
    w1 = w1_ref[...]
    pre = (tt * w1[0:1, :]
           + jnp.dot(jnp.cos(fw), w1[1:1 + HY_BANDS, :], preferred_element_type=F32, precision=HI)
           + jnp.dot(-jnp.sin(fw), w1[1 + HY_BANDS:1 + 2 * HY_BANDS, :], preferred_element_type=F32, precision=HI))
    fr = fr_ref[...]
    h = jnp.sin(fr * (pre + b1_ref[...]))
    h = jnp.sin(fr * (jnp.dot(h, w2_ref[...], preferred_element_type=F32, precision=HI) + b2_ref[...]))
    h = jnp.sin(fr * (jnp.dot(h, w3_ref[...], preferred_element_type=F32, precision=HI) + b3_ref[...]))
    max_decay = math.log(HY_TARGET) / HY_FAST
    min_decay = math.log(HY_TARGET) / HY_SLOW
    nblk = w_hy // LANES
    h_hi = h.astype(BF16)
    h_lo = (h - h_hi.astype(F32)).astype(BF16)
    for cb in range(nblk):
        ch = (cb * LANES + lax.broadcasted_iota(I32, (1, LANES), 1)).astype(F32)
        delta = jnp.abs(min_decay + ch * ((max_decay - min_decay) / (w_hy - 1)))
        decay = jnp.exp(-tt * delta)
        for od in range(HY_ORDER * 2):
            col = od * w_hy + cb * LANES
            w4 = w4_ref[:, col:col + LANES]
            w_hi = w4.astype(BF16)
            w_lo = (w4 - w_hi.astype(F32)).astype(BF16)
            v = (jnp.dot(h_hi, w_hi, preferred_element_type=F32) + jnp.dot(h_lo, w_hi, preferred_element_type=F32)
                 + jnp.dot(h_hi, w_lo, preferred_element_type=F32))
            o_ref[od * nblk + cb] = v * decay


def _filt_call(l_len, w_hy, w1, b1, w2, b2, w3, b3, freq, w4):
    rows = l_len // GRID_W
    tl = 512 if l_len % 512 == 0 else l_len
    fh = w2.shape[0]
    nslab = HY_ORDER * 2 * (w_hy // LANES)
    kern = functools.partial(_filt_kernel, l_len=l_len, rows=rows, tl=tl, w_hy=w_hy)
    full = lambda a: pl.BlockSpec(a.shape, lambda i: (0,) * a.ndim)
    args = (w1, b1.reshape(1, fh), w2, b2.reshape(1, fh), w3, b3.reshape(1, fh), freq.reshape(1, fh), w4)
    return pl.pallas_call(
        kern,
        out_shape=jax.ShapeDtypeStruct((nslab, l_len, LANES), F32),
        grid=(l_len // tl,),
        in_specs=[full(a) for a in args],
        out_specs=pl.BlockSpec((nslab, tl, LANES), lambda i: (0, i, 0)),
        compiler_params=_cparams("arbitrary"),
        name="filt",
    )(*args)


def _dft_tables(rows):
    n_len = DFT_N1 * rows
    k1 = jnp.arange(K1_PAD, dtype=I32)
    r = jnp.arange(rows, dtype=I32)
    w = jnp.arange(GRID_W, dtype=I32)
    idx = (k1[None, :, None] * (r[:, None, None] + rows * w[None, None, :])) % n_len
    ang = idx.astype(F32) * (2.0 * math.pi / n_len)
    valid = (k1 < K1_USED)[None, :, None]
    ca = jnp.where(valid, jnp.cos(ang), 0.0)
    sa = jnp.where(valid, jnp.sin(ang), 0.0)
    ta = jnp.stack([ca, -sa], axis=2).reshape(rows, 2 * K1_PAD, GRID_W)
    td = jnp.transpose(ta, (0, 2, 1))
    k2 = jnp.arange(rows, dtype=I32)
    th = ((k2[:, None] * r[None, :]) % rows).astype(F32) * (2.0 * math.pi / rows)
    c, s = jnp.cos(th), jnp.sin(th)
    tb = jnp.concatenate([jnp.stack([c, s], axis=2).reshape(rows, 2 * rows),
                          jnp.stack([-s, c], axis=2).reshape(rows, 2 * rows)], axis=0)
    tcm = jnp.stack([jnp.concatenate([c, -s], axis=1), jnp.concatenate([s, c], axis=1)], axis=1).reshape(2 * rows, 2 * rows)
    return ta.astype(BF16), tb.astype(BF16), tcm.astype(BF16), td.astype(BF16)


def _pitch_r(rows):
    return rows + SUBLANES


def _stage_a(tile_fn, ta_ref, s1_ref, rows):
    pitch = _pitch_r(rows)

    def body(r, c):
        out = jnp.dot(ta_ref[r], tile_fn(r), preferred_element_type=F32)
        words = pltpu.bitcast(out.astype(BF16), U32)
        s1_ref[pl.ds(r, K1_PAD, stride=pitch), :] = words
        return c
    lax.fori_loop(0, rows, body, 0, unroll=min(LOOP_UNROLL, rows))


def _loop_k1(body):
    lax.fori_loop(0, K1_USED - 1, body, 0, unroll=K1_UNROLL)
    body(K1_USED - 1, 0)


def _stage_b(k1, tb_ref, s1_ref, rows):
    pitch = _pitch_r(rows)
    t = s1_ref[pl.ds(pl.multiple_of(k1 * pitch, SUBLANES), rows), :]
    y = jnp.dot(tb_ref[...], pltpu.bitcast(t, BF16), preferred_element_type=F32)
    return y[:rows], y[rows:]


def _spec_kernel(hf_ref, hb_ref, ta_ref, tb_ref, kf_ref, s1_ref, *, rows):
    n_len = DFT_N1 * rows

    def scale(k1):
        return jnp.where((k1 == 0) | (k1 == K1_USED - 1), 1.0, 2.0) * (1.0 / n_len)

    _stage_a(lambda r: hf_ref[0, pl.ds(pl.multiple_of(r * GRID_W, GRID_W), GRID_W), :].astype(BF16),
             ta_ref, s1_ref, rows)

    def b_fwd(k1, c):
        yre, yim = _stage_b(k1, tb_ref, s1_ref, rows)
        sc = scale(k1)
        kf_ref[0, 0, k1, 0] = yre * sc
        kf_ref[0, 0, k1, 1] = yim * sc
        return c
    _loop_k1(b_fwd)

    rid = lax.broadcasted_iota(I32, (GRID_W, LANES), 0)

    def tile_b(r):
        x = hb_ref[0, pl.ds(pl.multiple_of(r * GRID_W, GRID_W), GRID_W), :]
        x = jnp.where((rid == 0) & (r == 0), 0.0, x)
        return x.astype(BF16)
    _stage_a(tile_b, ta_ref, s1_ref, rows)

    def b_bwd(k1, c):
        yre, yim = _stage_b(k1, tb_ref, s1_ref, rows)
        sc = scale(k1)
        kf_ref[0, 0, k1, 0] += yre * sc
        kf_ref[0, 0, k1, 1] -= yim * sc
        return c
    _loop_k1(b_bwd)


def _spec_call(hfilt, l_len, nblk, ta, tb):
    rows = l_len // GRID_W
    kern = functools.partial(_spec_kernel, rows=rows)
    return pl.pallas_call(
        kern,
        out_shape=jax.ShapeDtypeStruct((HY_ORDER, nblk, K1_USED, 2, rows, LANES), F32),
        grid=(HY_ORDER, nblk),
        in_specs=[pl.BlockSpec((1, l_len, LANES), lambda o, c: (o * 2 * nblk + c, 0, 0)),
                  pl.BlockSpec((1, l_len, LANES), lambda o, c: (o * 2 * nblk + nblk + c, 0, 0)),
                  pl.BlockSpec(ta.shape, lambda o, c: (0, 0, 0), pipeline_mode=pl.Buffered(1)),
                  pl.BlockSpec(tb.shape, lambda o, c: (0, 0), pipeline_mode=pl.Buffered(1))],
        out_specs=pl.BlockSpec((1, 1, K1_USED, 2, rows, LANES), lambda o, c: (o, c, 0, 0, 0, 0)),
        scratch_shapes=[pltpu.VMEM((K1_PAD * _pitch_r(rows), LANES), U32)],
        compiler_params=_cparams("arbitrary", "arbitrary"),
        name="spec",
    )(hfilt, hfilt, ta, tb)


def _conv_grid_order(src_ref, dst_ref, w_ref, b_ref, rows):
    w0, w1, w2, bb = w_ref[0:1, :], w_ref[1:2, :], w_ref[2:3, :], b_ref[...]
    g = GRID_W
    rid = lax.broadcasted_iota(I32, (g, LANES), 0)

    def grp(r0):
        return src_ref[0, pl.ds(r0, g), :].astype(F32)

    last = grp((rows - 1) * g)
    prev0 = jnp.where(rid >= 1, pltpu.roll(last, 1, 0), 0.0)
    dst_ref[pl.ds(0, g), :] = (w0 * prev0 + w1 * grp(0) + w2 * grp(g) + bb).astype(dst_ref.dtype)
    first = grp(0)
    nxt = jnp.where(rid < g - 1, pltpu.roll(first, g - 1, 0), 0.0)
    dst_ref[pl.ds((rows - 1) * g, g), :] = (w0 * grp((rows - 2) * g) + w1 * last + w2 * nxt + bb).astype(dst_ref.dtype)

    def body(r, c):
        rm = pl.multiple_of((r - 1) * g, g)
        r0 = pl.multiple_of(r * g, g)
        rp = pl.multiple_of((r + 1) * g, g)
        v = w0 * grp(rm) + w1 * grp(r0) + w2 * grp(rp) + bb
        dst_ref[pl.ds(r0, g), :] = v.astype(dst_ref.dtype)
        return c
    n_mid = rows - 2
    lax.fori_loop(1, rows - 1, body, 0, unroll=next(u for u in (6, 4, 2, 1) if n_mid % u == 0))


def _hyena_kernel(sig_ref, gate_ref, cws_ref, cbs_ref, cwg_ref, cbg_ref, d_ref, kf_ref,
                  ta_ref, tb_ref, tc_ref, td_ref, o_ref, u_ref, gt_ref, s1_ref, s2_ref,
                  *, rows, conv_sig):
    g = GRID_W

    @pl.when((pl.program_id(0) == 0) & (pl.program_id(1) == 0))
    def _():
        s2_ref[...] = jnp.zeros(s2_ref.shape, U32)

    if conv_sig:
        _conv_grid_order(sig_ref, u_ref, cws_ref, cbs_ref, rows)
    else:
        def cp(r, c):
            r0 = pl.multiple_of(r * g, g)
            u_ref[pl.ds(r0, g), :] = sig_ref[0, pl.ds(r0, g), :]
            return c
        lax.fori_loop(0, rows, cp, 0, unroll=min(LOOP_UNROLL, rows))
    _conv_grid_order(gate_ref, gt_ref, cwg_ref, cbg_ref, rows)

    _stage_a(lambda r: u_ref[pl.ds(pl.multiple_of(r * g, g), g), :], ta_ref, s1_ref, rows)

    def bc_body(k1, c):
        yre, yim = _stage_b(k1, tb_ref, s1_ref, rows)
        kre = kf_ref[0, 0, k1, 0]
        kim = kf_ref[0, 0, k1, 1]
        zre = yre * kre - yim * kim
        zim = yre * kim + yim * kre
        z = jnp.concatenate([zre, zim], axis=0).astype(BF16)
        v = jnp.dot(tc_ref[...], z, preferred_element_type=F32)
        s2_ref[pl.ds(k1, rows, stride=K1_PAD), :] = pltpu.bitcast(v.astype(BF16), U32)
        return c
    _loop_k1(bc_body)

    dd = d_ref[...]

    def d_body(r, c):
        r0 = pl.multiple_of(r * g, g)
        t = s2_ref[pl.ds(pl.multiple_of(r * K1_PAD, SUBLANES), K1_PAD), :]
        y = jnp.dot(td_ref[r], pltpu.bitcast(t, BF16), preferred_element_type=F32)
        uu = u_ref[pl.ds(r0, g), :].astype(F32)
        o_ref[0, pl.ds(r0, g), :] = (gt_ref[pl.ds(r0, g), :] * (y + dd * uu)).astype(o_ref.dtype)
        return c
    lax.fori_loop(0, rows, d_body, 0, unroll=min(LOOP_UNROLL, rows))


def _hyena_call(sig_arr, sig_slab0, gate_arr, gate_slab0, conv_w_sig, conv_b_sig, conv_w_gate, conv_b_gate,
                d_vec, kf, order, tables, n_batch, l_len, nblk, conv_sig):
    rows = l_len // GRID_W
    ta, tb, tcm, td = tables
    kern = functools.partial(_hyena_kernel, rows=rows, conv_sig=conv_sig)
    w_hy = nblk * LANES
    const = lambda a: pl.BlockSpec(a.shape, lambda c, b: (0,) * a.ndim, pipeline_mode=pl.Buffered(1))
    return pl.pallas_call(
        kern,
        out_shape=jax.ShapeDtypeStruct((nblk, n_batch * l_len, LANES), BF16),
        grid=(nblk, n_batch),
        in_specs=[pl.BlockSpec((1, l_len, LANES), lambda c, b: (sig_slab0 + c, b, 0)),
                  pl.BlockSpec((1, l_len, LANES), lambda c, b: (gate_slab0 + c, b, 0)),
                  pl.BlockSpec((HY_CONV, LANES), lambda c, b: (0, c)),
                  pl.BlockSpec((1, LANES), lambda c, b: (0, c)),
                  pl.BlockSpec((HY_CONV, LANES), lambda c, b: (0, c)),
                  pl.BlockSpec((1, LANES), lambda c, b: (0, c)),
                  pl.BlockSpec((1, LANES), lambda c, b: (0, c)),
                  pl.BlockSpec((1, 1, K1_USED, 2, rows, LANES), lambda c, b: (order, c, 0, 0, 0, 0),
                               pipeline_mode=pl.Buffered(1)),
                  const(ta), const(tb), const(tcm), const(td)],
        out_specs=pl.BlockSpec((1, l_len, LANES), lambda c, b: (c, b, 0)),
        scratch_shapes=[pltpu.VMEM((l_len, LANES), BF16),
                        pltpu.VMEM((l_len, LANES), F32),
                        pltpu.VMEM((K1_PAD * _pitch_r(rows), LANES), U32),
                        pltpu.VMEM((rows * K1_PAD, LANES), U32)],
        compiler_params=_cparams("arbitrary", "arbitrary"),
        name="hyena%d" % order,
    )(sig_arr, gate_arr, conv_w_sig, conv_b_sig.reshape(1, w_hy), conv_w_gate, conv_b_gate.reshape(1, w_hy),
      d_vec.reshape(1, w_hy), kf, ta, tb, tcm, td)


def _outproj_kernel(yl_ref, hy_ref, x_ref, mod_ref, gl_ref, gh_ref, n2_ref, w_ref, rw2_ref, rb_ref,
                    hx_ref, m_ref, lg_ref, *, d, nb_l, nb_h, ne):
    yl = jnp.concatenate([yl_ref[j].astype(F32) for j in range(nb_l)], axis=-1)
    hy = jnp.concatenate([hy_ref[j].astype(F32) for j in range(nb_h)], axis=-1)
    cat = jnp.concatenate([_rms(yl) * gl_ref[...], _rms(hy) * gh_ref[...]], axis=-1).astype(BF16)
    out = jnp.dot(cat, w_ref[...], preferred_element_type=F32)
    g1 = mod_ref[0, :, 2 * d:3 * d]
    sh2 = mod_ref[0, :, 3 * d:4 * d]
    sc2 = mod_ref[0, :, 4 * d:5 * d]
    hx = x_ref[...] + g1 * out
    hx_ref[...] = hx
    m = (_rms(hx) * n2_ref[...]) * (1.0 + sc2) + sh2
    m_ref[...] = _pack_halves(m)
    m_hi = m.astype(BF16)
    m_lo = (m - m_hi.astype(F32)).astype(BF16)
    a = jnp.dot(m_hi, rw2_ref[...], preferred_element_type=F32)
    b = jnp.dot(m_lo, rw2_ref[:, :ne], preferred_element_type=F32)
    lg_ref[...] = (a[:, :ne] + b) + a[:, ne:] + rb_ref[...]


def _outproj_call(y_lru, hy, x2d, mod3, rows_per_batch, gn_lru, gn_hy, norm2_g, w_out_bf16, router_w, router_b, tm):
    t, d = x2d.shape
    nb_l, nb_h = y_lru.shape[0], hy.shape[0]
    ne = router_w.shape[1]
    bpb = rows_per_batch // tm
    rw_hi = router_w.astype(BF16)
    rw_lo = (router_w - rw_hi.astype(F32)).astype(BF16)
    rw2 = jnp.concatenate([rw_hi, rw_lo], axis=1)
    kern = functools.partial(_outproj_kernel, d=d, nb_l=nb_l, nb_h=nb_h, ne=ne)
    return pl.pallas_call(
        kern,
        out_shape=[jax.ShapeDtypeStruct((t, d), F32), jax.ShapeDtypeStruct((t, d // 2), U32),
                   jax.ShapeDtypeStruct((t, ne), F32)],
        grid=(t // tm,),
        in_specs=[pl.BlockSpec((nb_l, tm, LANES), lambda i: (0, i, 0)),
                  pl.BlockSpec((nb_h, tm, LANES), lambda i: (0, i, 0)),
                  pl.BlockSpec((tm, d), lambda i: (i, 0)),
                  pl.BlockSpec((1, 1, N_MOD * d), lambda i: (i // bpb, 0, 0)),
                  pl.BlockSpec((1, nb_l * LANES), lambda i: (0, 0)),
                  pl.BlockSpec((1, nb_h * LANES), lambda i: (0, 0)),
                  pl.BlockSpec((1, d), lambda i: (0, 0)),
                  pl.BlockSpec(w_out_bf16.shape, lambda i: (0, 0), pipeline_mode=pl.Buffered(1)),
                  pl.BlockSpec((d, 2 * ne), lambda i: (0, 0)),
                  pl.BlockSpec((1, ne), lambda i: (0, 0))],
        out_specs=[pl.BlockSpec((tm, d), lambda i: (i, 0)),
                   pl.BlockSpec((tm, d // 2), lambda i: (i, 0)),
                   pl.BlockSpec((tm, ne), lambda i: (i, 0))],
        compiler_params=_cparams("arbitrary"),
        name="outproj",
    )(y_lru, hy, x2d, mod3, gn_lru.reshape(1, -1), gn_hy.reshape(1, -1), norm2_g.reshape(1, d),
      w_out_bf16, rw2, router_b.reshape(1, ne))


def _route_kernel(lg_ref, idx_ref, gate_ref, rank_ref, cnt_ref, run_ref, *, tb, ne):
    @pl.when(pl.program_id(0) == 0)
    def _():
        run_ref[...] = jnp.zeros(run_ref.shape, F32)

    l = lg_ref[...]
    lane = lax.broadcasted_iota(I32, (tb, ne), 1)
    vals, idxs, ohs = [], [], []
    for _ in range(TOP_K):
        m = jnp.max(l, axis=-1, keepdims=True)
        ix = jnp.min(jnp.where(l == m, lane, ne), axis=-1, keepdims=True)
        sel = lane == ix
        vals.append(m)
        idxs.append(ix)
        ohs.append(sel.astype(F32))
        l = jnp.where(sel, -jnp.inf, l)
    es = [jnp.exp(v - vals[0]) for v in vals]
    den = es[0] + es[1] + es[2] + es[3]
    oh_all = ohs[0] + ohs[1] + ohs[2] + ohs[3]
    ri = lax.broadcasted_iota(I32, (tb, tb), 0)
    ci = lax.broadcasted_iota(I32, (tb, tb), 1)
    ltri = (ci < ri).astype(BF16)
    before = jnp.dot(ltri, oh_all.astype(BF16), preferred_element_type=F32) + run_ref[...]
    for k in range(TOP_K):
        idx_ref[:, k:k + 1] = idxs[k]
        gate_ref[:, k:k + 1] = es[k] / den
        rank_ref[:, k:k + 1] = jnp.sum(ohs[k] * before, axis=-1, keepdims=True).astype(I32)
    run_ref[...] += jnp.sum(oh_all, axis=0, keepdims=True)
    cnt_ref[...] = run_ref[...].astype(I32)


def _route_call(logits, tb):
    t, ne = logits.shape
    kern = functools.partial(_route_kernel, tb=tb, ne=ne)
    blk = lambda: pl.BlockSpec((tb, TOP_K), lambda i: (i, 0))
    return pl.pallas_call(
        kern,
        out_shape=[jax.ShapeDtypeStruct((t, TOP_K), I32), jax.ShapeDtypeStruct((t, TOP_K), F32),
                   jax.ShapeDtypeStruct((t, TOP_K), I32), jax.ShapeDtypeStruct((1, ne), I32)],
        grid=(t // tb,),
        in_specs=[pl.BlockSpec((tb, ne), lambda i: (i, 0))],
        out_specs=[blk(), blk(), blk(), pl.BlockSpec((1, ne), lambda i: (0, 0))],
        scratch_shapes=[pltpu.VMEM((1, ne), F32)],
        compiler_params=_cparams("arbitrary"),
        name="route",
    )(logits)


def _dispatch_kernel(dest_ref, pend_ref, pcnt_ref, nu_ref, m_ref, xs_hbm, zbuf, sem, zsem, *, tbd, tm, ne, nblk):
    i = pl.program_id(0)

    @pl.when(i == 0)
    def _():
        zbuf[...] = jnp.zeros(zbuf.shape, U32)

        def zero_copy(row0):
            return pltpu.make_async_copy(zbuf, xs_hbm.at[pl.ds(pl.multiple_of(row0, tm), tm), :], zsem)

        def zfill(wait):
            def body(e, c):
                @pl.when(pcnt_ref[e] > 0)
                def _():
                    cp = zero_copy(pend_ref[e] - tm)
                    cp.wait() if wait else cp.start()
                return c
            return body

        def ztail(wait):
            def body(bi, c):
                cp = zero_copy(bi * tm)
                cp.wait() if wait else cp.start()
                return c
            return body

        for wait in (False, True):
            lax.fori_loop(0, ne, zfill(wait), 0)
            lax.fori_loop(nu_ref[0], nblk, ztail(wait), 0)

    def body(tt, c):
        tile = m_ref.at[pl.ds(pl.multiple_of(tt * SUBLANES, SUBLANES), SUBLANES), :]
        for s in range(SUBLANES):
            for k in range(TOP_K):
                d = dest_ref[(tt * SUBLANES + s) * TOP_K + k]
                pltpu.make_async_copy(tile.at[pl.ds(s, 1), :], xs_hbm.at[pl.ds(d, 1), :], sem).start(priority=k % 2)
        return c
    lax.fori_loop(0, tbd // SUBLANES, body, 0)
    for k in range(TOP_K):
        pltpu.make_async_copy(m_ref, xs_hbm.at[pl.ds(0, tbd), :], sem).wait()


def _dispatch_call(dest_flat, pend, pcnt, n_used, m, cap, tm, tbd):
    t, d = m.shape
    ne = pend.shape[0]
    kern = functools.partial(_dispatch_kernel, tbd=tbd, tm=tm, ne=ne, nblk=cap // tm)
    return pl.pallas_call(
        kern,
        out_shape=jax.ShapeDtypeStruct((cap, d), U32),
        grid=(t // tbd,),
        in_specs=[pl.BlockSpec((tbd * TOP_K,), lambda i: (i,), memory_space=pltpu.SMEM),
                  pl.BlockSpec(memory_space=pltpu.SMEM),
                  pl.BlockSpec(memory_space=pltpu.SMEM),
                  pl.BlockSpec(memory_space=pltpu.SMEM),
                  pl.BlockSpec((tbd, d), lambda i: (i, 0))],
        out_specs=pl.BlockSpec(memory_space=pl.ANY),
        scratch_shapes=[pltpu.VMEM((tm, d), U32), pltpu.SemaphoreType.DMA(()), pltpu.SemaphoreType.DMA(())],
        compiler_params=_cparams("arbitrary"),
        name="dispatch",
    )(dest_flat, pend, pcnt, n_used, m)


MOE_SUB = 4


def _row_blocks(nv, tm, compute, fill_zero):
    @pl.when(nv == tm)
    def _():
        compute(0, tm)

    tq = tm // MOE_SUB
    for q in range(MOE_SUB):
        @pl.when((nv < tm) & (nv > q * tq))
        def _():
            compute(q * tq, tq)

        @pl.when(nv <= q * tq)
        def _():
            fill_zero(q * tq, tq)


MOE_WSPLIT = 1


def _moe_up_kernel(be_ref, nu_ref, nv_ref, x_ref, *refs):
    wg_refs = refs[:MOE_WSPLIT]
    wl_refs = refs[MOE_WSPLIT:2 * MOE_WSPLIT]
    bg_ref, bl_ref, a_ref, xb_ref = refs[2 * MOE_WSPLIT:]
    i = pl.program_id(0)
    j = pl.program_id(1)
    half = x_ref.shape[-1]
    tm = x_ref.shape[0]
    th = wg_refs[0].shape[-1]
    nv = nv_ref[i]

    @pl.when((nv > 0) & (j == 0))
    def _():
        hi, lo = _unpack_halves(x_ref[...])
        xb_ref[:, :half] = hi.astype(BF16)
        xb_ref[:, half:] = lo.astype(BF16)

    def compute(r0, nr):
        xb = xb_ref[r0:r0 + nr, :]
        for p in range(MOE_WSPLIT):
            cols = slice(p * th, (p + 1) * th)
            glu = jnp.dot(xb, wg_refs[p][0].astype(BF16), preferred_element_type=F32) + bg_ref[0, :, cols]
            lin = jnp.dot(xb, wl_refs[p][0].astype(BF16), preferred_element_type=F32) + bl_ref[0, :, cols]
            glu = jnp.minimum(glu, SWIGLU_LIMIT)
            lin = jnp.clip(lin, -SWIGLU_LIMIT, SWIGLU_LIMIT)
            a_ref[r0:r0 + nr, cols] = (glu * jax.nn.sigmoid(SWIGLU_ALPHA * glu) * (lin + 1.0)).astype(BF16)

    def fill_zero(r0, nr):
        a_ref[r0:r0 + nr, :] = jnp.zeros((nr, a_ref.shape[1]), BF16)

    _row_blocks(nv, tm, compute, fill_zero)


def _moe_down_kernel(be_ref, nu_ref, nv_ref, a_ref, wa_ref, wb_ref, ba_ref, bb_ref, y_ref):
    i = pl.program_id(1)
    tm = a_ref.shape[0]

    def compute(r0, nr):
        a = a_ref[r0:r0 + nr, :]
        ya = jnp.dot(a, wa_ref[0].astype(BF16), preferred_element_type=F32) + ba_ref[0]
        yb = jnp.dot(a, wb_ref[0].astype(BF16), preferred_element_type=F32) + bb_ref[0]
        y_ref[r0:r0 + nr, :] = _pack_pair(ya, yb)

    def fill_zero(r0, nr):
        y_ref[r0:r0 + nr, :] = jnp.zeros((nr, y_ref.shape[1]), U32)

    _row_blocks(nv_ref[i], tm, compute, fill_zero)


def _moe_call(block_expert, n_used, n_valid, xs, w_gu, b_gu, w_down, b_down, tm, tf, tn):
    cap, half = xs.shape
    d = 2 * half
    ne, _, two_ff = w_gu.shape
    d_ff = two_ff // 2
    nf = d_ff // tf
    nn = half // tn
    nblk = cap // tm
    assert tm % (MOE_SUB * 2 * SUBLANES) == 0

    def keep(i, j, nu, last):
        return jnp.where(i < nu[0], j, last)

    th = tf // MOE_WSPLIT

    def w_piece(chunk0, p):
        return pl.BlockSpec((1, d, th), lambda i, j, be, nu, nv: (
            be[i], 0, (chunk0 + keep(i, j, nu, nf - 1)) * MOE_WSPLIT + p))

    up_spec = pltpu.PrefetchScalarGridSpec(
        num_scalar_prefetch=3,
        grid=(nblk, nf),
        in_specs=([pl.BlockSpec((tm, half), lambda i, j, be, nu, nv: (i, 0))]
                  + [w_piece(0, p) for p in range(MOE_WSPLIT)]
                  + [w_piece(nf, p) for p in range(MOE_WSPLIT)]
                  + [pl.BlockSpec((1, 1, tf), lambda i, j, be, nu, nv: (be[i], 0, keep(i, j, nu, nf - 1))),
                     pl.BlockSpec((1, 1, tf), lambda i, j, be, nu, nv: (be[i], 0, nf + keep(i, j, nu, nf - 1)))]),
        out_specs=pl.BlockSpec((tm, tf), lambda i, j, be, nu, nv: (i, j)),
        scratch_shapes=[pltpu.VMEM((tm, d), BF16)],
    )
    b_gu3 = b_gu.reshape(ne, 1, two_ff)
    act = pl.pallas_call(
        _moe_up_kernel,
        out_shape=jax.ShapeDtypeStruct((cap, d_ff), BF16),
        grid_spec=up_spec,
        compiler_params=_cparams("arbitrary", "arbitrary", vmem=VMEM_LIMIT_MOE_BYTES),
        name="moe_up",
    )(block_expert, n_used, n_valid, xs, *([w_gu] * (2 * MOE_WSPLIT)), b_gu3, b_gu3)

    down_spec = pltpu.PrefetchScalarGridSpec(
        num_scalar_prefetch=3,
        grid=(nn, nblk),
        in_specs=[pl.BlockSpec((tm, d_ff), lambda n, i, be, nu, nv: (i, 0)),
                  pl.BlockSpec((1, d_ff, tn), lambda n, i, be, nu, nv: (be[i], 0, n)),
                  pl.BlockSpec((1, d_ff, tn), lambda n, i, be, nu, nv: (be[i], 0, nn + n)),
                  pl.BlockSpec((1, 1, tn), lambda n, i, be, nu, nv: (be[i], 0, n)),
                  pl.BlockSpec((1, 1, tn), lambda n, i, be, nu, nv: (be[i], 0, nn + n))],
        out_specs=pl.BlockSpec((tm, tn), lambda n, i, be, nu, nv: (i, n)),
    )
    b_d3 = b_down.reshape(ne, 1, d)
    return pl.pallas_call(
        _moe_down_kernel,
        out_shape=jax.ShapeDtypeStruct((cap, half), U32),
        grid_spec=down_spec,
        compiler_params=_cparams("arbitrary", "arbitrary", vmem=VMEM_LIMIT_MOE_BYTES),
        name="moe_down",
    )(block_expert, n_used, n_valid, act, w_down, w_down, b_d3, b_d3)


def _combine_kernel(dest_ref, gate_ref, hx_ref, mod_ref, fg_ref, y_hbm, o_ref, buf, sem, *, tbc, d, nsteps):
    s = pl.program_id(0)
    slot_in = s % 2
    slot_out = (s + 1) % 2
    g2 = mod_ref[0, :, 5 * d:6 * d]
    fg = fg_ref[...]

    def issue(tt):
        r0 = pl.multiple_of(tt * SUBLANES, SUBLANES)
        for k in range(TOP_K):
            tile = buf.at[slot_in, k, pl.ds(r0, SUBLANES), :]
            for r in range(SUBLANES):
                dd = dest_ref[(tt * SUBLANES + r) * TOP_K + k]
                pltpu.make_async_copy(y_hbm.at[pl.ds(dd, 1), :], tile.at[pl.ds(r, 1), :],
                                      sem.at[slot_in]).start(priority=k % 2)

    def reduce_rows(r0, nr):
        gate = gate_ref[pl.ds(r0, nr), :]
        moe_a, moe_b = None, None
        for k in range(TOP_K):
            ya, yb = _unpack_halves(buf[slot_out, k, pl.ds(r0, nr), :])
            gk = gate[:, k:k + 1]
            moe_a = gk * ya if k == 0 else moe_a + gk * ya
            moe_b = gk * yb if k == 0 else moe_b + gk * yb
        moe = jnp.concatenate([moe_a, moe_b], axis=-1)
        hx = hx_ref[pl.ds(r0, nr), :] + g2 * moe
        o_ref[pl.ds(r0, nr), :] = _rms(hx) * fg

    def wait_prev():
        for k in range(TOP_K):
            pltpu.make_async_copy(y_hbm.at[pl.ds(0, tbc), :], buf.at[slot_out, k], sem.at[slot_out]).wait()

    @pl.when(s == 0)
    def _():
        def body(tt, c):
            issue(tt)
            return c
        lax.fori_loop(0, tbc // SUBLANES, body, 0)

    @pl.when((s > 0) & (s < nsteps))
    def _():
        wait_prev()

        def body(tt, c):
            issue(tt)
            reduce_rows(pl.multiple_of(tt * SUBLANES, SUBLANES), SUBLANES)
            return c
        lax.fori_loop(0, tbc // SUBLANES, body, 0, unroll=16 if tbc % (16 * SUBLANES) == 0 else 1)

    @pl.when(s == nsteps)
    def _():
        wait_prev()
        reduce_rows(0, tbc)


def _combine_call(dest_flat, gate, hx1, mod3, rows_per_batch, final_g, y, tbc):
    t, d = hx1.shape
    bpb = rows_per_batch // tbc
    nsteps = t // tbc
    kern = functools.partial(_combine_kernel, tbc=tbc, d=d, nsteps=nsteps)
    prev = lambda s: jnp.maximum(s - 1, 0)
    return pl.pallas_call(
        kern,
        out_shape=jax.ShapeDtypeStruct((t, d), F32),
        grid=(nsteps + 1,),
        in_specs=[pl.BlockSpec((tbc * TOP_K,), lambda s: (jnp.minimum(s, nsteps - 1),), memory_space=pltpu.SMEM),
                  pl.BlockSpec((tbc, TOP_K), lambda s: (prev(s), 0)),
                  pl.BlockSpec((tbc, d), lambda s: (prev(s), 0)),
                  pl.BlockSpec((1, 1, N_MOD * d), lambda s: (prev(s) // bpb, 0, 0)),
                  pl.BlockSpec((1, d), lambda s: (0, 0)),
                  pl.BlockSpec(memory_space=pl.ANY)],
        out_specs=pl.BlockSpec((tbc, d), lambda s: (prev(s), 0)),
        scratch_shapes=[pltpu.VMEM((2, TOP_K, tbc, d // 2), U32), pltpu.SemaphoreType.DMA((2,))],
        compiler_params=_cparams("arbitrary"),
        name="combine",
    )(dest_flat, gate, hx1, mod3, final_g.reshape(1, d), y)


def _tiles(t_tokens, seq):
    tok = 512 if seq % 512 == 0 else seq
    moe_tm = 1024 if t_tokens * TOP_K >= 32 * 1024 else 256
    big = 1024 if seq % 1024 == 0 else tok
    return dict(tok=tok, route=big, moe_tm=moe_tm, disp=big, comb=tok)


def kernel(x, c, ctx, c_ctx, w_mod, b_mod, norm1_g, norm2_g, w_in, lru_conv_w, lru_conv_b, lru_wa, lru_ba,
           lru_wi, lru_bi, lru_lambda, hy_conv_w, hy_conv_b, hy_w1, hy_b1, hy_w2, hy_b2, hy_w3, hy_b3,
           hy_freq, hy_w4, hy_d, gn_lru, gn_hy, w_out, router_w, router_b, exp_w_gu, exp_b_gu,
           exp_w_down, exp_b_down, final_g):
    depth = w_mod.shape[0]
    assert depth == 1, "single-layer stack: the context stream only feeds the latent scan states"
    n_batch, seq, d = x.shape
    ctx_len = ctx.shape[1]
    w_lru = lru_conv_w.shape[-1]
    w_hy = hy_d.shape[-1]
    nb_l, nb_h = w_lru // LANES, w_hy // LANES
    assert lru_wa.shape[2] == nb_l and lru_wa.shape[3] == LANES, "gate blocks must be 128 wide"
    assert seq % (GRID_W * SUBLANES) == 0
    ne = router_w.shape[-1]
    t_tokens = n_batch * seq
    tl = _tiles(t_tokens, seq)
    l = 0

    n_rows = -(-(n_batch + 1) // SUBLANES) * SUBLANES
    cstack = jnp.concatenate([c, c_ctx[None, :], jnp.zeros((n_rows - n_batch - 1, d), F32)], axis=0)
    mod = _mod_call(cstack, w_mod[l], b_mod[l])
    mod3 = mod.reshape(n_rows, 1, N_MOD * d)

    w_in_b = w_in[l].astype(BF16)
    x2d = x.reshape(t_tokens, d)
    bpb = seq // tl["tok"]
    p_x = _inproj_call(x2d, mod3, lambda i: i // bpb, norm1_g[l], w_in_b, tl["tok"])
    ctx_tm = ctx_len if ctx_len <= 512 else 256
    p_c = _inproj_call(ctx.reshape(n_batch * ctx_len, d), mod3, lambda i: n_batch, norm1_g[l],
                       w_in_b[:, :2 * w_lru], ctx_tm)

    lru_args = (lru_conv_w[l], lru_conv_b[l], lru_wa[l], lru_ba[l], lru_wi[l], lru_bi[l], lru_lambda[l])
    _, h_ctx = _lru_call(p_c, nb_l, n_batch, ctx_len, *lru_args, jnp.zeros((n_batch, 2, w_lru), F32))
    y_lru, _ = _lru_call(p_x, nb_l, n_batch, seq, *lru_args, h_ctx)

    rows = seq // GRID_W
    tables = _dft_tables(rows)
    hfilt = _filt_call(seq, w_hy, hy_w1[l], hy_b1[l], hy_w2[l], hy_b2[l], hy_w3[l], hy_b3[l], hy_freq[l], hy_w4[l])
    kf = _spec_call(hfilt, seq, nb_h, tables[0], tables[1])
    cw, cb = hy_conv_w[l], hy_conv_b[l]
    s0 = 2 * nb_l
    z = _hyena_call(p_x, s0, p_x, s0 + nb_h, cw[:, :w_hy], cb[:w_hy], cw[:, w_hy:2 * w_hy], cb[w_hy:2 * w_hy],
                    hy_d[l, 0], kf, 0, tables, n_batch, seq, nb_h, True)
    hy = _hyena_call(z, 0, p_x, s0 + 2 * nb_h, cw[:, :w_hy], cb[:w_hy], cw[:, 2 * w_hy:], cb[2 * w_hy:],
                     hy_d[l, 1], kf, 1, tables, n_batch, seq, nb_h, False)

    hx1, m, logits = _outproj_call(y_lru, hy, x2d, mod3, seq, gn_lru[l], gn_hy[l], norm2_g[l],
                                   w_out[l].astype(BF16), router_w[l], router_b[l], tl["tok"])

    idx, gate, rank, counts = _route_call(logits, tl["route"])
    tm = tl["moe_tm"]
    counts = counts[0]
    pcnt = (counts + tm - 1) // tm * tm
    pend = jnp.cumsum(pcnt)
    pstart = pend - pcnt
    e_ids = jnp.arange(ne, dtype=I32)
    dest = (jnp.sum(jnp.where(idx[:, :, None] == e_ids, pstart.astype(I32), 0), axis=-1) + rank).reshape(-1)
    nblk = t_tokens * TOP_K // tm + ne
    cap = nblk * tm
    n_used = (pend[-1] // tm).astype(I32).reshape(1)
    blk_ids = jnp.arange(nblk, dtype=I32)
    block_expert = jnp.minimum(jnp.sum((pend[None, :] <= (blk_ids * tm)[:, None]).astype(I32), axis=1), ne - 1)
    block_expert = jnp.where(blk_ids < n_used[0], block_expert, block_expert[n_used[0] - 1])
    group_end = (pstart + counts).astype(I32)
    n_valid = jnp.where(blk_ids < n_used[0], jnp.clip(group_end[block_expert] - blk_ids * tm, 0, tm), 0).astype(I32)

    xs = _dispatch_call(dest, pend.astype(I32), pcnt.astype(I32), n_used, m, cap, tm, tl["disp"])
    d_ff = exp_w_down.shape[2]
    tf = next((c for c in (1024, 512, 256) if d_ff % c == 0), d_ff)
    tn = next((c for c in (1024, 512) if (d // 2) % c == 0), d // 2)
    y = _moe_call(block_expert, n_used, n_valid, xs, exp_w_gu[l], exp_b_gu[l], exp_w_down[l], exp_b_down[l],
                  tm, tf, tn)
    out = _combine_call(dest, gate, hx1, mod3, seq, final_g, y, tl["comb"])
    return out.reshape(n_batch, seq, d)
```
